```python
import math
import jax
import jax.numpy as jnp
from jax import lax
import numpy as np

D_MODEL = 1024
BATCH = 8
SEQ = 4096
DEPTH = 2

F32 = jnp.float32
GRID_W = 64
CTX_LEN = 256
EPS = 1e-6
CONV_K = 7
ROPE_THETA = 10000.0
N_BRANCH = 4

MLSTM_HEAD_DIM = 64
MLSTM_WIDTH = D_MODEL // 4
MLSTM_HEADS = MLSTM_WIDTH // MLSTM_HEAD_DIM
MLSTM_CHUNK = 64
S5_WIDTH = D_MODEL // 4
S5_GROUP = 16
S5_GROUPS = S5_WIDTH // S5_GROUP
S5_STATE = 64
NA_HEAD_DIM = 64
NA_WIDTH = D_MODEL // 4
NA_HEADS = NA_WIDTH // NA_HEAD_DIM
NA_KH = 8
NA_KW = 16
SSD_HEAD_DIM = 64
SSD_WIDTH = D_MODEL // 2
SSD_HEADS = SSD_WIDTH // SSD_HEAD_DIM
SSD_GROUPS = 2
SSD_STATE = 128
SSD_CHUNK = 128
FFN_HIDDEN = -(-8 * D_MODEL // (3 * 256)) * 256

IN_LAYOUT = (
    ('mq', MLSTM_WIDTH), ('mk', MLSTM_WIDTH), ('mv', MLSTM_WIDTH), ('mo', MLSTM_WIDTH),
    ('mi', 2 * MLSTM_HEADS), ('mf', 2 * MLSTM_HEADS),
    ('su', S5_WIDTH),
    ('nqkv', 3 * NA_WIDTH),
    ('dz', SSD_WIDTH), ('dx', SSD_WIDTH), ('dB', SSD_GROUPS * SSD_STATE), ('dC', SSD_GROUPS * SSD_STATE),
    ('ddt', 2 * SSD_HEADS),
    ('gate', N_BRANCH * D_MODEL),
)
IN_TOTAL = sum(n for _, n in IN_LAYOUT)

kernel_name = 'hybrid_mlstm_s5_natten_ssd_ctxprefix'


def rmsnorm(x, w):
    x32 = x.astype(F32)
    y = x32 * lax.rsqrt(jnp.mean(x32 * x32, axis=-1, keepdims=True) + EPS)
    return (y * w.astype(F32)).astype(x.dtype)


def dwconv(x, w, b):
    k = w.shape[0]
    y = lax.conv_general_dilated(x, w[:, None, :].astype(x.dtype), (1,), [(k // 2, k // 2)],
                                 dimension_numbers=('NWC', 'WIO', 'NWC'), feature_group_count=x.shape[-1])
    return y + b.astype(x.dtype)


def split_heads(x, n_heads):
    b, t, _ = x.shape
    return x.reshape(b, t, n_heads, -1).transpose(0, 2, 1, 3)


def merge_heads(x):
    b, h, t, d = x.shape
    return x.transpose(0, 2, 1, 3).reshape(b, t, h * d)


def split_in(proj):
    out = {}
    off = 0
    for name, n in IN_LAYOUT:
        out[name] = proj[..., off:off + n]
        off += n
    return out


def axial_rope(x, rows, cols):
    d = x.shape[-1]
    half = d // 2
    nf = half // 2
    inv = ROPE_THETA ** (-jnp.arange(nf, dtype=F32) / nf)

    def rot(xa, pos):
        ang = pos.astype(F32)[:, None] * inv
        cos, sin = jnp.cos(ang), jnp.sin(ang)
        x1 = xa[..., :nf].astype(F32)
        x2 = xa[..., nf:].astype(F32)
        return jnp.concatenate([x1 * cos - x2 * sin, x1 * sin + x2 * cos], axis=-1)

    return jnp.concatenate([rot(x[..., :half], rows), rot(x[..., half:], cols)], axis=-1).astype(x.dtype)


def mlstm_scan(q, k, v, li, lf, state):
    b, h, t, d = q.shape
    L = min(MLSTM_CHUNK, t)
    nc = t // L

    def chunks(a):
        return jnp.moveaxis(a.reshape(a.shape[:2] + (nc, L) + a.shape[3:]), 2, 0)

    lower = jnp.tril(jnp.ones((L, L), dtype=bool))

    def step(carry, inp):
        c_mat, n_vec, m = carry
        qc, kc, vc, lic, lfc = inp
        bcum = jnp.cumsum(lfc, axis=-1)
        log_w = jnp.where(lower, bcum[..., :, None] - bcum[..., None, :] + lic[..., None, :], -jnp.inf)
        inter = bcum + m[..., None]
        m_t = jnp.maximum(inter, jnp.max(log_w, axis=-1))
        s = jnp.einsum('bhtd,bhsd->bhts', qc, kc) * jnp.exp(log_w - m_t[..., None])
        g = jnp.exp(inter - m_t)
        num = jnp.einsum('bhts,bhsd->bhtd', s, vc) + g[..., None] * jnp.einsum('bhtd,bhde->bhte', qc, c_mat)
        den = jnp.sum(s, axis=-1) + g * jnp.einsum('bhtd,bhd->bht', qc, n_vec)
        h_out = num / jnp.maximum(jnp.abs(den), jnp.exp(-m_t))[..., None]
        b_last = bcum[..., -1]
        log_k = b_last[..., None] - bcum + lic
        m_new = jnp.maximum(b_last + m, jnp.max(log_k, axis=-1))
        w_k = jnp.exp(log_k - m_new[..., None])
        decay = jnp.exp(b_last + m - m_new)
        c_new = decay[..., None, None] * c_mat + jnp.einsum('bhs,bhsd,bhse->bhde', w_k, kc, vc)
        n_new = decay[..., None] * n_vec + jnp.einsum('bhs,bhsd->bhd', w_k, kc)
        return (c_new, n_new, m_new), h_out

    state, hs = lax.scan(step, state, (chunks(q), chunks(k), chunks(v), chunks(li), chunks(lf)))
    return jnp.moveaxis(hs, 0, 2).reshape(b, h, t, d), state


def mlstm_zero_state(b):
    h, d = MLSTM_HEADS, MLSTM_HEAD_DIM
    return (jnp.zeros((b, h, d, d), F32), jnp.zeros((b, h, d), F32), jnp.zeros((b, h), F32))


def mlstm_qkv(pp, conv_w, conv_b, pos):
    qk = jax.nn.silu(dwconv(jnp.concatenate([pp['mq'], pp['mk']], axis=-1), conv_w, conv_b))
    q = split_heads(qk[..., :MLSTM_WIDTH], MLSTM_HEADS).astype(F32)
    k = split_heads(qk[..., MLSTM_WIDTH:], MLSTM_HEADS).astype(F32)
    v = split_heads(pp['mv'], MLSTM_HEADS).astype(F32)
    if pos is not None:
        q = axial_rope(q, pos[0], pos[1])
        k = axial_rope(k, pos[0], pos[1])
    return q * MLSTM_HEAD_DIM ** -0.5, k, v


def mlstm_gates(pp, ib, fb, direction):
    sl = slice(direction * MLSTM_HEADS, (direction + 1) * MLSTM_HEADS)
    li = pp['mi'][..., sl].astype(F32) + ib[direction].astype(F32)
    lf = jax.nn.log_sigmoid(pp['mf'][..., sl].astype(F32) + fb[direction].astype(F32))
    return jnp.swapaxes(li, 1, 2), jnp.swapaxes(lf, 1, 2)


def mlstm_readout(h, o_pre, norm_w):
    b, hh, t, d = h.shape
    h = h.transpose(0, 2, 1, 3) * jax.nn.sigmoid(o_pre.astype(F32)).reshape(b, t, hh, d)
    h = h * lax.rsqrt(jnp.mean(h * h, axis=-1, keepdims=True) + EPS)
    return (h.reshape(b, t, hh * d) * norm_w.astype(F32)).astype(o_pre.dtype)


def mlstm_branch(px, pc, conv_w, conv_b, ib, fb, norm_w, pos, with_ctx):
    qx, kx, vx = mlstm_qkv(px, conv_w, conv_b, pos)
    qc, kc, vc = mlstm_qkv(pc, conv_w, conv_b, None)
    hx_dirs, hc_dirs = [], []
    for direction in range(2):
        lix, lfx = mlstm_gates(px, ib, fb, direction)
        lic, lfc = mlstm_gates(pc, ib, fb, direction)
        seq_x = (qx, kx, vx, lix, lfx)
        seq_c = (qc, kc, vc, lic, lfc)
        if direction == 1:
            seq_x = tuple(jnp.flip(a, axis=2) for a in seq_x)
            seq_c = tuple(jnp.flip(a, axis=2) for a in seq_c)
        h_c, state = mlstm_scan(*seq_c, mlstm_zero_state(qc.shape[0]))
        h_x, _ = mlstm_scan(*seq_x, state)
        if direction == 1:
            h_x = jnp.flip(h_x, axis=2)
            h_c = jnp.flip(h_c, axis=2)
        hx_dirs.append(h_x)
        hc_dirs.append(h_c)
    y_x = mlstm_readout(hx_dirs[0] + hx_dirs[1], px['mo'], norm_w)
    y_c = mlstm_readout(hc_dirs[0] + hc_dirs[1], pc['mo'], norm_w) if with_ctx else None
    return y_x, y_c


def s5_discretize(lam_re, lam_im, log_dt, b_re, b_im):
    lam_re = lam_re.astype(F32)
    lam_im = lam_im.astype(F32)
    dt = jnp.exp(log_dt.astype(F32))[:, None]
    mag = jnp.exp(lam_re * dt)
    a_re = mag * jnp.cos(lam_im * dt)
    a_im = mag * jnp.sin(lam_im * dt)
    den = lam_re * lam_re + lam_im * lam_im
    nr = a_re - 1.0
    coef_re = (nr * lam_re + a_im * lam_im) / den
    coef_im = (a_im * lam_re - nr * lam_im) / den
    b_re = b_re.astype(F32)
    b_im = b_im.astype(F32)
    bb_re = coef_re[..., None] * b_re - coef_im[..., None] * b_im
    bb_im = coef_re[..., None] * b_im + coef_im[..., None] * b_re
    return a_re, a_im, bb_re, bb_im


def s5_scan(u, a_re, a_im, bb_re, bb_im, h0_re, h0_im):
    t = u.shape[1]
    bu_re = jnp.einsum('gpc,btgc->tbgp', bb_re, u)
    bu_im = jnp.einsum('gpc,btgc->tbgp', bb_im, u)
    bu_re = bu_re.at[0].add(a_re * h0_re - a_im * h0_im)
    bu_im = bu_im.at[0].add(a_re * h0_im + a_im * h0_re)
    shape = (t, 1) + a_re.shape
    elems = (jnp.broadcast_to(a_re, shape), jnp.broadcast_to(a_im, shape), bu_re, bu_im)

    def combine(e1, e2):
        a1r, a1i, b1r, b1i = e1
        a2r, a2i, b2r, b2i = e2
        return (a2r * a1r - a2i * a1i, a2r * a1i + a2i * a1r,
                a2r * b1r - a2i * b1i + b2r, a2r * b1i + a2i * b1r + b2i)

    _, _, h_re, h_im = lax.associative_scan(combine, elems, axis=0)
    return h_re, h_im


def s5_readout(h_re, h_im, c_re, c_im):
    return (jnp.einsum('gcp,tbgp->btgc', c_re.astype(F32), h_re)
            - jnp.einsum('gcp,tbgp->btgc', c_im.astype(F32), h_im))


def s5_output(y, u, d_skip, glu_w, dtype):
    b, t = y.shape[:2]
    y = (y + d_skip.astype(F32).reshape(S5_GROUPS, S5_GROUP) * u).reshape(b, t, S5_WIDTH)
    y = jax.nn.gelu(y).astype(dtype)
    ab = y @ glu_w
    return ab[..., :S5_WIDTH] * jax.nn.sigmoid(ab[..., S5_WIDTH:])


def s5_branch(px, pc, lam_re, lam_im, log_dt, b_re, b_im, c_re, c_im, d_skip, glu_w, with_ctx):
    def groups(u):
        b, t, _ = u.shape
        return u.astype(F32).reshape(b, t, S5_GROUPS, S5_GROUP)

    ux, uc = groups(px['su']), groups(pc['su'])
    zeros = jnp.zeros((ux.shape[0], S5_GROUPS, S5_STATE), F32)
    yx, yc = [], []
    for direction in range(2):
        a_re, a_im, bb_re, bb_im = s5_discretize(lam_re[direction], lam_im[direction], log_dt[direction], b_re, b_im)
        uxd = ux if direction == 0 else jnp.flip(ux, axis=1)
        ucd = uc if direction == 0 else jnp.flip(uc, axis=1)
        hc_re, hc_im = s5_scan(ucd, a_re, a_im, bb_re, bb_im, zeros, zeros)
        hx_re, hx_im = s5_scan(uxd, a_re, a_im, bb_re, bb_im, hc_re[-1], hc_im[-1])
        y_x = s5_readout(hx_re, hx_im, c_re, c_im)
        yx.append(y_x if direction == 0 else jnp.flip(y_x, axis=1))
        if with_ctx:
            y_c = s5_readout(hc_re, hc_im, c_re, c_im)
            yc.append(y_c if direction == 0 else jnp.flip(y_c, axis=1))
    out_x = s5_output(yx[0] + yx[1], ux, d_skip, glu_w, px['su'].dtype)
    out_c = s5_output(yc[0] + yc[1], uc, d_skip, glu_w, pc['su'].dtype) if with_ctx else None
    return out_x, out_c


def na_branch(px, pc, rpb, with_ctx):
    qx, kx, vx = [split_heads(a, NA_HEADS) for a in jnp.split(px['nqkv'], 3, axis=-1)]
    qc, kc, vc = [split_heads(a, NA_HEADS) for a in jnp.split(pc['nqkv'], 3, axis=-1)]
    b, h, t, d = qx.shape
    rows = t // GRID_W
    kh = min(NA_KH, rows)
    scale = d ** -0.5
    qg = (qx * scale).reshape(b, h, rows, GRID_W, d)
    kg = kx.reshape(b, h, rows, GRID_W, d)
    vg = vx.reshape(b, h, rows, GRID_W, d)
    r = jnp.arange(rows)
    row_idx = jnp.clip(r - kh // 2, 0, rows - kh)[:, None] + jnp.arange(kh)[None, :]
    kb = kg[:, :, row_idx]
    vb = vg[:, :, row_idx]
    col = jnp.arange(GRID_W)
    col0 = jnp.clip(col - NA_KW // 2, 0, GRID_W - NA_KW)
    in_win = (col[None, :] >= col0[:, None]) & (col[None, :] < col0[:, None] + NA_KW)
    dr = row_idx - r[:, None] + (NA_KH - 1)
    dc = jnp.clip(col[None, :] - col[:, None], -(NA_KW - 1), NA_KW - 1) + (NA_KW - 1)
    bias = rpb.astype(F32)[:, dr][..., dc].transpose(0, 1, 3, 2, 4)
    s_lat = jnp.einsum('bhrqd,bhrikd->bhrqik', qg, kb).astype(F32) + bias
    s_lat = jnp.where(in_win[:, None, :], s_lat, -jnp.inf)
    s_ctx = jnp.einsum('bhrqd,bhjd->bhrqj', qg, kc).astype(F32)
    n_lat = kh * GRID_W
    p = jax.nn.softmax(jnp.concatenate([s_lat.reshape(b, h, rows, GRID_W, n_lat), s_ctx], axis=-1), axis=-1)
    p = p.astype(vx.dtype)
    o = (jnp.einsum('bhrqik,bhrikd->bhrqd', p[..., :n_lat].reshape(b, h, rows, GRID_W, kh, GRID_W), vb)
         + jnp.einsum('bhrqj,bhjd->bhrqd', p[..., n_lat:], vc))
    y_x = merge_heads(o.reshape(b, h, t, d))
    y_c = None
    if with_ctx:
        s = jnp.einsum('bhid,bhjd->bhij', qc * scale, kc).astype(F32)
        y_c = merge_heads(jnp.einsum('bhij,bhjd->bhid', jax.nn.softmax(s, axis=-1).astype(vc.dtype), vc))
    return y_x, y_c


def segsum(a):
    t = a.shape[-1]
    x = jnp.broadcast_to(a[..., None], a.shape + (t,))
    x = jnp.where(jnp.tril(jnp.ones((t, t), dtype=bool), -1), x, 0.0)
    s = jnp.cumsum(x, axis=-2)
    return jnp.where(jnp.tril(jnp.ones((t, t), dtype=bool)), s, -jnp.inf)


def ssd_scan(xdt, a, bh, ch, h0):
    bsz, t, h, p = xdt.shape
    n = bh.shape[-1]
    L = min(SSD_CHUNK, t)
    nc = t // L
    xdt = xdt.reshape(bsz, nc, L, h, p)
    bh = bh.reshape(bsz, nc, L, h, n)
    ch = ch.reshape(bsz, nc, L, h, n)
    a = a.reshape(bsz, nc, L, h).transpose(0, 3, 1, 2)
    a_cs = jnp.cumsum(a, axis=-1)
    scores = jnp.einsum('bclhn,bcshn->bhcls', ch, bh) * jnp.exp(segsum(a))
    y_diag = jnp.einsum('bhcls,bcshp->bclhp', scores, xdt)
    decay_states = jnp.exp(a_cs[..., -1:] - a_cs)
    states = jnp.einsum('bclhn,bhcl,bclhp->bchpn', bh, decay_states, xdt)
    states = jnp.concatenate([h0[:, None], states], axis=1)
    decay_chunk = jnp.exp(segsum(jnp.pad(a_cs[..., -1], ((0, 0), (0, 0), (1, 0)))))
    new_states = jnp.einsum('bhzc,bchpn->bzhpn', decay_chunk, states)
    y_off = jnp.einsum('bclhn,bchpn,bhcl->bclhp', ch, new_states[:, :-1], jnp.exp(a_cs))
    return (y_diag + y_off).reshape(bsz, t, h, p), new_states[:, -1]


def ssd_inputs(pp, conv_w, conv_b):
    xbc = jax.nn.silu(dwconv(jnp.concatenate([pp['dx'], pp['dB'], pp['dC']], axis=-1), conv_w, conv_b)).astype(F32)
    b, t, _ = xbc.shape
    gn = SSD_GROUPS * SSD_STATE
    rep = SSD_HEADS // SSD_GROUPS
    xs = xbc[..., :SSD_WIDTH].reshape(b, t, SSD_HEADS, SSD_HEAD_DIM)
    bm = jnp.repeat(xbc[..., SSD_WIDTH:SSD_WIDTH + gn].reshape(b, t, SSD_GROUPS, SSD_STATE), rep, axis=2)
    cm = jnp.repeat(xbc[..., SSD_WIDTH + gn:].reshape(b, t, SSD_GROUPS, SSD_STATE), rep, axis=2)
    return xs, bm, cm


def ssd_branch(px, pc, conv_w, conv_b, a_log, dt_bias, d_skip, norm_w, with_ctx):
    xs_x, b_x, c_x = ssd_inputs(px, conv_w, conv_b)
    xs_c, b_c, c_c = ssd_inputs(pc, conv_w, conv_b)
    h0 = jnp.zeros((xs_x.shape[0], SSD_HEADS, SSD_HEAD_DIM, SSD_STATE), F32)
    yx, yc = [], []
    for direction in range(2):
        sl = slice(direction * SSD_HEADS, (direction + 1) * SSD_HEADS)
        a = -jnp.exp(a_log[direction].astype(F32))
        dt_x = jax.nn.softplus(px['ddt'][..., sl].astype(F32) + dt_bias[direction].astype(F32))
        dt_c = jax.nn.softplus(pc['ddt'][..., sl].astype(F32) + dt_bias[direction].astype(F32))
        seq_x = (xs_x * dt_x[..., None], dt_x * a, b_x, c_x)
        seq_c = (xs_c * dt_c[..., None], dt_c * a, b_c, c_c)
        if direction == 1:
            seq_x = tuple(jnp.flip(s, axis=1) for s in seq_x)
            seq_c = tuple(jnp.flip(s, axis=1) for s in seq_c)
        y_c, state = ssd_scan(*seq_c, h0)
        y_x, _ = ssd_scan(*seq_x, state)
        if direction == 1:
            y_x = jnp.flip(y_x, axis=1)
            y_c = jnp.flip(y_c, axis=1)
        yx.append(y_x)
        yc.append(y_c)

    def finish(y, xs, z):
        b, t = y.shape[:2]
        y = (y + d_skip.astype(F32)[:, None] * xs).reshape(b, t, SSD_WIDTH)
        return rmsnorm(y * jax.nn.silu(z.astype(F32)), norm_w).astype(z.dtype)

    out_x = finish(yx[0] + yx[1], xs_x, px['dz'])
    out_c = finish(yc[0] + yc[1], xs_c, pc['dz']) if with_ctx else None
    return out_x, out_c


def gated_merge(gate_pre, ys, w_br, w_out):
    g = jax.nn.sigmoid(gate_pre.astype(F32)).astype(gate_pre.dtype)
    m = g[..., :D_MODEL] * (ys[0] @ w_br[0])
    for i in range(1, N_BRANCH):
        m = m + g[..., i * D_MODEL:(i + 1) * D_MODEL] * (ys[i] @ w_br[i])
    return m @ w_out


def mixer(hx, hc, p, pos, with_ctx):
    px = split_in(hx @ p['w_in'])
    pc = split_in(hc @ p['w_in'])
    ya_x, ya_c = mlstm_branch(px, pc, p['mlstm_conv_w'], p['mlstm_conv_b'], p['mlstm_ib'], p['mlstm_fb'],
                              p['mlstm_norm_w'], pos, with_ctx)
    yb_x, yb_c = s5_branch(px, pc, p['s5_lam_re'], p['s5_lam_im'], p['s5_log_dt'], p['s5_b_re'], p['s5_b_im'],
                           p['s5_c_re'], p['s5_c_im'], p['s5_d'], p['s5_glu_w'], with_ctx)
    yc_x, yc_c = na_branch(px, pc, p['na_rpb'], with_ctx)
    yd_x, yd_c = ssd_branch(px, pc, p['ssd_conv_w'], p['ssd_conv_b'], p['ssd_a_log'], p['ssd_dt_bias'],
                            p['ssd_d'], p['ssd_norm_w'], with_ctx)
    w_br = (p['w_branch_a'], p['w_branch_b'], p['w_branch_c'], p['w_branch_d'])
    out_x = gated_merge(px['gate'], (ya_x, yb_x, yc_x, yd_x), w_br, p['w_out'])
    out_c = gated_merge(pc['gate'], (ya_c, yb_c, yc_c, yd_c), w_br, p['w_out']) if with_ctx else None
    return out_x, out_c


def adaln(cvec, w, b):
    m = jax.nn.silu(cvec) @ w + b
    return jnp.split(m, 6, axis=-1)


def swiglu(h, w_in, w_out):
    ab = h @ w_in
    return (jax.nn.silu(ab[..., :FFN_HIDDEN]) * ab[..., FFN_HIDDEN:]) @ w_out


def setup_inputs(seed: int = 0) -> dict:
    key = jax.random.key(seed)
    ks = iter(jax.random.split(key, 48))
    L, D = DEPTH, D_MODEL
    G, P = S5_GROUPS, S5_STATE

    def nrm(shape, scale=1.0):
        return scale * jax.random.normal(next(ks), shape, F32)

    def gain(shape):
        return 1.0 + 0.01 * jax.random.normal(next(ks), shape, F32)

    def unif(shape, lo, hi):
        return jax.random.uniform(next(ks), shape, F32, lo, hi)

    x = nrm((BATCH, SEQ, D))
    c = nrm((BATCH, D))
    ctx = nrm((BATCH, CTX_LEN, D))
    c_ctx = nrm((D,))
    ada_w = nrm((L, D, 6 * D), 0.5 * D ** -0.5)
    ada_b = nrm((L, 6 * D), 0.02)
    norm1_w = gain((L, D))
    norm2_w = gain((L, D))
    w_in = nrm((L, D, IN_TOTAL), D ** -0.5)
    mlstm_conv_w = nrm((L, CONV_K, 2 * MLSTM_WIDTH), CONV_K ** -0.5)
    mlstm_conv_b = nrm((L, 2 * MLSTM_WIDTH), 0.02)
    mlstm_ib = nrm((L, 2, MLSTM_HEADS), 0.1)
    mlstm_fb = jnp.linspace(3.0, 6.0, MLSTM_HEADS, dtype=F32) + nrm((L, 2, MLSTM_HEADS), 0.1)
    mlstm_norm_w = gain((L, MLSTM_WIDTH))
    s5_lam_re = -0.5 + nrm((L, 2, G, P), 0.01)
    s5_lam_im = math.pi * jnp.arange(P, dtype=F32) + nrm((L, 2, G, P), 0.01)
    s5_log_dt = unif((L, 2, G), math.log(1e-3), math.log(1e-1))
    s5_b_re = nrm((L, G, P, S5_GROUP), (2 * S5_GROUP) ** -0.5)
    s5_b_im = nrm((L, G, P, S5_GROUP), (2 * S5_GROUP) ** -0.5)
    s5_c_re = nrm((L, G, S5_GROUP, P), P ** -0.5)
    s5_c_im = nrm((L, G, S5_GROUP, P), P ** -0.5)
    s5_d = nrm((L, S5_WIDTH))
    s5_glu_w = nrm((L, S5_WIDTH, 2 * S5_WIDTH), S5_WIDTH ** -0.5)
    na_rpb = nrm((L, NA_HEADS, 2 * NA_KH - 1, 2 * NA_KW - 1), 0.1)
    conv_ch = SSD_WIDTH + 2 * SSD_GROUPS * SSD_STATE
    ssd_conv_w = nrm((L, CONV_K, conv_ch), CONV_K ** -0.5)
    ssd_conv_b = nrm((L, conv_ch), 0.02)
    ssd_a_log = jnp.log(unif((L, 2, SSD_HEADS), 1.0, 16.0))
    dt0 = jnp.exp(unif((L, 2, SSD_HEADS), math.log(1e-3), math.log(1e-1)))
    ssd_dt_bias = dt0 + jnp.log(-jnp.expm1(-dt0))
    ssd_d = 1.0 + nrm((L, SSD_HEADS), 0.1)
    ssd_norm_w = gain((L, SSD_WIDTH))
    w_branch_a = nrm((L, MLSTM_WIDTH, D), MLSTM_WIDTH ** -0.5)
    w_branch_b = nrm((L, S5_WIDTH, D), S5_WIDTH ** -0.5)
    w_branch_c = nrm((L, NA_WIDTH, D), NA_WIDTH ** -0.5)
    w_branch_d = nrm((L, SSD_WIDTH, D), SSD_WIDTH ** -0.5)
    w_out = nrm((L, D, D), D ** -0.5)
    ffn_w_in = nrm((L, D, 2 * FFN_HIDDEN), D ** -0.5)
    ffn_w_out = nrm((L, FFN_HIDDEN, D), FFN_HIDDEN ** -0.5)
    final_norm_w = gain((D,))
    return {
        'x': x, 'c': c, 'ctx': ctx, 'c_ctx': c_ctx, 'ada_w': ada_w, 'ada_b': ada_b,
        'norm1_w': norm1_w, 'norm2_w': norm2_w, 'w_in': w_in,
        'mlstm_conv_w': mlstm_conv_w, 'mlstm_conv_b': mlstm_conv_b, 'mlstm_ib': mlstm_ib, 'mlstm_fb': mlstm_fb,
        'mlstm_norm_w': mlstm_norm_w,
        's5_lam_re': s5_lam_re, 's5_lam_im': s5_lam_im, 's5_log_dt': s5_log_dt, 's5_b_re': s5_b_re,
        's5_b_im': s5_b_im, 's5_c_re': s5_c_re, 's5_c_im': s5_c_im, 's5_d': s5_d, 's5_glu_w': s5_glu_w,
        'na_rpb': na_rpb,
        'ssd_conv_w': ssd_conv_w, 'ssd_conv_b': ssd_conv_b, 'ssd_a_log': ssd_a_log, 'ssd_dt_bias': ssd_dt_bias,
        'ssd_d': ssd_d, 'ssd_norm_w': ssd_norm_w,
        'w_branch_a': w_branch_a, 'w_branch_b': w_branch_b, 'w_branch_c': w_branch_c, 'w_branch_d': w_branch_d,
        'w_out': w_out, 'ffn_w_in': ffn_w_in, 'ffn_w_out': ffn_w_out, 'final_norm_w': final_norm_w,
    }


def reference(x, c, ctx, c_ctx, ada_w, ada_b, norm1_w, norm2_w, w_in,
              mlstm_conv_w, mlstm_conv_b, mlstm_ib, mlstm_fb, mlstm_norm_w,
              s5_lam_re, s5_lam_im, s5_log_dt, s5_b_re, s5_b_im, s5_c_re, s5_c_im, s5_d, s5_glu_w,
              na_rpb,
              ssd_conv_w, ssd_conv_b, ssd_a_log, ssd_dt_bias, ssd_d, ssd_norm_w,
              w_branch_a, w_branch_b, w_branch_c, w_branch_d, w_out, ffn_w_in, ffn_w_out, final_norm_w):
    t = x.shape[1]
    tok = jnp.arange(t)
    pos = (tok // GRID_W, tok % GRID_W)
    for l in range(DEPTH):
        with_ctx = l < DEPTH - 1
        p = {
            'w_in': w_in[l], 'mlstm_conv_w': mlstm_conv_w[l], 'mlstm_conv_b': mlstm_conv_b[l],
            'mlstm_ib': mlstm_ib[l], 'mlstm_fb': mlstm_fb[l], 'mlstm_norm_w': mlstm_norm_w[l],
            's5_lam_re': s5_lam_re[l], 's5_lam_im': s5_lam_im[l], 's5_log_dt': s5_log_dt[l],
            's5_b_re': s5_b_re[l], 's5_b_im': s5_b_im[l], 's5_c_re': s5_c_re[l], 's5_c_im': s5_c_im[l],
            's5_d': s5_d[l], 's5_glu_w': s5_glu_w[l], 'na_rpb': na_rpb[l],
            'ssd_conv_w': ssd_conv_w[l], 'ssd_conv_b': ssd_conv_b[l], 'ssd_a_log': ssd_a_log[l],
            'ssd_dt_bias': ssd_dt_bias[l], 'ssd_d': ssd_d[l], 'ssd_norm_w': ssd_norm_w[l],
            'w_branch_a': w_branch_a[l], 'w_branch_b': w_branch_b[l], 'w_branch_c': w_branch_c[l],
            'w_branch_d': w_branch_d[l], 'w_out': w_out[l],
        }
        sh1, sc1, g1, sh2, sc2, g2 = [m[:, None, :] for m in adaln(c, ada_w[l], ada_b[l])]
        csh1, csc1, cg1, csh2, csc2, cg2 = adaln(c_ctx, ada_w[l], ada_b[l])
        hx = rmsnorm(x, norm1_w[l]) * (1 + sc1) + sh1
        hc = rmsnorm(ctx, norm1_w[l]) * (1 + csc1) + csh1
        mx, mc = mixer(hx, hc, p, pos, with_ctx)
        x = x + g1 * mx
        hx = rmsnorm(x, norm2_w[l]) * (1 + sc2) + sh2
        x = x + g2 * swiglu(hx, ffn_w_in[l], ffn_w_out[l])
        if with_ctx:
            ctx = ctx + cg1 * mc
            hc = rmsnorm(ctx, norm2_w[l]) * (1 + csc2) + csh2
            ctx = ctx + cg2 * swiglu(hc, ffn_w_in[l], ffn_w_out[l])
    return rmsnorm(x, final_norm_w)
```

```python
import functools
import math

import jax
import jax.numpy as jnp
from jax import lax
from jax.experimental import pallas as pl
from jax.experimental.pallas import tpu as pltpu

F32 = jnp.float32
MXU_DTYPE = jnp.bfloat16
HIGHEST = lax.Precision.HIGHEST

GRID_W = 64
EPS = 1e-6
CONV_K = 7
ROPE_THETA = 10000.0
HEAD_DIM = 64
MLSTM_HEADS = 4
S5_GROUPS = 16
S5_GROUP = 16
S5_STATE = 64
NA_HEADS = 4
NA_KH = 8
NA_KW = 16
SSD_HEADS = 8
SSD_GROUPS = 2
SSD_STATE = 128

TM = 256
HALO = 8
S5_SUB = 32
NEG = -1e30
VMEM_LIMIT = 56 * 1024 * 1024

W_QK, W_V, W_O, W_SMALL, W_SU, W_NQKV, W_DZ, W_DXBC = 512, 256, 256, 128, 256, 768, 512, 1024
PROJ_WIDTHS = (W_QK, W_V, W_O, W_SMALL, W_SU, W_NQKV, W_DZ, W_DXBC)
L_MI, L_MF, L_DDT = 0, 8, 16


def _mm(a, b):
    return jnp.dot(a.astype(MXU_DTYPE), b.astype(MXU_DTYPE), preferred_element_type=F32)


def _mm_nt(a, b):
    return lax.dot_general(a.astype(MXU_DTYPE), b.astype(MXU_DTYPE), (((1,), (1,)), ((), ())),
                           preferred_element_type=F32)


def _mm_tn(a, b):
    return lax.dot_general(a.astype(MXU_DTYPE), b.astype(MXU_DTYPE), (((0,), (0,)), ((), ())),
                           preferred_element_type=F32)


def _mm_f32(a, b):
    return jnp.dot(a, b, preferred_element_type=F32, precision=HIGHEST)


def _sigmoid(x):
    return 1.0 / (1.0 + jnp.exp(-x))


def _silu(x):
    return x * _sigmoid(x)


def _softplus(x):
    return jnp.maximum(x, 0.0) + jnp.log(1.0 + jnp.exp(-jnp.abs(x)))


def _iota(shape, dim):
    return lax.broadcasted_iota(jnp.int32, shape, dim)


def _head_of_lane(width):
    return jnp.right_shift(_iota((1, width), 1), 6)


def _expand_heads(cols, width):
    head = _head_of_lane(width)
    out = jnp.broadcast_to(cols[0], (cols[0].shape[0], width))
    for h in range(1, len(cols)):
        out = jnp.where(head == h, cols[h], out)
    return out


def _tile_of(d, j, nt):
    return jnp.where(d == 0, j, jnp.where(j == 0, 0, nt - j))


def _norm_mod(x, w, scale, shift):
    y = x * lax.rsqrt(jnp.mean(x * x, axis=-1, keepdims=True) + EPS) * w
    return y * (1.0 + scale) + shift


def _tri_masks(rev, n):
    r = _iota((n, n), 0)
    c = _iota((n, n), 1)
    diff = (c - r) * jnp.where(rev, -1, 1)
    return diff <= 0, diff >= 0


def _conv_silu(cur, prev, nxt, w_ref, b_ref, prev_ok, next_ok):
    n = cur.shape[0]
    prev = jnp.where(prev_ok, prev, 0.0)
    nxt = jnp.where(next_ok, nxt, 0.0)
    ext = jnp.concatenate([prev, cur, nxt], axis=0)
    rows = n + 2 * HALO
    acc = b_ref[...] + w_ref[CONV_K // 2:CONV_K // 2 + 1, :] * cur
    for k in range(CONV_K):
        if k == CONV_K // 2:
            continue
        shifted = pltpu.roll(ext, (CONV_K // 2 - k) % rows, 0)
        acc = acc + w_ref[k:k + 1, :] * shifted[HALO:HALO + n, :]
    return _silu(acc)


def _adaln_kernel(c_ref, w_ref, b_ref, o_ref):
    o_ref[...] = _mm_f32(_silu(c_ref[...]), w_ref[...]) + b_ref[...]


def _adaln(cc, ada_w, ada_b):
    depth, d, n = ada_w.shape
    tn = 768
    return pl.pallas_call(
        _adaln_kernel,
        grid=(depth, n // tn),
        in_specs=[pl.BlockSpec(cc.shape, lambda l, i: (0, 0)),
                  pl.BlockSpec((None, d, tn), lambda l, i: (l, 0, i)),
                  pl.BlockSpec((None, 1, tn), lambda l, i: (l, 0, i))],
        out_specs=pl.BlockSpec((None, cc.shape[0], tn), lambda l, i: (l, 0, i)),
        out_shape=jax.ShapeDtypeStruct((depth, cc.shape[0], n), F32),
        name="adaln",
    )(cc, ada_w, ada_b.reshape(depth, 1, n))


def _proj_kernel(x_ref, mod_ref, nw_ref, w_ref, o_qk, o_v, o_o, o_small, o_small_t, o_su, o_nqkv, o_dz, o_dxbc):
    h = _norm_mod(x_ref[...], nw_ref[...], mod_ref[1:2, :], mod_ref[0:1, :]).astype(MXU_DTYPE)
    off = 0
    for ref, n in zip((o_qk, o_v, o_o, o_small, o_su, o_nqkv, o_dz, o_dxbc), PROJ_WIDTHS):
        res = jnp.dot(h, w_ref[:, off:off + n], preferred_element_type=F32)
        ref[...] = res
        if ref is o_small:
            o_small_t[...] = res.T
        off += n


def _proj(seq, mod, norm_w, w_proj):
    b, s, d = seq.shape
    nt = s // TM
    n_all = sum(PROJ_WIDTHS)

    def tok(width):
        return pl.BlockSpec((None, TM, width), lambda bi, j: (bi, j, 0))

    def shape(width):
        return jax.ShapeDtypeStruct((b, s, width), F32)

    return pl.pallas_call(
        _proj_kernel,
        grid=(b, nt),
        in_specs=[tok(d),
                  pl.BlockSpec((None, None, 6, d), lambda bi, j: (bi, jnp.minimum(j, 1), 0, 0)),
                  pl.BlockSpec((1, d), lambda bi, j: (0, 0)),
                  pl.BlockSpec((d, n_all), lambda bi, j: (0, 0), pipeline_mode=pl.Buffered(1))],
        out_specs=[tok(W_QK), tok(W_V), tok(W_O), tok(W_SMALL),
                   pl.BlockSpec((None, W_SMALL, TM), lambda bi, j: (bi, 0, j)),
                   pl.BlockSpec((TM, W_SU), lambda bi, j: (j, bi)),
                   tok(W_NQKV), tok(W_DZ), tok(W_DXBC)],
        out_shape=[shape(W_QK), shape(W_V), shape(W_O), shape(W_SMALL),
                   jax.ShapeDtypeStruct((b, W_SMALL, s), F32),
                   jax.ShapeDtypeStruct((s, b * W_SU), F32),
                   shape(W_NQKV), shape(W_DZ), shape(W_DXBC)],
        compiler_params=pltpu.CompilerParams(vmem_limit_bytes=VMEM_LIMIT),
        name="proj",
    )(seq, mod, norm_w, w_proj)


def _halo_specs(width, s, nt):
    per = TM // HALO
    last = s // HALO - 1
    cur = pl.BlockSpec((None, TM, width), lambda d, j, bi: (bi, _tile_of(d, j, nt), 0))
    prev = pl.BlockSpec((None, HALO, width),
                        lambda d, j, bi: (bi, jnp.maximum(_tile_of(d, j, nt) * per - 1, 0), 0))
    nxt = pl.BlockSpec((None, HALO, width),
                       lambda d, j, bi: (bi, jnp.minimum((_tile_of(d, j, nt) + 1) * per, last), 0))
    return cur, prev, nxt


def _halo_valid(t, nt):
    return t > 1, (t >= 1) & (t < nt - 1)


def _mlstm_kernel(qk_ref, qkp_ref, qkn_ref, cw_ref, cb_ref, cos_ref, sin_ref, v_ref, sm_ref, smt_ref,
                  gb_ref, gbt_ref, h_ref, c_scr, n_scr, m_scr, *, nt):
    d = pl.program_id(0)
    j = pl.program_id(1)
    b = pl.program_id(2)
    t = _tile_of(d, j, nt)
    rev = d == 1
    width = MLSTM_HEADS * HEAD_DIM

    @pl.when(j == 0)
    def _():
        c_scr[b] = jnp.zeros(c_scr.shape[1:], F32)
        n_scr[b] = jnp.zeros(n_scr.shape[1:], F32)
        m_scr[b] = jnp.zeros(m_scr.shape[1:], F32)

    prev_ok, next_ok = _halo_valid(t, nt)
    qk = _conv_silu(qk_ref[...], qkp_ref[...], qkn_ref[...], cw_ref, cb_ref, prev_ok, next_ok)
    first = jnp.bitwise_and(_iota((1, 2 * width), 1), 31) < 16
    partner = jnp.where(first, pltpu.roll(qk, 2 * width - 16, 1), pltpu.roll(qk, 16, 1))
    qk = qk * cos_ref[...] + partner * sin_ref[...]
    q = qk[:, :width]
    k = qk[:, width:]
    v = v_ref[...]

    mask, mask_t = _tri_masks(rev, TM)
    g_all = sm_ref[...] + gb_ref[...]
    lf_all = -_softplus(-g_all)
    bcum_all = _mm_f32(mask.astype(F32), lf_all)
    g_t = smt_ref[0:16, :] + gbt_ref[...]
    li_t = g_t[0:8, :]
    bcum_t = _mm_f32(-_softplus(-g_t[8:16, :]), mask_t.astype(F32))

    c_old = c_scr[b]
    n_old = n_scr[b, 0:1, :]
    m_old = m_scr[b, 0:1, :]
    head = _head_of_lane(width)

    num = jnp.zeros((TM, width), F32)
    g_cols, mt_cols, den_cols, wk_cols, decay_cols, mnew_cols = [], [], [], [], [], []
    for h in range(MLSTM_HEADS):
        def pick(arr_f, arr_b):
            return jnp.where(rev, arr_b, arr_f)
        li_row = pick(li_t[h:h + 1, :], li_t[4 + h:5 + h, :])
        bc_row = pick(bcum_t[h:h + 1, :], bcum_t[4 + h:5 + h, :])
        li_col = pick(g_all[:, L_MI + h:L_MI + h + 1], g_all[:, L_MI + 4 + h:L_MI + 5 + h])
        bc_col = pick(bcum_all[:, L_MF + h:L_MF + h + 1], bcum_all[:, L_MF + 4 + h:L_MF + 5 + h])
        m_prev = m_old[:, h * HEAD_DIM:h * HEAD_DIM + 1]

        log_w = jnp.where(mask, bc_col - bc_row + li_row, NEG)
        inter = bc_col + m_prev
        m_t = jnp.maximum(inter, jnp.max(log_w, axis=1, keepdims=True))
        dm = jnp.exp(log_w - m_t)
        sm = _mm_nt(jnp.where(head == h, q, 0.0), k) * dm
        num = jnp.where(head == h, _mm(sm, v), num)
        g_cols.append(jnp.exp(inter - m_t))
        mt_cols.append(m_t)
        den_cols.append(jnp.sum(sm, axis=1, keepdims=True))

        b_last = pick(bc_col[TM - 1:TM, :], bc_col[0:1, :])
        log_k = b_last - bc_col + li_col
        m_new = jnp.maximum(b_last + m_prev, jnp.max(log_k, axis=0, keepdims=True))
        wk_cols.append(jnp.exp(log_k - m_new))
        decay_cols.append(jnp.exp(b_last + m_prev - m_new))
        mnew_cols.append(m_new)

    g_full = _expand_heads(g_cols, width)
    mt_full = _expand_heads(mt_cols, width)
    same_head = (jnp.right_shift(_iota((width, width), 0), 6) == jnp.right_shift(_iota((width, width), 1), 6))
    qn = _mm_f32(q * n_old, same_head.astype(F32))
    num = num + g_full * _mm(q, c_old)
    den = _expand_heads(den_cols, width) + g_full * qn
    h_ref[...] = num / jnp.maximum(jnp.abs(den), jnp.exp(-mt_full))

    kw = k * _expand_heads(wk_cols, width)
    decay = _expand_heads(decay_cols, width)
    c_scr[b] = c_old * decay + jnp.where(same_head, _mm_tn(kw, v), 0.0)
    n_scr[b, 0:1, :] = n_old * decay + jnp.sum(kw, axis=0, keepdims=True)
    m_scr[b, 0:1, :] = _expand_heads(mnew_cols, width)


def _mlstm(qk, v, small, small_t, conv_w, conv_b, cos_t, sin_t, gbias, gbias_t):
    b, s, _ = qk.shape
    nt = s // TM
    width = MLSTM_HEADS * HEAD_DIM
    cur, prev, nxt = _halo_specs(2 * width, s, nt)

    def const(shape):
        return pl.BlockSpec(shape, lambda d, j, bi: (0,) * len(shape))

    return pl.pallas_call(
        functools.partial(_mlstm_kernel, nt=nt),
        grid=(2, nt, b),
        in_specs=[cur, prev, nxt, const(conv_w.shape), const(conv_b.shape),
                  pl.BlockSpec((TM, 2 * width), lambda d, j, bi: (_tile_of(d, j, nt), 0)),
                  pl.BlockSpec((TM, 2 * width), lambda d, j, bi: (_tile_of(d, j, nt), 0)),
                  pl.BlockSpec((None, TM, width), lambda d, j, bi: (bi, _tile_of(d, j, nt), 0)),
                  pl.BlockSpec((None, TM, W_SMALL), lambda d, j, bi: (bi, _tile_of(d, j, nt), 0)),
                  pl.BlockSpec((None, W_SMALL, TM), lambda d, j, bi: (bi, 0, _tile_of(d, j, nt))),
                  const(gbias.shape), const(gbias_t.shape)],
        out_specs=pl.BlockSpec((None, None, TM, width), lambda d, j, bi: (d, bi, _tile_of(d, j, nt), 0)),
        out_shape=jax.ShapeDtypeStruct((2, b, s, width), F32),
        scratch_shapes=[pltpu.VMEM((b, width, width), F32), pltpu.VMEM((b, 8, width), F32),
                        pltpu.VMEM((b, 8, width), F32)],
        compiler_params=pltpu.CompilerParams(vmem_limit_bytes=VMEM_LIMIT),
        name="mlstm",
    )(qk, qk, qk, conv_w, conv_b, cos_t, sin_t, v, small, small_t, gbias, gbias_t)


def _s5_kernel(u_ref, lre_ref, lim_ref, ldt_ref, bre_ref, bim_ref, cw_ref, y_ref,
               wbu_scr, are_scr, aim_scr, hre_scr, him_scr, hbuf, *, nb):
    d = pl.program_id(0)
    j = pl.program_id(1)
    rev = d == 1
    n_state = S5_GROUPS * S5_STATE

    @pl.when(j == 0)
    def _():
        lre = lre_ref[...]
        lim = lim_ref[...]
        dt = jnp.exp(ldt_ref[...])
        mag = jnp.exp(lre * dt)
        a_re = mag * jnp.cos(lim * dt)
        a_im = mag * jnp.sin(lim * dt)
        den = lre * lre + lim * lim
        nr = a_re - 1.0
        coef_re = (nr * lre + a_im * lim) / den
        coef_im = (a_im * lre - nr * lim) / den
        wbu_scr[:, :n_state] = (coef_re * bre_ref[...] - coef_im * bim_ref[...]).astype(MXU_DTYPE)
        wbu_scr[:, n_state:] = (coef_re * bim_ref[...] + coef_im * bre_ref[...]).astype(MXU_DTYPE)
        are_scr[...] = jnp.broadcast_to(a_re, are_scr.shape)
        aim_scr[...] = jnp.broadcast_to(a_im, aim_scr.shape)
        hre_scr[...] = jnp.zeros(hre_scr.shape, F32)
        him_scr[...] = jnp.zeros(him_scr.shape, F32)

    a_re = are_scr[...]
    a_im = aim_scr[...]
    rows_sub = S5_SUB * nb
    n_sub = TM // S5_SUB
    carry = (hre_scr[...], him_scr[...])
    for sb in range(n_sub):
        sbi = jnp.where(rev, n_sub - 1 - sb, sb)
        rows = pl.ds(pl.multiple_of(sbi * rows_sub, rows_sub), rows_sub)
        hbuf[...] = jnp.dot(u_ref[rows, :].astype(MXU_DTYPE), wbu_scr[...], preferred_element_type=F32)

        def step(i, hc):
            ti = jnp.where(rev, S5_SUB - 1 - i, i)
            r = pl.ds(pl.multiple_of(ti * nb, nb), nb)
            h_re, h_im = hc
            n_re = a_re * h_re - a_im * h_im + hbuf[r, :n_state]
            n_im = a_re * h_im + a_im * h_re + hbuf[r, n_state:]
            hbuf[r, :n_state] = n_re
            hbuf[r, n_state:] = n_im
            return n_re, n_im

        carry = lax.fori_loop(0, S5_SUB, step, carry)
        y_ref[rows, :] = jnp.dot(hbuf[...].astype(MXU_DTYPE), cw_ref[...], preferred_element_type=F32)
    hre_scr[...] = carry[0]
    him_scr[...] = carry[1]


def _s5(u_tm, lam_re, lam_im, log_dt, braw_re, braw_im, cw, nb):
    rows, width = u_tm.shape
    s = rows // nb
    nt = s // TM
    n_state = S5_GROUPS * S5_STATE

    def per_dir():
        return pl.BlockSpec((None, 1, n_state), lambda d, j: (d, 0, 0))

    def const(shape):
        return pl.BlockSpec(shape, lambda d, j: (0,) * len(shape))

    return pl.pallas_call(
        functools.partial(_s5_kernel, nb=nb),
        grid=(2, nt),
        in_specs=[pl.BlockSpec((TM * nb, width), lambda d, j: (_tile_of(d, j, nt), 0)),
                  per_dir(), per_dir(), per_dir(), const(braw_re.shape), const(braw_im.shape), const(cw.shape)],
        out_specs=pl.BlockSpec((None, TM * nb, width), lambda d, j: (d, _tile_of(d, j, nt), 0)),
        out_shape=jax.ShapeDtypeStruct((2, rows, width), F32),
        scratch_shapes=[pltpu.VMEM((width, 2 * n_state), MXU_DTYPE),
                        pltpu.VMEM((nb, n_state), F32), pltpu.VMEM((nb, n_state), F32),
                        pltpu.VMEM((nb, n_state), F32), pltpu.VMEM((nb, n_state), F32),
                        pltpu.VMEM((S5_SUB * nb, 2 * n_state), F32)],
        compiler_params=pltpu.CompilerParams(vmem_limit_bytes=VMEM_LIMIT),
        name="s5",
    )(u_tm, lam_re, lam_im, log_dt, braw_re, braw_im, cw)


def _na_kernel(q_ref, k_ref, v_ref, tbl_ref, o_ref, *, first_tile, n_rows):
    t = first_tile + pl.program_id(1)
    width = NA_HEADS * HEAD_DIM
    head = _head_of_lane(width)
    scale = HEAD_DIM ** -0.5
    k_ctx = k_ref[0:TM, :]
    v_ctx = v_ref[0:TM, :]

    def attend(qh, keys, vals, bias):
        s = _mm_nt(qh, keys)
        return s if bias is None else s + bias

    @pl.when(t == 0)
    def _():
        q = q_ref[...] * scale
        acc = jnp.zeros((TM, width), F32)
        for h in range(NA_HEADS):
            s = _mm_nt(jnp.where(head == h, q, 0.0), k_ctx)
            p = jnp.exp(s - jnp.max(s, axis=1, keepdims=True))
            o = _mm(p, v_ctx) / jnp.sum(p, axis=1, keepdims=True)
            acc = jnp.where(head == h, o, acc)
        o_ref[...] = acc

    @pl.when(t > 0)
    def _():
        rows_per_tile = TM // GRID_W
        n_lat = NA_KH * GRID_W
        for rr in range(rows_per_tile):
            r = (t - 1) * rows_per_tile + rr
            row_start = jnp.clip(r - NA_KH // 2, 0, n_rows - NA_KH)
            off = (NA_KH - 1) - (r - row_start)
            win = pl.ds(pl.multiple_of(TM + row_start * GRID_W, GRID_W), n_lat)
            k_win = k_ref[win, :]
            v_win = v_ref[win, :]
            q = q_ref[rr * GRID_W:(rr + 1) * GRID_W, :] * scale
            acc = jnp.zeros((GRID_W, width), F32)
            for h in range(NA_HEADS):
                qh = jnp.where(head == h, q, 0.0)
                s_lat = _mm_nt(qh, k_win) + tbl_ref[h, off]
                s_ctx = _mm_nt(qh, k_ctx)
                m = jnp.maximum(jnp.max(s_lat, axis=1, keepdims=True), jnp.max(s_ctx, axis=1, keepdims=True))
                p_lat = jnp.exp(s_lat - m)
                p_ctx = jnp.exp(s_ctx - m)
                den = jnp.sum(p_lat, axis=1, keepdims=True) + jnp.sum(p_ctx, axis=1, keepdims=True)
                o = (_mm(p_lat, v_win) + _mm(p_ctx, v_ctx)) / den
                acc = jnp.where(head == h, o, acc)
            o_ref[rr * GRID_W:(rr + 1) * GRID_W, :] = acc


def _na(nqkv, tbl, with_ctx):
    b, s, _ = nqkv.shape
    nt = s // TM
    width = NA_HEADS * HEAD_DIM
    first_tile = 0 if with_ctx else 1
    n_rows = (s - TM) // GRID_W
    return pl.pallas_call(
        functools.partial(_na_kernel, first_tile=first_tile, n_rows=n_rows),
        grid=(b, nt - first_tile),
        in_specs=[pl.BlockSpec((None, TM, width), lambda bi, j: (bi, first_tile + j, 0)),
                  pl.BlockSpec((None, s, width), lambda bi, j: (bi, 0, 1)),
                  pl.BlockSpec((None, s, width), lambda bi, j: (bi, 0, 2)),
                  pl.BlockSpec(tbl.shape, lambda bi, j: (0, 0, 0, 0))],
        out_specs=pl.BlockSpec((None, TM, width), lambda bi, j: (bi, first_tile + j, 0)),
        out_shape=jax.ShapeDtypeStruct((b, s, width), F32),
        compiler_params=pltpu.CompilerParams(vmem_limit_bytes=VMEM_LIMIT),
        name="na",
    )(nqkv, nqkv, nqkv, tbl)


def _ssd_kernel(x_ref, xp_ref, xn_ref, cw_ref, cb_ref, sm_ref, smt_ref, dtb_ref, dtbt_ref, al_ref, alt_ref,
                dsk_ref, y_ref, st_scr, *, nt):
    d = pl.program_id(0)
    j = pl.program_id(1)
    b = pl.program_id(2)
    t = _tile_of(d, j, nt)
    rev = d == 1
    width = SSD_HEADS * HEAD_DIM
    gn = SSD_GROUPS * SSD_STATE
    per_group = SSD_HEADS // SSD_GROUPS

    @pl.when(j == 0)
    def _():
        st_scr[b] = jnp.zeros(st_scr.shape[1:], F32)

    prev_ok, next_ok = _halo_valid(t, nt)
    xbc = _conv_silu(x_ref[...], xp_ref[...], xn_ref[...], cw_ref, cb_ref, prev_ok, next_ok)
    xs = xbc[:, :width]
    bm = xbc[:, width:width + gn]
    cm = xbc[:, width + gn:]

    mask, mask_t = _tri_masks(rev, TM)
    dt_all = _softplus(sm_ref[...] + dtb_ref[...])
    acs_all = _mm_f32(mask.astype(F32), dt_all * (-jnp.exp(al_ref[...])))
    dt_t = _softplus(smt_ref[L_DDT:L_DDT + 2 * SSD_HEADS, :] + dtbt_ref[...])
    acs_t = _mm_f32(dt_t * (-jnp.exp(alt_ref[...])), mask_t.astype(F32))

    dt_cols, e_cols, decay_cols, last_cols, dmats = [], [], [], [], []
    for h in range(SSD_HEADS):
        lf, lb = L_DDT + h, L_DDT + SSD_HEADS + h
        dt_cols.append(jnp.where(rev, dt_all[:, lb:lb + 1], dt_all[:, lf:lf + 1]))
        acs_col = jnp.where(rev, acs_all[:, lb:lb + 1], acs_all[:, lf:lf + 1])
        acs_row = jnp.where(rev, acs_t[SSD_HEADS + h:SSD_HEADS + h + 1, :], acs_t[h:h + 1, :])
        a_last = jnp.where(rev, acs_col[0:1, :], acs_col[TM - 1:TM, :])
        e_cols.append(jnp.exp(acs_col))
        decay_cols.append(jnp.exp(a_last - acs_col))
        last_cols.append(jnp.exp(a_last))
        dmats.append(jnp.exp(jnp.where(mask, acs_col - acs_row, NEG)))

    xdt = xs * _expand_heads(dt_cols, width)
    e_full = _expand_heads(e_cols, width)
    xdec = xdt * _expand_heads(decay_cols, width)
    last_full = _expand_heads(last_cols, width)
    st_old = st_scr[b]
    head = _head_of_lane(width)

    y = jnp.zeros((TM, width), F32)
    st_new = []
    for g in range(SSD_GROUPS):
        b_g = bm[:, g * SSD_STATE:(g + 1) * SSD_STATE]
        c_g = cm[:, g * SSD_STATE:(g + 1) * SSD_STATE]
        lanes = slice(g * per_group * HEAD_DIM, (g + 1) * per_group * HEAD_DIM)
        cb = _mm_nt(c_g, b_g)
        y_g = _mm(c_g, st_old[:, lanes]) * e_full[:, lanes]
        head_g = head[:, lanes]
        for hh in range(per_group):
            h = g * per_group + hh
            y_h = _mm(cb * dmats[h], xdt[:, lanes])
            y_g = y_g + jnp.where(head_g == h, y_h, 0.0)
        st_new.append(st_old[:, lanes] * last_full[:, lanes] + _mm_tn(b_g, xdec[:, lanes]))
        y = y_g if g == 0 else jnp.concatenate([y, y_g], axis=1)
    st_scr[b] = jnp.concatenate(st_new, axis=1)
    y_ref[...] = y + jnp.where(rev, 0.0, 1.0) * dsk_ref[...] * xs


def _ssd(dxbc, small, small_t, conv_w, conv_b, dtb, dtb_t, alog, alog_t, dskip):
    b, s, cw = dxbc.shape
    nt = s // TM
    width = SSD_HEADS * HEAD_DIM
    cur, prev, nxt = _halo_specs(cw, s, nt)

    def const(shape):
        return pl.BlockSpec(shape, lambda d, j, bi: (0,) * len(shape))

    return pl.pallas_call(
        functools.partial(_ssd_kernel, nt=nt),
        grid=(2, nt, b),
        in_specs=[cur, prev, nxt, const(conv_w.shape), const(conv_b.shape),
                  pl.BlockSpec((None, TM, W_SMALL), lambda d, j, bi: (bi, _tile_of(d, j, nt), 0)),
                  pl.BlockSpec((None, W_SMALL, TM), lambda d, j, bi: (bi, 0, _tile_of(d, j, nt))),
                  const(dtb.shape), const(dtb_t.shape), const(alog.shape), const(alog_t.shape),
                  const(dskip.shape)],
        out_specs=pl.BlockSpec((None, None, TM, width), lambda d, j, bi: (d, bi, _tile_of(d, j, nt), 0)),
        out_shape=jax.ShapeDtypeStruct((2, b, s, width), F32),
        scratch_shapes=[pltpu.VMEM((b, SSD_STATE, width), F32)],
        compiler_params=pltpu.CompilerParams(vmem_limit_bytes=VMEM_LIMIT),
        name="ssd",
    )(dxbc, dxbc, dxbc, conv_w, conv_b, small, small_t, dtb, dtb_t, alog, alog_t, dskip)


def _merge_kernel(x_ref, mod_ref, nw_ref, hm_ref, mo_ref, mnw_ref, ys_ref, su_ref, s5d_ref, glu_ref, na_ref,
                  yd_ref, dz_ref, dnw_ref, wg_ref, wa_ref, wb_ref, wc_ref, wd_ref, wo_ref, o_ref):
    x = x_ref[...]
    d_model = x.shape[1]
    h = _norm_mod(x, nw_ref[...], mod_ref[1:2, :], mod_ref[0:1, :]).astype(MXU_DTYPE)

    wm = MLSTM_HEADS * HEAD_DIM
    hm = (hm_ref[0] + hm_ref[1]) * _sigmoid(mo_ref[...])
    same_head = (jnp.right_shift(_iota((wm, wm), 0), 6) == jnp.right_shift(_iota((wm, wm), 1), 6))
    ms = _mm_f32(hm * hm, same_head.astype(F32)) * (1.0 / HEAD_DIM)
    ya = hm * lax.rsqrt(ms + EPS) * mnw_ref[...]

    u = su_ref[...]
    ys = ys_ref[0] + ys_ref[1] + s5d_ref[...] * u
    ys = 0.5 * ys * (1.0 + jnp.tanh(math.sqrt(2.0 / math.pi) * (ys + 0.044715 * (ys * ys * ys))))
    ab = _mm(ys, glu_ref[...])
    ws = S5_GROUPS * S5_GROUP
    yb = ab[:, :ws] * _sigmoid(ab[:, ws:])

    yc = na_ref[...]

    yd = (yd_ref[0] + yd_ref[1]) * _silu(dz_ref[...])
    yd = yd * lax.rsqrt(jnp.mean(yd * yd, axis=-1, keepdims=True) + EPS) * dnw_ref[...]

    m = None
    for i, (y, w_ref) in enumerate(((ya, wa_ref), (yb, wb_ref), (yc, wc_ref), (yd, wd_ref))):
        gate = _sigmoid(jnp.dot(h, wg_ref[:, i * d_model:(i + 1) * d_model], preferred_element_type=F32))
        term = gate * _mm(y, w_ref[...])
        m = term if m is None else m + term
    o_ref[...] = x + mod_ref[2:3, :] * _mm(m, wo_ref[...])


def _merge(seq, mod, norm_w, hm, mo, mnw, ys, su_tm, s5d, glu_w, yna, yd, dz, dnw, wg, wa, wb, wc, wd, wo,
           with_ctx):
    b, s, d = seq.shape
    nt = s // TM
    first = 0 if with_ctx else 1

    def tok(width):
        return pl.BlockSpec((None, TM, width), lambda bi, j: (bi, first + j, 0))

    def tok2(width):
        return pl.BlockSpec((2, None, TM, width), lambda bi, j: (0, bi, first + j, 0))

    def const(arr):
        return pl.BlockSpec(arr.shape, lambda bi, j: (0,) * arr.ndim, pipeline_mode=pl.Buffered(1))

    ws = S5_GROUPS * S5_GROUP
    return pl.pallas_call(
        _merge_kernel,
        grid=(b, nt - first),
        in_specs=[tok(d),
                  pl.BlockSpec((None, None, 6, d), lambda bi, j: (bi, jnp.minimum(first + j, 1), 0, 0)),
                  const(norm_w),
                  tok2(MLSTM_HEADS * HEAD_DIM), tok(W_O), const(mnw),
                  pl.BlockSpec((2, TM, ws), lambda bi, j: (0, first + j, bi)),
                  pl.BlockSpec((TM, ws), lambda bi, j: (first + j, bi)),
                  const(s5d), const(glu_w),
                  tok(NA_HEADS * HEAD_DIM),
                  tok2(SSD_HEADS * HEAD_DIM), tok(W_DZ), const(dnw),
                  const(wg), const(wa), const(wb), const(wc), const(wd), const(wo)],
        out_specs=tok(d),
        out_shape=jax.ShapeDtypeStruct((b, s, d), F32),
        input_output_aliases={0: 0},
        compiler_params=pltpu.CompilerParams(vmem_limit_bytes=VMEM_LIMIT),
        name="merge",
    )(seq, mod, norm_w, hm, mo, mnw, ys, su_tm, s5d, glu_w, yna, yd, dz, dnw, wg, wa, wb, wc, wd, wo)


def _ffn_kernel(x_ref, mod_ref, nw_ref, wi_ref, wo_ref, fw_ref, o_ref, *, final):
    x = x_ref[...]
    hidden = wo_ref.shape[0]
    h = _norm_mod(x, nw_ref[...], mod_ref[4:5, :], mod_ref[3:4, :]).astype(MXU_DTYPE)
    a = jnp.dot(h, wi_ref[:, :hidden], preferred_element_type=F32)
    g = jnp.dot(h, wi_ref[:, hidden:], preferred_element_type=F32)
    y = x + mod_ref[5:6, :] * _mm(_silu(a) * g, wo_ref[...])
    if final:
        y = y * lax.rsqrt(jnp.mean(y * y, axis=-1, keepdims=True) + EPS) * fw_ref[...]
    o_ref[...] = y


def _ffn(seq, mod, norm_w, wi, wo, final_w, final):
    b, s, d = seq.shape
    nt = s // TM
    first = 1 if final else 0

    def const(arr):
        return pl.BlockSpec(arr.shape, lambda bi, j: (0,) * arr.ndim, pipeline_mode=pl.Buffered(1))

    tok_in = pl.BlockSpec((None, TM, d), lambda bi, j: (bi, first + j, 0))
    if final:
        out_spec = pl.BlockSpec((None, TM, d), lambda bi, j: (bi, j, 0))
        out_shape = jax.ShapeDtypeStruct((b, s - TM, d), F32)
        aliases = {}
    else:
        out_spec = tok_in
        out_shape = jax.ShapeDtypeStruct((b, s, d), F32)
        aliases = {0: 0}
    return pl.pallas_call(
        functools.partial(_ffn_kernel, final=final),
        grid=(b, nt - first),
        in_specs=[tok_in,
                  pl.BlockSpec((None, None, 6, d), lambda bi, j: (bi, jnp.minimum(first + j, 1), 0, 0)),
                  const(norm_w), const(wi), const(wo), const(final_w)],
        out_specs=out_spec,
        out_shape=out_shape,
        input_output_aliases=aliases,
        compiler_params=pltpu.CompilerParams(vmem_limit_bytes=VMEM_LIMIT),
        name="ffn_final" if final else "ffn",
    )(seq, mod, norm_w, wi, wo, final_w)


def _rope_tables(t, width):
    nf = HEAD_DIM // 4
    inv = ROPE_THETA ** (-jnp.arange(nf, dtype=F32) / nf)
    tok = jnp.arange(t)
    ang_r = (tok // GRID_W).astype(F32)[:, None] * inv
    ang_c = (tok % GRID_W).astype(F32)[:, None] * inv
    cos_h = jnp.concatenate([jnp.cos(ang_r)] * 2 + [jnp.cos(ang_c)] * 2, axis=1)
    sin_h = jnp.concatenate([-jnp.sin(ang_r), jnp.sin(ang_r), -jnp.sin(ang_c), jnp.sin(ang_c)], axis=1)
    cos_x = jnp.tile(cos_h, (1, width // HEAD_DIM))
    sin_x = jnp.tile(sin_h, (1, width // HEAD_DIM))
    cos_t = jnp.concatenate([jnp.ones((TM, width), F32), cos_x], axis=0)
    sin_t = jnp.concatenate([jnp.zeros((TM, width), F32), sin_x], axis=0)
    scale = HEAD_DIM ** -0.5
    return (jnp.concatenate([cos_t * scale, cos_t], axis=1), jnp.concatenate([sin_t * scale, sin_t], axis=1))


def _na_bias_tables(rpb):
    col = jnp.arange(GRID_W)
    col0 = jnp.clip(col - NA_KW // 2, 0, GRID_W - NA_KW)
    in_win = (col[None, :] >= col0[:, None]) & (col[None, :] < col0[:, None] + NA_KW)
    dc = jnp.clip(col[None, :] - col[:, None], -(NA_KW - 1), NA_KW - 1) + (NA_KW - 1)
    dr = jnp.arange(NA_KH)[:, None] + jnp.arange(NA_KH)[None, :]
    bias = rpb.astype(F32)[:, dr][..., dc]
    bias = jnp.where(in_win[None, None, None], bias, NEG)
    return bias.transpose(0, 1, 3, 2, 4).reshape(NA_HEADS, NA_KH, GRID_W, NA_KH * GRID_W)


def _block_diag(blocks):
    g, r, c = blocks.shape
    eye = jnp.eye(g, dtype=blocks.dtype)
    return (eye[:, None, :, None] * blocks[:, :, None, :]).reshape(g * r, g * c)


def _lanes(vec, at, width=W_SMALL):
    return jnp.zeros((1, width), F32).at[0, at:at + vec.shape[0]].set(vec.astype(F32))


def kernel(x, c, ctx, c_ctx, ada_w, ada_b, norm1_w, norm2_w, w_in, mlstm_conv_w, mlstm_conv_b, mlstm_ib, mlstm_fb, mlstm_norm_w, s5_lam_re, s5_lam_im, s5_log_dt, s5_b_re, s5_b_im, s5_c_re, s5_c_im, s5_d, s5_glu_w, na_rpb, ssd_conv_w, ssd_conv_b, ssd_a_log, ssd_dt_bias, ssd_d, ssd_norm_w, w_branch_a, w_branch_b, w_branch_c, w_branch_d, w_out, ffn_w_in, ffn_w_out, final_norm_w):
    b, t, d = x.shape
    depth = w_in.shape[0]
    assert ctx.shape[1] == TM and t % TM == 0 and t % GRID_W == 0 and b % 8 == 0
    assert t // GRID_W >= NA_KH

    seq = jnp.concatenate([ctx, x], axis=1)

    pad = (-(b + 1)) % 8
    cc = jnp.concatenate([c, c_ctx[None, :], jnp.zeros((pad, d), F32)], axis=0)
    mod_all = _adaln(cc, ada_w, ada_b)
    mod_x = mod_all[:, :b].reshape(depth, b, 1, 6, d)
    mod_c = jnp.broadcast_to(mod_all[:, b].reshape(depth, 1, 1, 6, d), (depth, b, 1, 6, d))
    mod = jnp.concatenate([mod_c, mod_x], axis=2)

    cos_t, sin_t = _rope_tables(t, MLSTM_HEADS * HEAD_DIM)
    n_state = S5_GROUPS * S5_STATE

    for l in range(depth):
        with_ctx = l < depth - 1
        wl = w_in[l]
        w_small = jnp.concatenate([wl[:, 1024:1040], wl[:, 3600:3616], jnp.zeros((d, W_SMALL - 32), F32)], axis=1)
        w_proj = jnp.concatenate([wl[:, 0:512], wl[:, 512:768], wl[:, 768:1024], w_small, wl[:, 1040:1296],
                                  wl[:, 1296:2064], wl[:, 2064:2576], wl[:, 2576:3600]], axis=1).astype(MXU_DTYPE)
        w_gate = wl[:, 3616:].astype(MXU_DTYPE)

        qk, v, mo, small, small_t, su_tm, nqkv, dz, dxbc = _proj(seq, mod[l], norm1_w[l][None, :], w_proj)

        gbias = _lanes(mlstm_ib[l].reshape(-1), L_MI) + _lanes(mlstm_fb[l].reshape(-1), L_MF)
        conv_w = jnp.concatenate([mlstm_conv_w[l], jnp.zeros((1, mlstm_conv_w.shape[2]), F32)], axis=0)
        hm = _mlstm(qk, v, small, small_t, conv_w, mlstm_conv_b[l][None, :], cos_t, sin_t,
                    gbias, gbias[0, :16][:, None])

        braw_re = _block_diag(jnp.swapaxes(s5_b_re[l], 1, 2))
        braw_im = _block_diag(jnp.swapaxes(s5_b_im[l], 1, 2))
        cw = jnp.concatenate([_block_diag(jnp.swapaxes(s5_c_re[l], 1, 2)),
                              -_block_diag(jnp.swapaxes(s5_c_im[l], 1, 2))], axis=0).astype(MXU_DTYPE)
        ys = _s5(su_tm.reshape(-1, W_SU), s5_lam_re[l].reshape(2, 1, n_state), s5_lam_im[l].reshape(2, 1, n_state),
                 jnp.repeat(s5_log_dt[l], S5_STATE, axis=1).reshape(2, 1, n_state), braw_re, braw_im, cw, b)
        ys = ys.reshape(2, -1, b * W_SU)

        yna = _na(nqkv, _na_bias_tables(na_rpb[l]), with_ctx)

        dtb = _lanes(ssd_dt_bias[l].reshape(-1), L_DDT)
        alog = _lanes(ssd_a_log[l].reshape(-1), L_DDT)
        sconv_w = jnp.concatenate([ssd_conv_w[l], jnp.zeros((1, ssd_conv_w.shape[2]), F32)], axis=0)
        yd = _ssd(dxbc, small, small_t, sconv_w, ssd_conv_b[l][None, :], dtb,
                  ssd_dt_bias[l].reshape(-1, 1), alog, ssd_a_log[l].reshape(-1, 1),
                  jnp.repeat(ssd_d[l], HEAD_DIM)[None, :])

        seq = _merge(seq, mod[l], norm1_w[l][None, :], hm, mo, mlstm_norm_w[l][None, :], ys, su_tm,
                     s5_d[l][None, :], s5_glu_w[l].astype(MXU_DTYPE), yna, yd, dz, ssd_norm_w[l][None, :],
                     w_gate, w_branch_a[l].astype(MXU_DTYPE), w_branch_b[l].astype(MXU_DTYPE),
                     w_branch_c[l].astype(MXU_DTYPE), w_branch_d[l].astype(MXU_DTYPE), w_out[l].astype(MXU_DTYPE),
                     with_ctx)
        seq = _ffn(seq, mod[l], norm2_w[l][None, :], ffn_w_in[l].astype(MXU_DTYPE), ffn_w_out[l].astype(MXU_DTYPE),
                   final_norm_w[None, :], not with_ctx)
    return seq
```

```python
import functools
import math

import jax
import jax.numpy as jnp
from jax import lax
from jax.experimental import pallas as pl
from jax.experimental.pallas import tpu as pltpu

F32 = jnp.float32
MXU_DTYPE = jnp.bfloat16
HIGHEST = lax.Precision.HIGHEST

GRID_W = 64
EPS = 1e-6
CONV_K = 7
ROPE_THETA = 10000.0
HEAD_DIM = 64
MLSTM_HEADS = 4
S5_GROUPS = 16
S5_GROUP = 16
S5_STATE = 64
NA_HEADS = 4
NA_KH = 8
NA_KW = 16
SSD_HEADS = 8
SSD_GROUPS = 2
SSD_STATE = 128

TM = 256
SUB = 128
HALO = 8
S5_SUB = 32
NEG = -1e30
VMEM_LIMIT = 56 * 1024 * 1024

W_QK, W_XBC, W_V, W_O, W_SMALL, W_SU, W_NQKV, W_DZ = 512, 1024, 256, 256, 128, 256, 768, 512
W_CV = W_QK + W_XBC
PROJ_WIDTHS = (W_CV, W_V, W_O, W_SMALL, W_SU, W_NQKV, W_DZ)
L_MI, L_MF, L_DDT = 0, 8, 16
G_LI, G_BCUM, G_ACS, G_DT = 0, 4, 8, 16


def _mm(a, b):
    return jnp.dot(a.astype(MXU_DTYPE), b.astype(MXU_DTYPE), preferred_element_type=F32)


def _mm_nt(a, b):
    return lax.dot_general(a.astype(MXU_DTYPE), b.astype(MXU_DTYPE), (((1,), (1,)), ((), ())),
                           preferred_element_type=F32)


def _mm_tn(a, b):
    return lax.dot_general(a.astype(MXU_DTYPE), b.astype(MXU_DTYPE), (((0,), (0,)), ((), ())),
                           preferred_element_type=F32)


def _mm_f32(a, b):
    return jnp.dot(a, b, preferred_element_type=F32, precision=HIGHEST)


def _split3(x):
    hi = x.astype(MXU_DTYPE)
    r1 = x - hi.astype(F32)
    mid = r1.astype(MXU_DTYPE)
    lo = (r1 - mid.astype(F32)).astype(MXU_DTYPE)
    return hi, mid, lo


def _mm_exact_rhs(x, sel):
    sel = _as_01(sel)
    return sum(jnp.dot(p, sel, preferred_element_type=F32) for p in _split3(x))


def _mm_exact_lhs(sel, x):
    sel = _as_01(sel)
    return sum(jnp.dot(sel, p, preferred_element_type=F32) for p in _split3(x))


def _as_01(sel):
    if sel.dtype == jnp.bool_:
        sel = jnp.where(sel, 1.0, 0.0)
    return sel.astype(MXU_DTYPE)


def _sigmoid(x):
    return 1.0 / (1.0 + jnp.exp(-x))


def _silu(x):
    return x * _sigmoid(x)


def _softplus(x):
    return jnp.maximum(x, 0.0) + jnp.log(1.0 + jnp.exp(-jnp.abs(x)))


def _iota(shape, dim):
    return lax.broadcasted_iota(jnp.int32, shape, dim)


def _head_of_lane(width):
    return jnp.right_shift(_iota((1, width), 1), 6)


def _same_head(rows, cols):
    return jnp.right_shift(_iota((rows, cols), 0), 6) == jnp.right_shift(_iota((rows, cols), 1), 6)


def _expand_heads(cols, width):
    head = _head_of_lane(width)
    out = jnp.broadcast_to(cols[0], (cols[0].shape[0], width))
    for h in range(1, len(cols)):
        out = jnp.where(head == h, cols[h], out)
    return out


def _tile_of(d, j, nt):
    return jnp.where(d == 0, j, jnp.where(j == 0, 0, nt - j))


def _norm_mod(x, w, scale, shift):
    y = x * lax.rsqrt(jnp.mean(x * x, axis=-1, keepdims=True) + EPS) * w
    return y * (1.0 + scale) + shift


def _order_mask(rev, n):
    diff = (_iota((n, n), 1) - _iota((n, n), 0)) * jnp.where(rev, -1, 1)
    return diff <= 0


def _adaln_kernel(c_ref, w_ref, b_ref, o_ref):
    o_ref[...] = _mm_f32(_silu(c_ref[...]), w_ref[...]) + b_ref[...]


def _adaln(cc, ada_w, ada_b):
    depth, d, n = ada_w.shape
    tn = 768
    return pl.pallas_call(
        _adaln_kernel,
        grid=(depth, n // tn),
        in_specs=[pl.BlockSpec(cc.shape, lambda l, i: (0, 0)),
                  pl.BlockSpec((None, d, tn), lambda l, i: (l, 0, i)),
                  pl.BlockSpec((None, 1, tn), lambda l, i: (l, 0, i))],
        out_specs=pl.BlockSpec((None, cc.shape[0], tn), lambda l, i: (l, 0, i)),
        out_shape=jax.ShapeDtypeStruct((depth, cc.shape[0], n), F32),
        name="adaln",
    )(cc, ada_w, ada_b.reshape(depth, 1, n))


def _proj_kernel(x_ref, mod_ref, nw_ref, w_ref, *out_refs):
    h = _norm_mod(x_ref[...], nw_ref[...], mod_ref[1:2, :], mod_ref[0:1, :]).astype(MXU_DTYPE)
    off = 0
    for ref, n in zip(out_refs, PROJ_WIDTHS):
        ref[...] = jnp.dot(h, w_ref[:, off:off + n], preferred_element_type=F32)
        off += n


def _proj(seq, mod, norm_w, w_proj):
    b, s, d = seq.shape
    nt = s // TM
    n_all = sum(PROJ_WIDTHS)

    def tok(width):
        return pl.BlockSpec((None, TM, width), lambda bi, j: (bi, j, 0))

    def shape(width):
        return jax.ShapeDtypeStruct((b, s, width), F32)

    su_pos = 4
    out_specs = [tok(w) for w in PROJ_WIDTHS]
    out_shape = [shape(w) for w in PROJ_WIDTHS]
    out_specs[su_pos] = pl.BlockSpec((TM, W_SU), lambda bi, j: (j, bi))
    out_shape[su_pos] = jax.ShapeDtypeStruct((s, b * W_SU), F32)
    return pl.pallas_call(
        _proj_kernel,
        grid=(b, nt),
        in_specs=[tok(d),
                  pl.BlockSpec((None, None, 6, d), lambda bi, j: (bi, jnp.minimum(j, 1), 0, 0)),
                  pl.BlockSpec((1, d), lambda bi, j: (0, 0)),
                  pl.BlockSpec((d, n_all), lambda bi, j: (0, 0), pipeline_mode=pl.Buffered(1))],
        out_specs=out_specs,
        out_shape=out_shape,
        compiler_params=pltpu.CompilerParams(vmem_limit_bytes=VMEM_LIMIT),
        name="proj",
    )(seq, mod, norm_w, w_proj)


def _prep_kernel(cv_ref, cvp_ref, cvn_ref, cw_ref, cb_ref, cos_ref, sin_ref, sm_ref, gb_ref, al_ref, perm_ref,
                 qk_ref, xbc_ref, gc_ref, gr_ref, ext, *, nt):
    t = pl.program_id(1)
    prev_ok = t > 1
    next_ok = (t >= 1) & (t < nt - 1)
    ext[0:HALO, :] = jnp.where(prev_ok, cvp_ref[...], 0.0)
    ext[HALO:HALO + TM, :] = cv_ref[...]
    ext[HALO + TM:, :] = jnp.where(next_ok, cvn_ref[...], 0.0)
    acc = cb_ref[...] + cw_ref[0:1, :] * ext[pl.ds(HALO - CONV_K // 2, TM), :]
    for k in range(1, CONV_K):
        acc = acc + cw_ref[k:k + 1, :] * ext[pl.ds(HALO - CONV_K // 2 + k, TM), :]
    acc = _silu(acc)
    xbc_ref[...] = acc[:, W_QK:]
    qk = acc[:, :W_QK]
    first = jnp.bitwise_and(_iota((1, W_QK), 1), 31) < 16
    partner = jnp.where(first, pltpu.roll(qk, W_QK - 16, 1), pltpu.roll(qk, 16, 1))
    qk_ref[...] = qk * cos_ref[...] + partner * sin_ref[...]

    lane = _iota((1, W_SMALL), 1)
    g_all = sm_ref[...] + gb_ref[...]
    dt = _softplus(g_all)
    src = jnp.where(lane < L_MF, g_all, jnp.where(lane < L_DDT, -_softplus(-g_all), dt * (-jnp.exp(al_ref[...]))))
    src = jnp.concatenate([src, dt], axis=1)
    r = _iota((TM, TM), 0)
    c = _iota((TM, TM), 1)
    same_chunk = jnp.right_shift(r, 7) == jnp.right_shift(c, 7)
    cum = (lane >= G_BCUM) & (lane < G_DT)
    for d in range(2):
        tri = same_chunk & ((c <= r) if d == 0 else (c >= r))
        g = _mm_exact_rhs(src, perm_ref[d])
        g = jnp.where(cum, _mm_exact_lhs(tri, jnp.where(cum, g, 0.0)), g)
        gc_ref[d] = g
        for i in range(TM // SUB):
            gr_ref[d, i] = g[i * SUB:(i + 1) * SUB, :].T


def _prep(cv, small, conv_w, conv_b, cos_t, sin_t, gbias, alog, perm):
    b, s, _ = cv.shape
    nt = s // TM
    per = TM // HALO
    last = s // HALO - 1
    nsub = TM // SUB

    def const(arr):
        return pl.BlockSpec(arr.shape, lambda bi, j: (0,) * arr.ndim)

    return pl.pallas_call(
        functools.partial(_prep_kernel, nt=nt),
        grid=(b, nt),
        in_specs=[pl.BlockSpec((None, TM, W_CV), lambda bi, j: (bi, j, 0)),
                  pl.BlockSpec((None, HALO, W_CV), lambda bi, j: (bi, jnp.maximum(j * per - 1, 0), 0)),
                  pl.BlockSpec((None, HALO, W_CV), lambda bi, j: (bi, jnp.minimum((j + 1) * per, last), 0)),
                  const(conv_w), const(conv_b),
                  pl.BlockSpec((TM, W_QK), lambda bi, j: (j, 0)),
                  pl.BlockSpec((TM, W_QK), lambda bi, j: (j, 0)),
                  pl.BlockSpec((None, TM, W_SMALL), lambda bi, j: (bi, j, 0)),
                  const(gbias), const(alog), const(perm)],
        out_specs=[pl.BlockSpec((None, TM, W_QK), lambda bi, j: (bi, j, 0)),
                   pl.BlockSpec((None, TM, W_XBC), lambda bi, j: (bi, j, 0)),
                   pl.BlockSpec((2, None, TM, W_SMALL), lambda bi, j: (0, bi, j, 0)),
                   pl.BlockSpec((2, None, nsub, W_SMALL, SUB), lambda bi, j: (0, bi, j, 0, 0))],
        out_shape=[jax.ShapeDtypeStruct((b, s, W_QK), F32),
                   jax.ShapeDtypeStruct((b, s, W_XBC), F32),
                   jax.ShapeDtypeStruct((2, b, s, W_SMALL), F32),
                   jax.ShapeDtypeStruct((2, b, s // SUB, W_SMALL, SUB), F32)],
        scratch_shapes=[pltpu.VMEM((TM + 2 * HALO, W_CV), F32)],
        compiler_params=pltpu.CompilerParams(vmem_limit_bytes=VMEM_LIMIT),
        name="prep",
    )(cv, cv, cv, conv_w, conv_b, cos_t, sin_t, small, gbias, alog, perm)


def _scan_specs(nt, widths):
    nsub = TM // SUB
    specs = [pl.BlockSpec((None, TM, w), lambda d, j, bi: (bi, _tile_of(d, j, nt), 0)) for w in widths]
    specs.append(pl.BlockSpec((None, None, TM, W_SMALL), lambda d, j, bi: (d, bi, _tile_of(d, j, nt), 0)))
    specs.append(pl.BlockSpec((None, None, nsub, W_SMALL, SUB), lambda d, j, bi: (d, bi, _tile_of(d, j, nt), 0, 0)))
    return specs


def _mlstm_kernel(qk_ref, v_ref, gc_ref, gr_ref, h_ref, c_scr, n_scr, m_scr):
    d = pl.program_id(0)
    j = pl.program_id(1)
    b = pl.program_id(2)
    rev = d == 1
    width = MLSTM_HEADS * HEAD_DIM
    nsub = TM // SUB

    @pl.when(j == 0)
    def _():
        c_scr[b] = jnp.zeros(c_scr.shape[1:], F32)
        n_scr[b] = jnp.zeros(n_scr.shape[1:], F32)
        m_scr[b] = jnp.zeros(m_scr.shape[1:], F32)

    mask = _order_mask(rev, SUB)
    head = _head_of_lane(width)
    same_head = _same_head(width, width)
    c_st = c_scr[b]
    n_st = n_scr[b, 0:1, :]
    m_st = m_scr[b, 0:1, :]

    for step in range(nsub):
        ci = jnp.where(rev, nsub - 1 - step, step)
        rows = pl.ds(pl.multiple_of(ci * SUB, SUB), SUB)
        q = qk_ref[rows, 0:width]
        k = qk_ref[rows, width:2 * width]
        v = v_ref[rows, :]
        gcol = gc_ref[rows, :]
        grow = gr_ref[ci]
        qb = q.astype(MXU_DTYPE)
        kb = k.astype(MXU_DTYPE)
        vb = v.astype(MXU_DTYPE)

        num = jnp.zeros((SUB, width), F32)
        g_cols, mt_cols, den_cols, wk_cols, decay_cols, mnew_cols = [], [], [], [], [], []
        for h in range(MLSTM_HEADS):
            li_row = grow[G_LI + h:G_LI + h + 1, :]
            bc_row = grow[G_BCUM + h:G_BCUM + h + 1, :]
            li_col = gcol[:, G_LI + h:G_LI + h + 1]
            bc_col = gcol[:, G_BCUM + h:G_BCUM + h + 1]
            m_prev = m_st[:, h * HEAD_DIM:h * HEAD_DIM + 1]

            log_w = jnp.where(mask, bc_col - bc_row + li_row, NEG)
            inter = bc_col + m_prev
            m_t = jnp.maximum(inter, jnp.max(log_w, axis=1, keepdims=True))
            sm = _mm_nt(jnp.where(head == h, qb, 0), kb) * jnp.exp(log_w - m_t)
            num = jnp.where(head == h, _mm(sm, vb), num)
            g_cols.append(jnp.exp(inter - m_t))
            mt_cols.append(m_t)
            den_cols.append(jnp.sum(sm, axis=1, keepdims=True))

            b_last = jnp.where(rev, bc_col[0:1, :], bc_col[SUB - 1:SUB, :])
            log_k = b_last - bc_col + li_col
            m_new = jnp.maximum(b_last + m_prev, jnp.max(log_k, axis=0, keepdims=True))
            wk_cols.append(jnp.exp(log_k - m_new))
            decay_cols.append(jnp.exp(b_last + m_prev - m_new))
            mnew_cols.append(m_new)

        g_full = _expand_heads(g_cols, width)
        qn = _mm(q * n_st, _as_01(same_head))
        num = num + g_full * _mm(qb, c_st)
        den = _expand_heads(den_cols, width) + g_full * qn
        h_ref[rows, :] = num / jnp.maximum(jnp.abs(den), jnp.exp(-_expand_heads(mt_cols, width)))

        kw = k * _expand_heads(wk_cols, width)
        decay = _expand_heads(decay_cols, width)
        c_st = c_st * decay + jnp.where(same_head, _mm_tn(kw, vb), 0.0)
        n_st = n_st * decay + jnp.sum(kw, axis=0, keepdims=True)
        m_st = _expand_heads(mnew_cols, width)

    c_scr[b] = c_st
    n_scr[b, 0:1, :] = n_st
    m_scr[b, 0:1, :] = m_st


def _mlstm(qk, v, gc, gr):
    b, s, _ = qk.shape
    nt = s // TM
    width = MLSTM_HEADS * HEAD_DIM
    return pl.pallas_call(
        _mlstm_kernel,
        grid=(2, nt, b),
        in_specs=_scan_specs(nt, (2 * width, width)),
        out_specs=pl.BlockSpec((None, None, TM, width), lambda d, j, bi: (d, bi, _tile_of(d, j, nt), 0)),
        out_shape=jax.ShapeDtypeStruct((2, b, s, width), F32),
        scratch_shapes=[pltpu.VMEM((b, width, width), F32), pltpu.VMEM((b, 8, width), F32),
                        pltpu.VMEM((b, 8, width), F32)],
        compiler_params=pltpu.CompilerParams(vmem_limit_bytes=VMEM_LIMIT),
        name="mlstm",
    )(qk, v, gc, gr)


def _s5_kernel(u_ref, lre_ref, lim_ref, ldt_ref, bre_ref, bim_ref, cw_ref, y_ref,
               wbu_scr, are_scr, aim_scr, hre_scr, him_scr, hbuf, *, nb):
    d = pl.program_id(0)
    j = pl.program_id(1)
    rev = d == 1
    n_state = S5_GROUPS * S5_STATE

    @pl.when(j == 0)
    def _():
        lre = lre_ref[...]
        lim = lim_ref[...]
        dt = jnp.exp(ldt_ref[...])
        mag = jnp.exp(lre * dt)
        a_re = mag * jnp.cos(lim * dt)
        a_im = mag * jnp.sin(lim * dt)
        den = lre * lre + lim * lim
        nr = a_re - 1.0
        coef_re = (nr * lre + a_im * lim) / den
        coef_im = (a_im * lre - nr * lim) / den
        wbu_scr[:, :n_state] = (coef_re * bre_ref[...] - coef_im * bim_ref[...]).astype(MXU_DTYPE)
        wbu_scr[:, n_state:] = (coef_re * bim_ref[...] + coef_im * bre_ref[...]).astype(MXU_DTYPE)
        are_scr[...] = jnp.broadcast_to(a_re, are_scr.shape)
        aim_scr[...] = jnp.broadcast_to(a_im, aim_scr.shape)
        hre_scr[...] = jnp.zeros(hre_scr.shape, F32)
        him_scr[...] = jnp.zeros(him_scr.shape, F32)

    a_re = are_scr[...]
    a_im = aim_scr[...]
    rows_sub = S5_SUB * nb
    n_sub = TM // S5_SUB
    carry = (hre_scr[...], him_scr[...])
    for sb in range(n_sub):
        sbi = jnp.where(rev, n_sub - 1 - sb, sb)
        rows = pl.ds(pl.multiple_of(sbi * rows_sub, rows_sub), rows_sub)
        hbuf[...] = jnp.dot(u_ref[rows, :].astype(MXU_DTYPE), wbu_scr[...], preferred_element_type=F32)

        def step(i, hc):
            ti = jnp.where(rev, S5_SUB - 1 - i, i)
            r = pl.ds(pl.multiple_of(ti * nb, nb), nb)
            h_re, h_im = hc
            n_re = a_re * h_re - a_im * h_im + hbuf[r, :n_state]
            n_im = a_re * h_im + a_im * h_re + hbuf[r, n_state:]
            hbuf[r, :n_state] = n_re
            hbuf[r, n_state:] = n_im
            return n_re, n_im

        carry = lax.fori_loop(0, S5_SUB, step, carry)
        y_ref[rows, :] = jnp.dot(hbuf[...].astype(MXU_DTYPE), cw_ref[...], preferred_element_type=F32)
    hre_scr[...] = carry[0]
    him_scr[...] = carry[1]


def _s5(u_tm, lam_re, lam_im, log_dt, braw_re, braw_im, cw, nb):
    rows, width = u_tm.shape
    s = rows // nb
    nt = s // TM
    n_state = S5_GROUPS * S5_STATE

    def per_dir():
        return pl.BlockSpec((None, 1, n_state), lambda d, j: (d, 0, 0))

    def const(shape):
        return pl.BlockSpec(shape, lambda d, j: (0,) * len(shape))

    return pl.pallas_call(
        functools.partial(_s5_kernel, nb=nb),
        grid=(2, nt),
        in_specs=[pl.BlockSpec((TM * nb, width), lambda d, j: (_tile_of(d, j, nt), 0)),
                  per_dir(), per_dir(), per_dir(), const(braw_re.shape), const(braw_im.shape), const(cw.shape)],
        out_specs=pl.BlockSpec((None, TM * nb, width), lambda d, j: (d, _tile_of(d, j, nt), 0)),
        out_shape=jax.ShapeDtypeStruct((2, rows, width), F32),
        scratch_shapes=[pltpu.VMEM((width, 2 * n_state), MXU_DTYPE),
                        pltpu.VMEM((nb, n_state), F32), pltpu.VMEM((nb, n_state), F32),
                        pltpu.VMEM((nb, n_state), F32), pltpu.VMEM((nb, n_state), F32),
                        pltpu.VMEM((S5_SUB * nb, 2 * n_state), F32)],
        compiler_params=pltpu.CompilerParams(vmem_limit_bytes=VMEM_LIMIT),
        name="s5",
    )(u_tm, lam_re, lam_im, log_dt, braw_re, braw_im, cw)


def _na_kernel(q_ref, k_ref, v_ref, tbl_ref, o_ref, *, first_tile, n_rows):
    t = first_tile + pl.program_id(1)
    width = NA_HEADS * HEAD_DIM
    head = _head_of_lane(width)
    scale = HEAD_DIM ** -0.5
    k_ctx = k_ref[0:TM, :].astype(MXU_DTYPE)
    v_ctx = v_ref[0:TM, :].astype(MXU_DTYPE)

    def stack_heads(q):
        return jnp.concatenate([jnp.where(head == h, q, 0.0) for h in range(NA_HEADS)], axis=0).astype(MXU_DTYPE)

    def unstack_heads(o, n):
        acc = o[0:n, :]
        for h in range(1, NA_HEADS):
            acc = jnp.where(head == h, o[h * n:(h + 1) * n, :], acc)
        return acc

    @pl.when(t == 0)
    def _():
        s = _mm_nt(stack_heads(q_ref[...] * scale), k_ctx)
        p = jnp.exp(s - jnp.max(s, axis=1, keepdims=True))
        o = _mm(p, v_ctx) / jnp.sum(p, axis=1, keepdims=True)
        o_ref[...] = unstack_heads(o, TM)

    @pl.when(t > 0)
    def _():
        rows_per_tile = TM // GRID_W
        n_lat = NA_KH * GRID_W
        for rr in range(rows_per_tile):
            r = (t - 1) * rows_per_tile + rr
            row_start = jnp.clip(r - NA_KH // 2, 0, n_rows - NA_KH)
            off = (NA_KH - 1) - (r - row_start)
            win = pl.ds(pl.multiple_of(TM + row_start * GRID_W, GRID_W), n_lat)
            qs = stack_heads(q_ref[rr * GRID_W:(rr + 1) * GRID_W, :] * scale)
            s_lat = _mm_nt(qs, k_ref[win, :]) + tbl_ref[off]
            s_ctx = _mm_nt(qs, k_ctx)
            m = jnp.maximum(jnp.max(s_lat, axis=1, keepdims=True), jnp.max(s_ctx, axis=1, keepdims=True))
            p_lat = jnp.exp(s_lat - m)
            p_ctx = jnp.exp(s_ctx - m)
            den = jnp.sum(p_lat, axis=1, keepdims=True) + jnp.sum(p_ctx, axis=1, keepdims=True)
            o = (_mm(p_lat, v_ref[win, :]) + _mm(p_ctx, v_ctx)) / den
            o_ref[rr * GRID_W:(rr + 1) * GRID_W, :] = unstack_heads(o, GRID_W)


def _na(nqkv, tbl, with_ctx):
    b, s, _ = nqkv.shape
    nt = s // TM
    width = NA_HEADS * HEAD_DIM
    first_tile = 0 if with_ctx else 1
    n_rows = (s - TM) // GRID_W
    return pl.pallas_call(
        functools.partial(_na_kernel, first_tile=first_tile, n_rows=n_rows),
        grid=(b, nt - first_tile),
        in_specs=[pl.BlockSpec((None, TM, width), lambda bi, j: (bi, first_tile + j, 0)),
                  pl.BlockSpec((None, s, width), lambda bi, j: (bi, 0, 1)),
                  pl.BlockSpec((None, s, width), lambda bi, j: (bi, 0, 2)),
                  pl.BlockSpec(tbl.shape, lambda bi, j: (0, 0, 0))],
        out_specs=pl.BlockSpec((None, TM, width), lambda bi, j: (bi, first_tile + j, 0)),
        out_shape=jax.ShapeDtypeStruct((b, s, width), F32),
        compiler_params=pltpu.CompilerParams(vmem_limit_bytes=VMEM_LIMIT),
        name="na",
    )(nqkv, nqkv, nqkv, tbl)


def _ssd_kernel(x_ref, gc_ref, gr_ref, dsk_ref, y_ref, st_scr):
    d = pl.program_id(0)
    j = pl.program_id(1)
    b = pl.program_id(2)
    rev = d == 1
    width = SSD_HEADS * HEAD_DIM
    gn = SSD_GROUPS * SSD_STATE
    per_group = SSD_HEADS // SSD_GROUPS
    nsub = TM // SUB

    @pl.when(j == 0)
    def _():
        st_scr[b] = jnp.zeros(st_scr.shape[1:], F32)

    mask = _order_mask(rev, SUB)
    head = _head_of_lane(width)
    st = st_scr[b]
    skip = jnp.where(rev, 0.0, 1.0) * dsk_ref[...]

    for step in range(nsub):
        ci = jnp.where(rev, nsub - 1 - step, step)
        rows = pl.ds(pl.multiple_of(ci * SUB, SUB), SUB)
        xs = x_ref[rows, 0:width]
        bm = x_ref[rows, width:width + gn].astype(MXU_DTYPE)
        cm = x_ref[rows, width + gn:width + 2 * gn].astype(MXU_DTYPE)
        gcol = gc_ref[rows, :]
        grow = gr_ref[ci]
        xsb = xs.astype(MXU_DTYPE)

        e_cols, w_cols, last_cols, dmats = [], [], [], []
        for h in range(SSD_HEADS):
            acs_col = gcol[:, G_ACS + h:G_ACS + h + 1]
            acs_row = grow[G_ACS + h:G_ACS + h + 1, :]
            a_last = jnp.where(rev, acs_col[0:1, :], acs_col[SUB - 1:SUB, :])
            e_cols.append(jnp.exp(acs_col))
            w_cols.append(jnp.exp(a_last - acs_col) * gcol[:, G_DT + h:G_DT + h + 1])
            last_cols.append(jnp.exp(a_last))
            dmats.append(jnp.exp(jnp.where(mask, acs_col - acs_row, NEG)) * grow[G_DT + h:G_DT + h + 1, :])

        e_full = _expand_heads(e_cols, width)
        xw = xs * _expand_heads(w_cols, width)
        last_full = _expand_heads(last_cols, width)

        y_parts, st_parts = [], []
        for g in range(SSD_GROUPS):
            b_g = bm[:, g * SSD_STATE:(g + 1) * SSD_STATE]
            c_g = cm[:, g * SSD_STATE:(g + 1) * SSD_STATE]
            lanes = slice(g * per_group * HEAD_DIM, (g + 1) * per_group * HEAD_DIM)
            cb = _mm_nt(c_g, b_g)
            y_g = _mm(c_g, st[:, lanes]) * e_full[:, lanes]
            head_g = head[:, lanes]
            for hh in range(per_group):
                h = g * per_group + hh
                y_g = y_g + jnp.where(head_g == h, _mm(cb * dmats[h], xsb[:, lanes]), 0.0)
            st_parts.append(st[:, lanes] * last_full[:, lanes] + _mm_tn(b_g, xw[:, lanes]))
            y_parts.append(y_g)
        st = jnp.concatenate(st_parts, axis=1)
        y_ref[rows, :] = jnp.concatenate(y_parts, axis=1) + skip * xs
    st_scr[b] = st


def _ssd(xbc, gc, gr, dskip):
    b, s, cw = xbc.shape
    nt = s // TM
    width = SSD_HEADS * HEAD_DIM
    return pl.pallas_call(
        _ssd_kernel,
        grid=(2, nt, b),
        in_specs=_scan_specs(nt, (cw,)) + [pl.BlockSpec(dskip.shape, lambda d, j, bi: (0, 0))],
        out_specs=pl.BlockSpec((None, None, TM, width), lambda d, j, bi: (d, bi, _tile_of(d, j, nt), 0)),
        out_shape=jax.ShapeDtypeStruct((2, b, s, width), F32),
        scratch_shapes=[pltpu.VMEM((b, SSD_STATE, width), F32)],
        compiler_params=pltpu.CompilerParams(vmem_limit_bytes=VMEM_LIMIT),
        name="ssd",
    )(xbc, gc, gr, dskip)


def _merge_kernel(x_ref, mod_ref, nw_ref, hm_ref, mo_ref, mnw_ref, ys_ref, su_ref, s5d_ref, glu_ref, na_ref,
                  yd_ref, dz_ref, dnw_ref, wg_ref, wa_ref, wb_ref, wc_ref, wd_ref, wo_ref, o_ref):
    x = x_ref[...]
    d_model = x.shape[1]
    h = _norm_mod(x, nw_ref[...], mod_ref[1:2, :], mod_ref[0:1, :]).astype(MXU_DTYPE)

    wm = MLSTM_HEADS * HEAD_DIM
    hm = (hm_ref[0] + hm_ref[1]) * _sigmoid(mo_ref[...])
    ms = _mm_exact_rhs(hm * hm, _same_head(wm, wm)) * (1.0 / HEAD_DIM)
    ya = hm * lax.rsqrt(ms + EPS) * mnw_ref[...]

    u = su_ref[...]
    ys = ys_ref[0] + ys_ref[1] + s5d_ref[...] * u
    ys = 0.5 * ys * (1.0 + jnp.tanh(math.sqrt(2.0 / math.pi) * (ys + 0.044715 * (ys * ys * ys))))
    ab = _mm(ys, glu_ref[...])
    ws = S5_GROUPS * S5_GROUP
    yb = ab[:, :ws] * _sigmoid(ab[:, ws:])

    yc = na_ref[...]

    yd = (yd_ref[0] + yd_ref[1]) * _silu(dz_ref[...])
    yd = yd * lax.rsqrt(jnp.mean(yd * yd, axis=-1, keepdims=True) + EPS) * dnw_ref[...]

    m = None
    for i, (y, w_ref) in enumerate(((ya, wa_ref), (yb, wb_ref), (yc, wc_ref), (yd, wd_ref))):
        gate = _sigmoid(jnp.dot(h, wg_ref[:, i * d_model:(i + 1) * d_model], preferred_element_type=F32))
        term = gate * _mm(y, w_ref[...])
        m = term if m is None else m + term
    o_ref[...] = x + mod_ref[2:3, :] * _mm(m, wo_ref[...])


def _merge(seq, mod, norm_w, hm, mo, mnw, ys, su_tm, s5d, glu_w, yna, yd, dz, dnw, wg, wa, wb, wc, wd, wo,
           with_ctx):
    b, s, d = seq.shape
    nt = s // TM
    first = 0 if with_ctx else 1

    def tok(width):
        return pl.BlockSpec((None, TM, width), lambda bi, j: (bi, first + j, 0))

    def tok2(width):
        return pl.BlockSpec((2, None, TM, width), lambda bi, j: (0, bi, first + j, 0))

    def const(arr):
        return pl.BlockSpec(arr.shape, lambda bi, j: (0,) * arr.ndim, pipeline_mode=pl.Buffered(1))

    ws = S5_GROUPS * S5_GROUP
    return pl.pallas_call(
        _merge_kernel,
        grid=(b, nt - first),
        in_specs=[tok(d),
                  pl.BlockSpec((None, None, 6, d), lambda bi, j: (bi, jnp.minimum(first + j, 1), 0, 0)),
                  const(norm_w),
                  tok2(MLSTM_HEADS * HEAD_DIM), tok(W_O), const(mnw),
                  pl.BlockSpec((2, TM, ws), lambda bi, j: (0, first + j, bi)),
                  pl.BlockSpec((TM, ws), lambda bi, j: (first + j, bi)),
                  const(s5d), const(glu_w),
                  tok(NA_HEADS * HEAD_DIM),
                  tok2(SSD_HEADS * HEAD_DIM), tok(W_DZ), const(dnw),
                  const(wg), const(wa), const(wb), const(wc), const(wd), const(wo)],
        out_specs=tok(d),
        out_shape=jax.ShapeDtypeStruct((b, s, d), F32),
        input_output_aliases={0: 0},
        compiler_params=pltpu.CompilerParams(vmem_limit_bytes=VMEM_LIMIT),
        name="merge",
    )(seq, mod, norm_w, hm, mo, mnw, ys, su_tm, s5d, glu_w, yna, yd, dz, dnw, wg, wa, wb, wc, wd, wo)


def _ffn_kernel(x_ref, mod_ref, nw_ref, wi_ref, wo_ref, fw_ref, o_ref, *, final):
    x = x_ref[...]
    hidden = wo_ref.shape[0]
    h = _norm_mod(x, nw_ref[...], mod_ref[4:5, :], mod_ref[3:4, :]).astype(MXU_DTYPE)
    a = jnp.dot(h, wi_ref[:, :hidden], preferred_element_type=F32)
    g = jnp.dot(h, wi_ref[:, hidden:], preferred_element_type=F32)
    y = x + mod_ref[5:6, :] * _mm(_silu(a) * g, wo_ref[...])
    if final:
        y = y * lax.rsqrt(jnp.mean(y * y, axis=-1, keepdims=True) + EPS) * fw_ref[...]
    o_ref[...] = y


def _ffn(seq, mod, norm_w, wi, wo, final_w, final):
    b, s, d = seq.shape
    nt = s // TM
    first = 1 if final else 0

    def const(arr):
        return pl.BlockSpec(arr.shape, lambda bi, j: (0,) * arr.ndim, pipeline_mode=pl.Buffered(1))

    tok_in = pl.BlockSpec((None, TM, d), lambda bi, j: (bi, first + j, 0))
    if final:
        out_spec = pl.BlockSpec((None, TM, d), lambda bi, j: (bi, j, 0))
        out_shape = jax.ShapeDtypeStruct((b, s - TM, d), F32)
        aliases = {}
    else:
        out_spec = tok_in
        out_shape = jax.ShapeDtypeStruct((b, s, d), F32)
        aliases = {0: 0}
    return pl.pallas_call(
        functools.partial(_ffn_kernel, final=final),
        grid=(b, nt - first),
        in_specs=[tok_in,
                  pl.BlockSpec((None, None, 6, d), lambda bi, j: (bi, jnp.minimum(first + j, 1), 0, 0)),
                  const(norm_w), const(wi), const(wo), const(final_w)],
        out_specs=out_spec,
        out_shape=out_shape,
        input_output_aliases=aliases,
        compiler_params=pltpu.CompilerParams(vmem_limit_bytes=VMEM_LIMIT),
        name="ffn_final" if final else "ffn",
    )(seq, mod, norm_w, wi, wo, final_w)


def _rope_tables(t, width):
    nf = HEAD_DIM // 4
    inv = ROPE_THETA ** (-jnp.arange(nf, dtype=F32) / nf)
    tok = jnp.arange(t)
    ang_r = (tok // GRID_W).astype(F32)[:, None] * inv
    ang_c = (tok % GRID_W).astype(F32)[:, None] * inv
    cos_h = jnp.concatenate([jnp.cos(ang_r)] * 2 + [jnp.cos(ang_c)] * 2, axis=1)
    sin_h = jnp.concatenate([-jnp.sin(ang_r), jnp.sin(ang_r), -jnp.sin(ang_c), jnp.sin(ang_c)], axis=1)
    cos_x = jnp.tile(cos_h, (1, width // HEAD_DIM))
    sin_x = jnp.tile(sin_h, (1, width // HEAD_DIM))
    cos_t = jnp.concatenate([jnp.ones((TM, width), F32), cos_x], axis=0)
    sin_t = jnp.concatenate([jnp.zeros((TM, width), F32), sin_x], axis=0)
    scale = HEAD_DIM ** -0.5
    return (jnp.concatenate([cos_t * scale, cos_t], axis=1), jnp.concatenate([sin_t * scale, sin_t], axis=1))


def _na_bias_tables(rpb):
    col = jnp.arange(GRID_W)
    col0 = jnp.clip(col - NA_KW // 2, 0, GRID_W - NA_KW)
    in_win = (col[None, :] >= col0[:, None]) & (col[None, :] < col0[:, None] + NA_KW)
    dc = jnp.clip(col[None, :] - col[:, None], -(NA_KW - 1), NA_KW - 1) + (NA_KW - 1)
    dr = jnp.arange(NA_KH)[:, None] + jnp.arange(NA_KH)[None, :]
    bias = rpb.astype(F32)[:, dr][..., dc]
    bias = jnp.where(in_win[None, None, None], bias, NEG)
    return bias.transpose(1, 0, 3, 2, 4).reshape(NA_KH, NA_HEADS * GRID_W, NA_KH * GRID_W)


def _block_diag(blocks):
    g, r, c = blocks.shape
    eye = jnp.eye(g, dtype=blocks.dtype)
    return (eye[:, None, :, None] * blocks[:, :, None, :]).reshape(g * r, g * c)


def _lanes(vec, at, width=W_SMALL):
    return jnp.zeros((1, width), F32).at[0, at:at + vec.shape[0]].set(vec.astype(F32))


def _gate_perm():
    perm = jnp.zeros((2, 2 * W_SMALL, W_SMALL), F32)
    for d in range(2):
        for h in range(MLSTM_HEADS):
            perm = perm.at[d, L_MI + d * MLSTM_HEADS + h, G_LI + h].set(1.0)
            perm = perm.at[d, L_MF + d * MLSTM_HEADS + h, G_BCUM + h].set(1.0)
        for h in range(SSD_HEADS):
            perm = perm.at[d, L_DDT + d * SSD_HEADS + h, G_ACS + h].set(1.0)
            perm = perm.at[d, W_SMALL + L_DDT + d * SSD_HEADS + h, G_DT + h].set(1.0)
    return perm.astype(MXU_DTYPE)


def kernel(x, c, ctx, c_ctx, ada_w, ada_b, norm1_w, norm2_w, w_in, mlstm_conv_w, mlstm_conv_b, mlstm_ib, mlstm_fb, mlstm_norm_w, s5_lam_re, s5_lam_im, s5_log_dt, s5_b_re, s5_b_im, s5_c_re, s5_c_im, s5_d, s5_glu_w, na_rpb, ssd_conv_w, ssd_conv_b, ssd_a_log, ssd_dt_bias, ssd_d, ssd_norm_w, w_branch_a, w_branch_b, w_branch_c, w_branch_d, w_out, ffn_w_in, ffn_w_out, final_norm_w):
    b, t, d = x.shape
    depth = w_in.shape[0]
    assert ctx.shape[1] == TM and t % TM == 0 and t % GRID_W == 0 and b % 8 == 0
    assert t // GRID_W >= NA_KH

    seq = jnp.concatenate([ctx, x], axis=1)

    pad = (-(b + 1)) % 8
    cc = jnp.concatenate([c, c_ctx[None, :], jnp.zeros((pad, d), F32)], axis=0)
    mod_all = _adaln(cc, ada_w, ada_b)
    mod_x = mod_all[:, :b].reshape(depth, b, 1, 6, d)
    mod_c = jnp.broadcast_to(mod_all[:, b].reshape(depth, 1, 1, 6, d), (depth, b, 1, 6, d))
    mod = jnp.concatenate([mod_c, mod_x], axis=2)

    cos_t, sin_t = _rope_tables(t, MLSTM_HEADS * HEAD_DIM)
    perm = _gate_perm()
    n_state = S5_GROUPS * S5_STATE

    for l in range(depth):
        with_ctx = l < depth - 1
        wl = w_in[l]
        w_small = jnp.concatenate([wl[:, 1024:1040], wl[:, 3600:3616], jnp.zeros((d, W_SMALL - 32), F32)], axis=1)
        w_proj = jnp.concatenate([wl[:, 0:512], wl[:, 2576:3600], wl[:, 512:768], wl[:, 768:1024], w_small,
                                  wl[:, 1040:1296], wl[:, 1296:2064], wl[:, 2064:2576]], axis=1).astype(MXU_DTYPE)
        w_gate = wl[:, 3616:].astype(MXU_DTYPE)

        cv, v, mo, small, su_tm, nqkv, dz = _proj(seq, mod[l], norm1_w[l][None, :], w_proj)

        conv_w = jnp.concatenate([mlstm_conv_w[l], ssd_conv_w[l]], axis=1)
        conv_w = jnp.concatenate([conv_w, jnp.zeros((1, W_CV), F32)], axis=0)
        conv_b = jnp.concatenate([mlstm_conv_b[l], ssd_conv_b[l]])[None, :]
        gbias = (_lanes(mlstm_ib[l].reshape(-1), L_MI) + _lanes(mlstm_fb[l].reshape(-1), L_MF)
                 + _lanes(ssd_dt_bias[l].reshape(-1), L_DDT))
        qk, xbc, gc, gr = _prep(cv, small, conv_w, conv_b, cos_t, sin_t, gbias,
                                _lanes(ssd_a_log[l].reshape(-1), L_DDT), perm)

        hm = _mlstm(qk, v, gc, gr)

        braw_re = _block_diag(jnp.swapaxes(s5_b_re[l], 1, 2))
        braw_im = _block_diag(jnp.swapaxes(s5_b_im[l], 1, 2))
        cw = jnp.concatenate([_block_diag(jnp.swapaxes(s5_c_re[l], 1, 2)),
                              -_block_diag(jnp.swapaxes(s5_c_im[l], 1, 2))], axis=0).astype(MXU_DTYPE)
        ys = _s5(su_tm.reshape(-1, W_SU), s5_lam_re[l].reshape(2, 1, n_state), s5_lam_im[l].reshape(2, 1, n_state),
                 jnp.repeat(s5_log_dt[l], S5_STATE, axis=1).reshape(2, 1, n_state), braw_re, braw_im, cw, b)
        ys = ys.reshape(2, -1, b * W_SU)

        yna = _na(nqkv, _na_bias_tables(na_rpb[l]), with_ctx)

        yd = _ssd(xbc, gc, gr, jnp.repeat(ssd_d[l], HEAD_DIM)[None, :])

        seq = _merge(seq, mod[l], norm1_w[l][None, :], hm, mo, mlstm_norm_w[l][None, :], ys, su_tm,
                     s5_d[l][None, :], s5_glu_w[l].astype(MXU_DTYPE), yna, yd, dz, ssd_norm_w[l][None, :],
                     w_gate, w_branch_a[l].astype(MXU_DTYPE), w_branch_b[l].astype(MXU_DTYPE),
                     w_branch_c[l].astype(MXU_DTYPE), w_branch_d[l].astype(MXU_DTYPE), w_out[l].astype(MXU_DTYPE),
                     with_ctx)
        seq = _ffn(seq, mod[l], norm2_w[l][None, :], ffn_w_in[l].astype(MXU_DTYPE), ffn_w_out[l].astype(MXU_DTYPE),
                   final_norm_w[None, :], not with_ctx)
    return seq
```

```python
import functools
import math

import jax
import jax.numpy as jnp
from jax import lax
from jax.experimental import pallas as pl
from jax.experimental.pallas import tpu as pltpu

F32 = jnp.float32
MXU_DTYPE = jnp.bfloat16
HIGHEST = lax.Precision.HIGHEST

GRID_W = 64
EPS = 1e-6
CONV_K = 7
ROPE_THETA = 10000.0
HEAD_DIM = 64
MLSTM_HEADS = 4
S5_GROUPS = 16
S5_GROUP = 16
S5_STATE = 64
NA_HEADS = 4
NA_KH = 8
NA_KW = 16
SSD_HEADS = 8
SSD_GROUPS = 2
SSD_STATE = 128

TM = 256
NB = 2
NBS = 4
SUB = 128
HALO = 8
S5_CHUNK = 16
NEG = -1e30
VMEM_LIMIT = 56 * 1024 * 1024

W_QK, W_XBC, W_V, W_O, W_SMALL, W_SU, W_NQKV, W_DZ = 512, 1024, 256, 256, 128, 256, 768, 512
W_CV = W_QK + W_XBC
PROJ_WIDTHS = (W_CV, W_V, W_O, W_SMALL, W_SU, W_NQKV, W_DZ)
L_MI, L_MF, L_DDT = 0, 8, 16
G_LI, G_BCUM, G_ACS, G_DT = 0, 4, 8, 16


def _mm(a, b):
    return jnp.dot(a.astype(MXU_DTYPE), b.astype(MXU_DTYPE), preferred_element_type=F32)


def _mm_nt(a, b):
    return lax.dot_general(a.astype(MXU_DTYPE), b.astype(MXU_DTYPE), (((1,), (1,)), ((), ())),
                           preferred_element_type=F32)


def _mm_tn(a, b):
    return lax.dot_general(a.astype(MXU_DTYPE), b.astype(MXU_DTYPE), (((0,), (0,)), ((), ())),
                           preferred_element_type=F32)


def _mm_f32(a, b):
    return jnp.dot(a, b, preferred_element_type=F32, precision=HIGHEST)


def _split3(x):
    hi = x.astype(MXU_DTYPE)
    r1 = x - hi.astype(F32)
    mid = r1.astype(MXU_DTYPE)
    lo = (r1 - mid.astype(F32)).astype(MXU_DTYPE)
    return hi, mid, lo


def _as_01(sel):
    if sel.dtype == jnp.bool_:
        sel = jnp.where(sel, 1.0, 0.0)
    return sel.astype(MXU_DTYPE)


def _mm_exact_rhs(x, sel):
    sel = _as_01(sel)
    return sum(jnp.dot(p, sel, preferred_element_type=F32) for p in _split3(x))


def _mm_exact_lhs(sel, x):
    sel = _as_01(sel)
    return sum(jnp.dot(sel, p, preferred_element_type=F32) for p in _split3(x))


def _sigmoid(x):
    return 1.0 / (1.0 + jnp.exp(-x))


def _silu(x):
    return x * _sigmoid(x)


def _softplus(x):
    return jnp.maximum(x, 0.0) + jnp.log(1.0 + jnp.exp(-jnp.abs(x)))


def _iota(shape, dim):
    return lax.broadcasted_iota(jnp.int32, shape, dim)


def _head_of_lane(width):
    return jnp.right_shift(_iota((1, width), 1), 6)


def _same_head(rows, cols):
    return jnp.right_shift(_iota((rows, cols), 0), 6) == jnp.right_shift(_iota((rows, cols), 1), 6)


def _expand_heads(cols, width):
    head = _head_of_lane(width)
    out = jnp.broadcast_to(cols[0], (cols[0].shape[0], width))
    for h in range(1, len(cols)):
        out = jnp.where(head == h, cols[h], out)
    return out


def _stack(fn):
    return jnp.concatenate([fn(i) for i in range(NB)], axis=0)


def _round_robin(chains):
    live = list(chains)
    while live:
        for chain in list(live):
            try:
                next(chain)
            except StopIteration:
                live.remove(chain)


def _tile_of(d, j, nt):
    return jnp.where(d == 0, j, jnp.where(j == 0, 0, nt - j))


def _norm_mod(x, w, scale, shift):
    y = x * lax.rsqrt(jnp.mean(x * x, axis=-1, keepdims=True) + EPS) * w
    return y * (1.0 + scale) + shift


def _order_mask(rev, n):
    diff = (_iota((n, n), 1) - _iota((n, n), 0)) * jnp.where(rev, -1, 1)
    return diff <= 0


def _const_spec(arr, n_grid, single=False):
    kwargs = {"pipeline_mode": pl.Buffered(1)} if single else {}
    return pl.BlockSpec(arr.shape, lambda *_: (0,) * arr.ndim, **kwargs)


def _mod_spec(d, first):
    return pl.BlockSpec((NB, None, 6, d), lambda bp, j: (bp, jnp.minimum(first + j, 1), 0, 0))


def _adaln_kernel(c_ref, w_ref, b_ref, o_ref):
    o_ref[...] = _mm_f32(_silu(c_ref[...]), w_ref[...]) + b_ref[...]


def _adaln(cc, ada_w, ada_b):
    depth, d, n = ada_w.shape
    tn = 768
    return pl.pallas_call(
        _adaln_kernel,
        grid=(depth, n // tn),
        in_specs=[pl.BlockSpec(cc.shape, lambda l, i: (0, 0)),
                  pl.BlockSpec((None, d, tn), lambda l, i: (l, 0, i)),
                  pl.BlockSpec((None, 1, tn), lambda l, i: (l, 0, i))],
        out_specs=pl.BlockSpec((None, cc.shape[0], tn), lambda l, i: (l, 0, i)),
        out_shape=jax.ShapeDtypeStruct((depth, cc.shape[0], n), F32),
        name="adaln",
    )(cc, ada_w, ada_b.reshape(depth, 1, n))


def _proj_kernel(x_ref, mod_ref, nw_ref, w_ref, *out_refs):
    h = _stack(lambda i: _norm_mod(x_ref[i], nw_ref[...], mod_ref[i, 1:2, :], mod_ref[i, 0:1, :]))
    h = h.astype(MXU_DTYPE)
    off = 0
    for ref, n in zip(out_refs, PROJ_WIDTHS):
        res = jnp.dot(h, w_ref[:, off:off + n], preferred_element_type=F32)
        for i in range(NB):
            ref[i] = res[i * TM:(i + 1) * TM, :]
        off += n


def _proj(seq, mod, norm_w, w_proj):
    b, s, d = seq.shape
    nt = s // TM

    def tok(width):
        return pl.BlockSpec((NB, TM, width), lambda bp, j: (bp, j, 0))

    out_specs = [tok(w) for w in PROJ_WIDTHS]
    out_shape = [jax.ShapeDtypeStruct((b, s, w), F32) for w in PROJ_WIDTHS]
    return pl.pallas_call(
        _proj_kernel,
        grid=(b // NB, nt),
        in_specs=[tok(d), _mod_spec(d, 0), _const_spec(norm_w, 2), _const_spec(w_proj, 2, single=True)],
        out_specs=out_specs,
        out_shape=out_shape,
        compiler_params=pltpu.CompilerParams(vmem_limit_bytes=VMEM_LIMIT),
        name="proj",
    )(seq, mod, norm_w, w_proj)


def _prep_kernel(cv_ref, cvp_ref, cvn_ref, cw_ref, cb_ref, cos_ref, sin_ref, sm_ref, gb_ref, al_ref, perm_ref,
                 pa_ref, pb_ref, one_ref, qk_ref, xbc_ref, gc_ref, gr_ref, la_ref, lb_ref, ext, *, nt):
    t = pl.program_id(1)
    prev_ok = t > 1
    next_ok = (t >= 1) & (t < nt - 1)
    lane = _iota((1, W_SMALL), 1)
    first = jnp.bitwise_and(_iota((1, W_QK), 1), 31) < 16
    r = _iota((TM, TM), 0)
    c = _iota((TM, TM), 1)
    same_chunk = jnp.right_shift(r, 7) == jnp.right_shift(c, 7)
    cum = (lane >= G_BCUM) & (lane < G_DT)

    for i in range(NB):
        ext[i, 0:HALO, :] = jnp.where(prev_ok, cvp_ref[i], 0.0)
        ext[i, HALO:HALO + TM, :] = cv_ref[i]
        ext[i, HALO + TM:, :] = jnp.where(next_ok, cvn_ref[i], 0.0)
        acc = cb_ref[...] + cw_ref[0:1, :] * ext[i, pl.ds(HALO - CONV_K // 2, TM), :]
        for k in range(1, CONV_K):
            acc = acc + cw_ref[k:k + 1, :] * ext[i, pl.ds(HALO - CONV_K // 2 + k, TM), :]
        acc = _silu(acc)
        xbc_ref[i] = acc[:, W_QK:]
        qk = acc[:, :W_QK]
        partner = jnp.where(first, pltpu.roll(qk, W_QK - 16, 1), pltpu.roll(qk, 16, 1))
        qk_ref[i] = qk * cos_ref[...] + partner * sin_ref[...]

        g_all = sm_ref[i] + gb_ref[...]
        dt = _softplus(g_all)
        src = jnp.where(lane < L_MF, g_all,
                        jnp.where(lane < L_DDT, -_softplus(-g_all), dt * (-jnp.exp(al_ref[...]))))
        src = jnp.concatenate([src, dt], axis=1)
        for d in range(2):
            tri = same_chunk & ((c <= r) if d == 0 else (c >= r))
            g = _mm_exact_rhs(src, perm_ref[d])
            g = jnp.where(cum, _mm_exact_lhs(tri, jnp.where(cum, g, 0.0)), g)
            gc_ref[d, i] = g
            a_mat = one_ref[0:1, :] + sum(jnp.dot(p, pa_ref[k], preferred_element_type=F32)
                                           for k, p in enumerate(_split3(g)))
            cv = jnp.where(lane < MLSTM_HEADS, g - pltpu.roll(g, W_SMALL - G_BCUM, 1), -g)
            b_mat = one_ref[1:2, :] + sum(jnp.dot(p, pb_ref[k], preferred_element_type=F32)
                                           for k, p in enumerate(_split3(cv)))
            la_ref[d, i] = a_mat.astype(MXU_DTYPE)
            for ci in range(TM // SUB):
                gr_ref[d, i, ci] = g[ci * SUB:(ci + 1) * SUB, :].T
                lb_ref[d, i, ci] = b_mat[ci * SUB:(ci + 1) * SUB, :].T.astype(MXU_DTYPE)


def _prep(cv, small, conv_w, conv_b, cos_t, sin_t, gbias, alog, perm, pa, pb, ones_ab):
    b, s, _ = cv.shape
    nt = s // TM
    per = TM // HALO
    last = s // HALO - 1
    nsub = TM // SUB
    return pl.pallas_call(
        functools.partial(_prep_kernel, nt=nt),
        grid=(b // NB, nt),
        in_specs=[pl.BlockSpec((NB, TM, W_CV), lambda bp, j: (bp, j, 0)),
                  pl.BlockSpec((NB, HALO, W_CV), lambda bp, j: (bp, jnp.maximum(j * per - 1, 0), 0)),
                  pl.BlockSpec((NB, HALO, W_CV), lambda bp, j: (bp, jnp.minimum((j + 1) * per, last), 0)),
                  _const_spec(conv_w, 2), _const_spec(conv_b, 2),
                  pl.BlockSpec((TM, W_QK), lambda bp, j: (j, 0)),
                  pl.BlockSpec((TM, W_QK), lambda bp, j: (j, 0)),
                  pl.BlockSpec((NB, TM, W_SMALL), lambda bp, j: (bp, j, 0)),
                  _const_spec(gbias, 2), _const_spec(alog, 2), _const_spec(perm, 2),
                  _const_spec(pa, 2), _const_spec(pb, 2), _const_spec(ones_ab, 2)],
        out_specs=[pl.BlockSpec((NB, TM, W_QK), lambda bp, j: (bp, j, 0)),
                   pl.BlockSpec((NB, TM, W_XBC), lambda bp, j: (bp, j, 0)),
                   pl.BlockSpec((2, NB, TM, W_SMALL), lambda bp, j: (0, bp, j, 0)),
                   pl.BlockSpec((2, NB, nsub, W_SMALL, SUB), lambda bp, j: (0, bp, j, 0, 0)),
                   pl.BlockSpec((2, NB, TM, W_SMALL), lambda bp, j: (0, bp, j, 0)),
                   pl.BlockSpec((2, NB, nsub, W_SMALL, SUB), lambda bp, j: (0, bp, j, 0, 0))],
        out_shape=[jax.ShapeDtypeStruct((b, s, W_QK), F32),
                   jax.ShapeDtypeStruct((b, s, W_XBC), F32),
                   jax.ShapeDtypeStruct((2, b, s, W_SMALL), F32),
                   jax.ShapeDtypeStruct((2, b, s // SUB, W_SMALL, SUB), F32),
                   jax.ShapeDtypeStruct((2, b, s, W_SMALL), MXU_DTYPE),
                   jax.ShapeDtypeStruct((2, b, s // SUB, W_SMALL, SUB), MXU_DTYPE)],
        scratch_shapes=[pltpu.VMEM((NB, TM + 2 * HALO, W_CV), F32)],
        compiler_params=pltpu.CompilerParams(vmem_limit_bytes=VMEM_LIMIT),
        name="prep",
    )(cv, cv, cv, conv_w, conv_b, cos_t, sin_t, small, gbias, alog, perm, pa, pb, ones_ab)


def _scan_specs(nt, widths, kinds):
    nsub = TM // SUB
    specs = [pl.BlockSpec((NBS, TM, w), lambda d, j, bp: (bp, _tile_of(d, j, nt), 0)) for w in widths]
    for kind in kinds:
        if kind == "col":
            specs.append(pl.BlockSpec((None, NBS, TM, W_SMALL), lambda d, j, bp: (d, bp, _tile_of(d, j, nt), 0)))
        else:
            specs.append(pl.BlockSpec((None, NBS, nsub, W_SMALL, SUB),
                                      lambda d, j, bp: (d, bp, _tile_of(d, j, nt), 0, 0)))
    return specs


def _mlstm_kernel(qk_ref, v_ref, gc_ref, la_ref, lb_ref, h_ref, c_scr, n_scr, m_scr):
    d = pl.program_id(0)
    j = pl.program_id(1)
    bp = pl.program_id(2)
    rev = d == 1
    width = MLSTM_HEADS * HEAD_DIM
    nsub = TM // SUB
    heads = list(range(MLSTM_HEADS))
    stacked = (MLSTM_HEADS * SUB, SUB)
    t_minus_s = (jnp.bitwise_and(_iota(stacked, 0), SUB - 1) - _iota(stacked, 1)) * jnp.where(rev, -1, 1)
    mask_s = t_minus_s >= 0
    head = _head_of_lane(width)
    same_head = _same_head(width, width)
    lane = _iota((1, W_SMALL), 1)
    valid = lane < MLSTM_HEADS
    group = jnp.right_shift(lane, 3)
    to_heads = _as_01(jnp.right_shift(_iota((W_SMALL, width), 1), 6) == _iota((W_SMALL, width), 0))
    from_heads = _as_01(jnp.right_shift(_iota((width, W_SMALL), 0), 6) == _iota((width, W_SMALL), 1))
    ones_sq = jnp.ones((SUB, W_SMALL), MXU_DTYPE)
    ones_rows = jnp.ones((8, SUB), MXU_DTYPE)

    @pl.when(j == 0)
    def _():
        for i in range(NBS):
            c_scr[bp * NBS + i] = jnp.zeros(c_scr.shape[1:], F32)
            n_scr[bp * NBS + i] = jnp.zeros(n_scr.shape[1:], F32)
            m_scr[bp * NBS + i] = jnp.zeros(m_scr.shape[1:], F32)

    def chain(i):
        bi = bp * NBS + i
        c_st = c_scr[bi]
        n_st = n_scr[bi, 0:1, :]
        m_st = m_scr[bi, 0:1, :]
        for step in range(nsub):
            ci = jnp.where(rev, nsub - 1 - step, step)
            rows = pl.ds(pl.multiple_of(ci * SUB, SUB), SUB)
            q = qk_ref[i, rows, 0:width]
            k = qk_ref[i, rows, width:2 * width]
            qb = q.astype(MXU_DTYPE)
            kb = k.astype(MXU_DTYPE)
            vb = v_ref[i, rows, :].astype(MXU_DTYPE)
            li = gc_ref[i, rows, :]
            bc = pltpu.roll(li, W_SMALL - G_BCUM, 1)
            la = la_ref[i, rows, :]
            lb = lb_ref[i, ci]

            la_s = jnp.concatenate([jnp.where(group == h, la, 0) for h in heads], axis=0)
            q_s = jnp.concatenate([jnp.where(head == h, qb, 0) for h in heads], axis=0)
            log_w = jnp.where(mask_s, jnp.dot(la_s, lb, preferred_element_type=F32), NEG)
            yield
            inter = bc + m_st
            inter_s = jnp.concatenate([inter[:, h:h + 1] for h in heads], axis=0)
            m_t_s = jnp.maximum(inter_s, jnp.max(log_w, axis=1, keepdims=True))
            sm = (_mm_nt(q_s, kb) * jnp.exp(log_w - m_t_s)).astype(MXU_DTYPE)
            yield
            pv = jnp.dot(sm, vb, preferred_element_type=F32)
            rs = jnp.dot(sm, ones_sq, preferred_element_type=F32)
            yield
            num = pv[0:SUB, :]
            den = rs[0:SUB, :]
            m_t = jnp.broadcast_to(m_t_s[0:SUB, :], (SUB, W_SMALL))
            for h in heads[1:]:
                blk = slice(h * SUB, (h + 1) * SUB)
                num = jnp.where(head == h, pv[blk, :], num)
                den = jnp.where(lane == h, rs[blk, :], den)
                m_t = jnp.where(lane == h, m_t_s[blk, :], m_t)
            g = jnp.exp(inter - m_t)
            den = den + g * jnp.dot((q * n_st).astype(MXU_DTYPE), from_heads, preferred_element_type=F32)
            inv = 1.0 / jnp.maximum(jnp.abs(den), jnp.exp(-m_t))
            yield
            inv_full = _mm_exact_rhs(jnp.where(valid, inv, 0.0), to_heads)
            ig_full = _mm_exact_rhs(jnp.where(valid, inv * g, 0.0), to_heads)
            h_ref[i, rows, :] = num * inv_full + _mm(qb, c_st) * ig_full
            yield

            b_last = jnp.where(rev, bc[0:1, :], bc[SUB - 1:SUB, :])
            log_k = b_last - bc + li
            m_new = jnp.maximum(b_last + m_st, jnp.max(log_k, axis=0, keepdims=True))
            wk_full = _mm_exact_rhs(jnp.where(valid, jnp.exp(log_k - m_new), 0.0), to_heads)
            decay = jnp.broadcast_to(jnp.where(valid, jnp.exp(b_last + m_st - m_new), 0.0), (8, W_SMALL))
            decay_full = _mm_exact_rhs(decay, to_heads)[0:1, :]
            kw = (k * wk_full).astype(MXU_DTYPE)
            yield
            c_st = c_st * decay_full + jnp.where(same_head, _mm_tn(kw, vb), 0.0)
            n_st = n_st * decay_full + jnp.dot(ones_rows, kw, preferred_element_type=F32)[0:1, :]
            m_st = jnp.where(valid, m_new, 0.0)
            yield

        c_scr[bi] = c_st
        n_scr[bi, 0:1, :] = n_st
        m_scr[bi, 0:1, :] = m_st

    _round_robin([chain(i) for i in range(NBS)])


def _mlstm(qk, v, gc, la, lb):
    b, s, _ = qk.shape
    nt = s // TM
    width = MLSTM_HEADS * HEAD_DIM
    return pl.pallas_call(
        _mlstm_kernel,
        grid=(2, nt, b // NBS),
        in_specs=_scan_specs(nt, (2 * width, width), ("col", "col", "row")),
        out_specs=pl.BlockSpec((None, NBS, TM, width), lambda d, j, bp: (d, bp, _tile_of(d, j, nt), 0)),
        out_shape=jax.ShapeDtypeStruct((2, b, s, width), F32),
        scratch_shapes=[pltpu.VMEM((b, width, width), F32), pltpu.VMEM((b, 8, width), F32),
                        pltpu.VMEM((b, 8, W_SMALL), F32)],
        compiler_params=pltpu.CompilerParams(vmem_limit_bytes=VMEM_LIMIT),
        name="mlstm",
    )(qk, v, gc, la, lb)


def _s5_kernel(x_ref, toep_ref, wend_ref, wout_ref, al_ref, y_ref, e_scr, hin_scr, *, nb, n_ctx, n_chunks):
    q = 2 * S5_STATE
    x = x_ref[...]
    e_scr[...] = jnp.dot(x, wend_ref[...], preferred_element_type=F32)
    al = al_ref[...]
    a_fr, a_fi, a_br, a_bi = (jnp.broadcast_to(al[k:k + 1, :], (nb, q)) for k in range(4))

    def step(k, carry):
        f_re, f_im, b_re, b_im = carry
        kb = jnp.where(k < n_ctx, n_ctx - 1 - k, n_chunks - 1 - (k - n_ctx))
        rf = pl.ds(pl.multiple_of(k * nb, nb), nb)
        rb = pl.ds(pl.multiple_of(kb * nb, nb), nb)
        hin_scr[rf, 0:q] = f_re
        hin_scr[rf, q:2 * q] = f_im
        hin_scr[rb, 2 * q:3 * q] = b_re
        hin_scr[rb, 3 * q:4 * q] = b_im
        return (a_fr * f_re - a_fi * f_im + e_scr[rf, 0:q], a_fr * f_im + a_fi * f_re + e_scr[rf, q:2 * q],
                a_br * b_re - a_bi * b_im + e_scr[rb, 2 * q:3 * q], a_br * b_im + a_bi * b_re + e_scr[rb, 3 * q:4 * q])

    zero = jnp.zeros((nb, q), F32)
    lax.fori_loop(0, n_chunks, step, (zero, zero, zero, zero))
    y_ref[...] = (jnp.dot(x, toep_ref[...], preferred_element_type=F32)
                  + jnp.dot(hin_scr[...].astype(MXU_DTYPE), wout_ref[...], preferred_element_type=F32))


def _s5(x_pairs, toep, wend, wout, al, nb):
    n_pairs, rows, width = x_pairs.shape
    n_chunks = rows // nb

    def per_pair(arr):
        return pl.BlockSpec((None,) + arr.shape[1:], lambda p: (p,) + (0,) * (arr.ndim - 1))

    return pl.pallas_call(
        functools.partial(_s5_kernel, nb=nb, n_ctx=TM // S5_CHUNK, n_chunks=n_chunks),
        grid=(n_pairs,),
        in_specs=[per_pair(x_pairs), per_pair(toep), per_pair(wend), per_pair(wout), per_pair(al)],
        out_specs=pl.BlockSpec((None, rows, width), lambda p: (p, 0, 0)),
        out_shape=jax.ShapeDtypeStruct((n_pairs, rows, width), F32),
        scratch_shapes=[pltpu.VMEM((rows, 4 * 2 * S5_STATE), F32), pltpu.VMEM((rows, 4 * 2 * S5_STATE), F32)],
        compiler_params=pltpu.CompilerParams(vmem_limit_bytes=VMEM_LIMIT),
        name="s5",
    )(x_pairs, toep, wend, wout, al)


def _na_kernel(q_ref, k_ref, v_ref, tbl_ref, o_ref, *, first_tile, n_rows):
    t = first_tile + pl.program_id(1)
    width = NA_HEADS * HEAD_DIM
    head = _head_of_lane(width)
    scale = HEAD_DIM ** -0.5
    k_ctx = k_ref[0:TM, :].astype(MXU_DTYPE)
    v_ctx = v_ref[0:TM, :].astype(MXU_DTYPE)

    def stack_heads(q):
        return jnp.concatenate([jnp.where(head == h, q, 0.0) for h in range(NA_HEADS)], axis=0).astype(MXU_DTYPE)

    def unstack_heads(o, n):
        acc = o[0:n, :]
        for h in range(1, NA_HEADS):
            acc = jnp.where(head == h, o[h * n:(h + 1) * n, :], acc)
        return acc

    @pl.when(t == 0)
    def _():
        s = _mm_nt(stack_heads(q_ref[...] * scale), k_ctx)
        p = jnp.exp(s - jnp.max(s, axis=1, keepdims=True))
        o = _mm(p, v_ctx) / jnp.sum(p, axis=1, keepdims=True)
        o_ref[...] = unstack_heads(o, TM)

    @pl.when(t > 0)
    def _():
        rows_per_tile = TM // GRID_W
        n_lat = NA_KH * GRID_W
        for rr in range(rows_per_tile):
            r = (t - 1) * rows_per_tile + rr
            row_start = jnp.clip(r - NA_KH // 2, 0, n_rows - NA_KH)
            off = (NA_KH - 1) - (r - row_start)
            win = pl.ds(pl.multiple_of(TM + row_start * GRID_W, GRID_W), n_lat)
            qs = stack_heads(q_ref[rr * GRID_W:(rr + 1) * GRID_W, :] * scale)
            s_lat = _mm_nt(qs, k_ref[win, :]) + tbl_ref[off]
            s_ctx = _mm_nt(qs, k_ctx)
            m = jnp.maximum(jnp.max(s_lat, axis=1, keepdims=True), jnp.max(s_ctx, axis=1, keepdims=True))
            p_lat = jnp.exp(s_lat - m)
            p_ctx = jnp.exp(s_ctx - m)
            den = jnp.sum(p_lat, axis=1, keepdims=True) + jnp.sum(p_ctx, axis=1, keepdims=True)
            o = (_mm(p_lat, v_ref[win, :]) + _mm(p_ctx, v_ctx)) / den
            o_ref[rr * GRID_W:(rr + 1) * GRID_W, :] = unstack_heads(o, GRID_W)


def _na(nqkv, tbl, with_ctx):
    b, s, _ = nqkv.shape
    nt = s // TM
    width = NA_HEADS * HEAD_DIM
    first_tile = 0 if with_ctx else 1
    n_rows = (s - TM) // GRID_W
    return pl.pallas_call(
        functools.partial(_na_kernel, first_tile=first_tile, n_rows=n_rows),
        grid=(b, nt - first_tile),
        in_specs=[pl.BlockSpec((None, TM, width), lambda bi, j: (bi, first_tile + j, 0)),
                  pl.BlockSpec((None, s, width), lambda bi, j: (bi, 0, 1)),
                  pl.BlockSpec((None, s, width), lambda bi, j: (bi, 0, 2)),
                  _const_spec(tbl, 2)],
        out_specs=pl.BlockSpec((None, TM, width), lambda bi, j: (bi, first_tile + j, 0)),
        out_shape=jax.ShapeDtypeStruct((b, s, width), F32),
        compiler_params=pltpu.CompilerParams(vmem_limit_bytes=VMEM_LIMIT),
        name="na",
    )(nqkv, nqkv, nqkv, tbl)


def _ssd_kernel(x_ref, gc_ref, gr_ref, dsk_ref, y_ref, st_scr):
    d = pl.program_id(0)
    j = pl.program_id(1)
    bp = pl.program_id(2)
    rev = d == 1
    width = SSD_HEADS * HEAD_DIM
    gn = SSD_GROUPS * SSD_STATE
    per_group = SSD_HEADS // SSD_GROUPS
    nsub = TM // SUB
    mask = _order_mask(rev, SUB)
    head = _head_of_lane(width)
    skip = jnp.where(rev, 0.0, 1.0) * dsk_ref[...]

    @pl.when(j == 0)
    def _():
        for i in range(NBS):
            st_scr[bp * NBS + i] = jnp.zeros(st_scr.shape[1:], F32)

    def chain(i):
        bi = bp * NBS + i
        st = st_scr[bi]
        for step in range(nsub):
            ci = jnp.where(rev, nsub - 1 - step, step)
            rows = pl.ds(pl.multiple_of(ci * SUB, SUB), SUB)
            xs = x_ref[i, rows, 0:width]
            bm = x_ref[i, rows, width:width + gn].astype(MXU_DTYPE)
            cm = x_ref[i, rows, width + gn:width + 2 * gn].astype(MXU_DTYPE)
            gcol = gc_ref[i, rows, :]
            grow = gr_ref[i, ci]
            xsb = xs.astype(MXU_DTYPE)

            e_cols, w_cols, last_cols, dmats = [], [], [], []
            for h in range(SSD_HEADS):
                acs_col = gcol[:, G_ACS + h:G_ACS + h + 1]
                acs_row = grow[G_ACS + h:G_ACS + h + 1, :]
                a_last = jnp.where(rev, acs_col[0:1, :], acs_col[SUB - 1:SUB, :])
                e_cols.append(jnp.exp(acs_col))
                w_cols.append(jnp.exp(a_last - acs_col) * gcol[:, G_DT + h:G_DT + h + 1])
                last_cols.append(jnp.exp(a_last))
                dmats.append(jnp.exp(jnp.where(mask, acs_col - acs_row, NEG)) * grow[G_DT + h:G_DT + h + 1, :])
                if h % 2 == 1:
                    yield

            e_full = _expand_heads(e_cols, width)
            xw = xs * _expand_heads(w_cols, width)
            last_full = _expand_heads(last_cols, width)
            yield

            y_parts, st_parts = [], []
            for g in range(SSD_GROUPS):
                b_g = bm[:, g * SSD_STATE:(g + 1) * SSD_STATE]
                c_g = cm[:, g * SSD_STATE:(g + 1) * SSD_STATE]
                lanes = slice(g * per_group * HEAD_DIM, (g + 1) * per_group * HEAD_DIM)
                cb = _mm_nt(c_g, b_g)
                y_g = _mm(c_g, st[:, lanes]) * e_full[:, lanes]
                head_g = head[:, lanes]
                for hh in range(per_group):
                    h = g * per_group + hh
                    y_g = y_g + jnp.where(head_g == h, _mm(cb * dmats[h], xsb[:, lanes]), 0.0)
                    yield
                st_parts.append(st[:, lanes] * last_full[:, lanes] + _mm_tn(b_g, xw[:, lanes]))
                y_parts.append(y_g)
            st = jnp.concatenate(st_parts, axis=1)
            y_ref[i, rows, :] = jnp.concatenate(y_parts, axis=1) + skip * xs
            yield
        st_scr[bi] = st

    _round_robin([chain(i) for i in range(NBS)])


def _ssd(xbc, gc, gr, dskip):
    b, s, cw = xbc.shape
    nt = s // TM
    width = SSD_HEADS * HEAD_DIM
    return pl.pallas_call(
        _ssd_kernel,
        grid=(2, nt, b // NBS),
        in_specs=_scan_specs(nt, (cw,), ("col", "row")) + [_const_spec(dskip, 3)],
        out_specs=pl.BlockSpec((None, NBS, TM, width), lambda d, j, bp: (d, bp, _tile_of(d, j, nt), 0)),
        out_shape=jax.ShapeDtypeStruct((2, b, s, width), F32),
        scratch_shapes=[pltpu.VMEM((b, SSD_STATE, width), F32)],
        compiler_params=pltpu.CompilerParams(vmem_limit_bytes=VMEM_LIMIT),
        name="ssd",
    )(xbc, gc, gr, dskip)


def _merge_kernel(x_ref, mod_ref, nw_ref, hm_ref, mo_ref, mnw_ref, ys_ref, su_ref, s5d_ref, glu_ref, na_ref,
                  yd_ref, dz_ref, dnw_ref, wg_ref, wa_ref, wb_ref, wc_ref, wd_ref, wo_ref, o_ref):
    d_model = x_ref.shape[2]
    h = _stack(lambda i: _norm_mod(x_ref[i], nw_ref[...], mod_ref[i, 1:2, :], mod_ref[i, 0:1, :]))
    h = h.astype(MXU_DTYPE)

    wm = MLSTM_HEADS * HEAD_DIM
    hm = _stack(lambda i: (hm_ref[0, i] + hm_ref[1, i]) * _sigmoid(mo_ref[i]))
    ms = _mm_exact_rhs(hm * hm, _same_head(wm, wm)) * (1.0 / HEAD_DIM)
    ya = hm * lax.rsqrt(ms + EPS) * mnw_ref[...]

    ws = S5_GROUPS * S5_GROUP
    ys = _stack(lambda i: ys_ref[i] + s5d_ref[...] * su_ref[i])
    ys = 0.5 * ys * (1.0 + jnp.tanh(math.sqrt(2.0 / math.pi) * (ys + 0.044715 * (ys * ys * ys))))
    ab = _mm(ys, glu_ref[...])
    yb = ab[:, :ws] * _sigmoid(ab[:, ws:])

    yc = _stack(lambda i: na_ref[i])

    yd = _stack(lambda i: (yd_ref[0, i] + yd_ref[1, i]) * _silu(dz_ref[i]))
    yd = yd * lax.rsqrt(jnp.mean(yd * yd, axis=-1, keepdims=True) + EPS) * dnw_ref[...]

    m = None
    for k, (y, w_ref) in enumerate(((ya, wa_ref), (yb, wb_ref), (yc, wc_ref), (yd, wd_ref))):
        gate = _sigmoid(jnp.dot(h, wg_ref[:, k * d_model:(k + 1) * d_model], preferred_element_type=F32))
        term = gate * _mm(y, w_ref[...])
        m = term if m is None else m + term
    out = _mm(m, wo_ref[...])
    for i in range(NB):
        o_ref[i] = x_ref[i] + mod_ref[i, 2:3, :] * out[i * TM:(i + 1) * TM, :]


def _merge(seq, mod, norm_w, hm, mo, mnw, ys, su, s5d, glu_w, yna, yd, dz, dnw, wg, wa, wb, wc, wd, wo,
           with_ctx):
    b, s, d = seq.shape
    nt = s // TM
    first = 0 if with_ctx else 1

    def tok(width):
        return pl.BlockSpec((NB, TM, width), lambda bp, j: (bp, first + j, 0))

    def tok2(width):
        return pl.BlockSpec((2, NB, TM, width), lambda bp, j: (0, bp, first + j, 0))

    def const(arr):
        return _const_spec(arr, 2, single=True)

    ws = S5_GROUPS * S5_GROUP
    return pl.pallas_call(
        _merge_kernel,
        grid=(b // NB, nt - first),
        in_specs=[tok(d), _mod_spec(d, first), const(norm_w),
                  tok2(MLSTM_HEADS * HEAD_DIM), tok(W_O), const(mnw),
                  tok(ws), tok(ws),
                  const(s5d), const(glu_w),
                  tok(NA_HEADS * HEAD_DIM),
                  tok2(SSD_HEADS * HEAD_DIM), tok(W_DZ), const(dnw),
                  const(wg), const(wa), const(wb), const(wc), const(wd), const(wo)],
        out_specs=tok(d),
        out_shape=jax.ShapeDtypeStruct((b, s, d), F32),
        input_output_aliases={0: 0},
        compiler_params=pltpu.CompilerParams(vmem_limit_bytes=VMEM_LIMIT),
        name="merge",
    )(seq, mod, norm_w, hm, mo, mnw, ys, su, s5d, glu_w, yna, yd, dz, dnw, wg, wa, wb, wc, wd, wo)


def _ffn_kernel(x_ref, mod_ref, nw_ref, wi_ref, wo_ref, fw_ref, o_ref, *, final):
    hidden = wo_ref.shape[0]
    h = _stack(lambda i: _norm_mod(x_ref[i], nw_ref[...], mod_ref[i, 4:5, :], mod_ref[i, 3:4, :]))
    h = h.astype(MXU_DTYPE)
    a = jnp.dot(h, wi_ref[:, :hidden], preferred_element_type=F32)
    g = jnp.dot(h, wi_ref[:, hidden:], preferred_element_type=F32)
    out = _mm(_silu(a) * g, wo_ref[...])
    for i in range(NB):
        y = x_ref[i] + mod_ref[i, 5:6, :] * out[i * TM:(i + 1) * TM, :]
        if final:
            y = y * lax.rsqrt(jnp.mean(y * y, axis=-1, keepdims=True) + EPS) * fw_ref[...]
        o_ref[i] = y


def _ffn(seq, mod, norm_w, wi, wo, final_w, final):
    b, s, d = seq.shape
    nt = s // TM
    first = 1 if final else 0

    def const(arr):
        return _const_spec(arr, 2, single=True)

    tok_in = pl.BlockSpec((NB, TM, d), lambda bp, j: (bp, first + j, 0))
    if final:
        out_spec = pl.BlockSpec((NB, TM, d), lambda bp, j: (bp, j, 0))
        out_shape = jax.ShapeDtypeStruct((b, s - TM, d), F32)
        aliases = {}
    else:
        out_spec = tok_in
        out_shape = jax.ShapeDtypeStruct((b, s, d), F32)
        aliases = {0: 0}
    return pl.pallas_call(
        functools.partial(_ffn_kernel, final=final),
        grid=(b // NB, nt - first),
        in_specs=[tok_in, _mod_spec(d, first), const(norm_w), const(wi), const(wo), const(final_w)],
        out_specs=out_spec,
        out_shape=out_shape,
        input_output_aliases=aliases,
        compiler_params=pltpu.CompilerParams(vmem_limit_bytes=VMEM_LIMIT),
        name="ffn_final" if final else "ffn",
    )(seq, mod, norm_w, wi, wo, final_w)


def _rope_tables(t, width):
    nf = HEAD_DIM // 4
    inv = ROPE_THETA ** (-jnp.arange(nf, dtype=F32) / nf)
    tok = jnp.arange(t)
    ang_r = (tok // GRID_W).astype(F32)[:, None] * inv
    ang_c = (tok % GRID_W).astype(F32)[:, None] * inv
    cos_h = jnp.concatenate([jnp.cos(ang_r)] * 2 + [jnp.cos(ang_c)] * 2, axis=1)
    sin_h = jnp.concatenate([-jnp.sin(ang_r), jnp.sin(ang_r), -jnp.sin(ang_c), jnp.sin(ang_c)], axis=1)
    cos_x = jnp.tile(cos_h, (1, width // HEAD_DIM))
    sin_x = jnp.tile(sin_h, (1, width // HEAD_DIM))
    cos_t = jnp.concatenate([jnp.ones((TM, width), F32), cos_x], axis=0)
    sin_t = jnp.concatenate([jnp.zeros((TM, width), F32), sin_x], axis=0)
    scale = HEAD_DIM ** -0.5
    return (jnp.concatenate([cos_t * scale, cos_t], axis=1), jnp.concatenate([sin_t * scale, sin_t], axis=1))


def _na_bias_tables(rpb):
    col = jnp.arange(GRID_W)
    col0 = jnp.clip(col - NA_KW // 2, 0, GRID_W - NA_KW)
    in_win = (col[None, :] >= col0[:, None]) & (col[None, :] < col0[:, None] + NA_KW)
    dc = jnp.clip(col[None, :] - col[:, None], -(NA_KW - 1), NA_KW - 1) + (NA_KW - 1)
    per_row = jnp.where(in_win, rpb.astype(F32)[:, :, dc], NEG)
    bias = jnp.stack([per_row[:, off:off + NA_KH] for off in range(NA_KH)], axis=0)
    return bias.transpose(0, 1, 3, 2, 4).reshape(NA_KH, NA_HEADS * GRID_W, NA_KH * GRID_W)


def _s5_params(lam_re, lam_im, log_dt, b_re, b_im, c_re, c_im):
    n = S5_CHUNK
    g, p, ch = b_re.shape
    lam_re = lam_re.astype(F32)
    lam_im = lam_im.astype(F32)
    dt = jnp.exp(log_dt.astype(F32))[..., None]
    tau = jnp.arange(n + 1, dtype=F32)[:, None, None, None]
    pmag = jnp.exp(tau * lam_re * dt)
    pw_re = pmag * jnp.cos(tau * lam_im * dt)
    pw_im = pmag * jnp.sin(tau * lam_im * dt)
    a_re, a_im = pw_re[1], pw_im[1]
    den = lam_re * lam_re + lam_im * lam_im
    nr = a_re - 1.0
    coef_re = (nr * lam_re + a_im * lam_im) / den
    coef_im = (a_im * lam_re - nr * lam_im) / den
    bb_re = coef_re[..., None] * b_re.astype(F32) - coef_im[..., None] * b_im.astype(F32)
    bb_im = coef_re[..., None] * b_im.astype(F32) + coef_im[..., None] * b_re.astype(F32)
    cp_re = c_re.astype(F32) * pw_re[:, :, :, None, :] - c_im.astype(F32) * pw_im[:, :, :, None, :]
    cp_im = c_re.astype(F32) * pw_im[:, :, :, None, :] + c_im.astype(F32) * pw_re[:, :, :, None, :]
    taps = (jnp.einsum('tdgcp,dgpe->tdgce', cp_re[:n], bb_re) - jnp.einsum('tdgcp,dgpe->tdgce', cp_im[:n], bb_im))
    s_idx = jnp.arange(n)[:, None]
    t_idx = jnp.arange(n)[None, :]
    k_idx = jnp.arange(n)[None, None, :]
    oh_f = ((t_idx - s_idx)[..., None] == k_idx).astype(F32)
    oh_b = ((s_idx - t_idx)[..., None] == k_idx).astype(F32)
    toep = (jnp.einsum('stk,kgce->gsetc', oh_f, taps[:, 0]) + jnp.einsum('stk,kgce->gsetc', oh_b, taps[:, 1]))
    toep = toep.reshape(g, n * ch, n * ch)

    def times_bbar(w_re, w_im, d):
        re = w_re[..., None] * bb_re[d] - w_im[..., None] * bb_im[d]
        im = w_re[..., None] * bb_im[d] + w_im[..., None] * bb_re[d]
        return [a.transpose(1, 0, 3, 2).reshape(g, n * ch, p) for a in (re, im)]

    wend4 = jnp.stack(times_bbar(pw_re[:n, 0][::-1], pw_im[:n, 0][::-1], 0)
                      + times_bbar(pw_re[:n, 1], pw_im[:n, 1], 1))

    def readout(c_pw_re, c_pw_im):
        return [a.transpose(1, 3, 0, 2).reshape(g, p, n * ch) for a in (c_pw_re, -c_pw_im)]

    wout4 = jnp.stack(readout(cp_re[1:, 0], cp_im[1:, 0]) + readout(cp_re[1:, 1][::-1], cp_im[1:, 1][::-1]))
    al4 = jnp.stack([pw_re[n, 0], pw_im[n, 0], pw_re[n, 1], pw_im[n, 1]])

    eye2 = jnp.eye(2, dtype=F32)
    hp = g // 2
    toep_p = jnp.einsum('ij,qirc->qirjc', eye2, toep.reshape(hp, 2, n * ch, n * ch))
    wend_p = jnp.einsum('ij,kqirp->qirkjp', eye2, wend4.reshape(4, hp, 2, n * ch, p))
    wout_p = jnp.einsum('ij,kqipn->qkipjn', eye2, wout4.reshape(4, hp, 2, p, n * ch))
    return (toep_p.reshape(hp, 2 * n * ch, 2 * n * ch).astype(MXU_DTYPE),
            wend_p.reshape(hp, 2 * n * ch, 8 * p).astype(MXU_DTYPE),
            wout_p.reshape(hp, 8 * p, 2 * n * ch).astype(MXU_DTYPE),
            al4.reshape(4, hp, 2 * p).transpose(1, 0, 2))


def _lanes(vec, at, width=W_SMALL):
    return jnp.zeros((1, width), F32).at[0, at:at + vec.shape[0]].set(vec.astype(F32))


def _gate_perm():
    perm = jnp.zeros((2, 2 * W_SMALL, W_SMALL), F32)
    for d in range(2):
        for h in range(MLSTM_HEADS):
            perm = perm.at[d, L_MI + d * MLSTM_HEADS + h, G_LI + h].set(1.0)
            perm = perm.at[d, L_MF + d * MLSTM_HEADS + h, G_BCUM + h].set(1.0)
        for h in range(SSD_HEADS):
            perm = perm.at[d, L_DDT + d * SSD_HEADS + h, G_ACS + h].set(1.0)
            perm = perm.at[d, W_SMALL + L_DDT + d * SSD_HEADS + h, G_DT + h].set(1.0)
    return perm.astype(MXU_DTYPE)


def _log_decay_perms():
    pa = jnp.zeros((3, W_SMALL, W_SMALL), F32)
    pb = jnp.zeros((3, W_SMALL, W_SMALL), F32)
    ones_ab = jnp.zeros((2, W_SMALL), F32)
    slots = ([(G_BCUM + h, G_LI + h) for h in range(MLSTM_HEADS)]
             + [(G_ACS + h, G_ACS + h) for h in range(SSD_HEADS)])
    for n, (cum_lane, x_lane) in enumerate(slots):
        for k in range(3):
            pa = pa.at[k, cum_lane, 8 * n + k].set(1.0)
            pb = pb.at[k, x_lane, 8 * n + 3 + k].set(1.0)
            ones_ab = ones_ab.at[0, 8 * n + 3 + k].set(1.0)
            ones_ab = ones_ab.at[1, 8 * n + k].set(1.0)
    return pa.astype(MXU_DTYPE), pb.astype(MXU_DTYPE), ones_ab


def kernel(x, c, ctx, c_ctx, ada_w, ada_b, norm1_w, norm2_w, w_in, mlstm_conv_w, mlstm_conv_b, mlstm_ib, mlstm_fb, mlstm_norm_w, s5_lam_re, s5_lam_im, s5_log_dt, s5_b_re, s5_b_im, s5_c_re, s5_c_im, s5_d, s5_glu_w, na_rpb, ssd_conv_w, ssd_conv_b, ssd_a_log, ssd_dt_bias, ssd_d, ssd_norm_w, w_branch_a, w_branch_b, w_branch_c, w_branch_d, w_out, ffn_w_in, ffn_w_out, final_norm_w):
    b, t, d = x.shape
    depth = w_in.shape[0]
    assert ctx.shape[1] == TM and t % TM == 0 and t % GRID_W == 0 and b % 8 == 0 and b % NB == 0
    assert t // GRID_W >= NA_KH

    seq = jnp.concatenate([ctx, x], axis=1)

    pad = (-(b + 1)) % 8
    cc = jnp.concatenate([c, c_ctx[None, :], jnp.zeros((pad, d), F32)], axis=0)
    mod_all = _adaln(cc, ada_w, ada_b)
    mod_x = mod_all[:, :b].reshape(depth, b, 1, 6, d)
    mod_c = jnp.broadcast_to(mod_all[:, b].reshape(depth, 1, 1, 6, d), (depth, b, 1, 6, d))
    mod = jnp.concatenate([mod_c, mod_x], axis=2)

    cos_t, sin_t = _rope_tables(t, MLSTM_HEADS * HEAD_DIM)
    perm = _gate_perm()
    pa, pb, ones_ab = _log_decay_perms()
    s_len = seq.shape[1]

    for l in range(depth):
        with_ctx = l < depth - 1
        wl = w_in[l]
        w_small = jnp.concatenate([wl[:, 1024:1040], wl[:, 3600:3616], jnp.zeros((d, W_SMALL - 32), F32)], axis=1)
        w_proj = jnp.concatenate([wl[:, 0:512], wl[:, 2576:3600], wl[:, 512:768], wl[:, 768:1024], w_small,
                                  wl[:, 1040:1296], wl[:, 1296:2064], wl[:, 2064:2576]], axis=1).astype(MXU_DTYPE)
        w_gate = wl[:, 3616:].astype(MXU_DTYPE)

        cv, v, mo, small, su, nqkv, dz = _proj(seq, mod[l], norm1_w[l][None, :], w_proj)

        conv_w = jnp.concatenate([mlstm_conv_w[l], ssd_conv_w[l]], axis=1)
        conv_w = jnp.concatenate([conv_w, jnp.zeros((1, W_CV), F32)], axis=0)
        conv_b = jnp.concatenate([mlstm_conv_b[l], ssd_conv_b[l]])[None, :]
        gbias = (_lanes(mlstm_ib[l].reshape(-1), L_MI) + _lanes(mlstm_fb[l].reshape(-1), L_MF)
                 + _lanes(ssd_dt_bias[l].reshape(-1), L_DDT))
        qk, xbc, gc, gr, la, lb = _prep(cv, small, conv_w, conv_b, cos_t, sin_t, gbias,
                                        _lanes(ssd_a_log[l].reshape(-1), L_DDT), perm, pa, pb, ones_ab)

        hm = _mlstm(qk, v, gc, la, lb)

        n_chunks = s_len // S5_CHUNK
        half = S5_GROUPS // 2
        x_pairs = su.reshape(b, n_chunks, S5_CHUNK, half, 2, S5_GROUP).transpose(3, 1, 0, 4, 2, 5)
        x_pairs = x_pairs.reshape(half, n_chunks * b, 2 * S5_CHUNK * S5_GROUP).astype(MXU_DTYPE)
        ys = _s5(x_pairs, *_s5_params(s5_lam_re[l], s5_lam_im[l], s5_log_dt[l], s5_b_re[l], s5_b_im[l],
                                      s5_c_re[l], s5_c_im[l]), b)
        ys = ys.reshape(half, n_chunks, b, 2, S5_CHUNK, S5_GROUP).transpose(2, 1, 4, 0, 3, 5).reshape(b, s_len, W_SU)

        yna = _na(nqkv, _na_bias_tables(na_rpb[l]), with_ctx)

        yd = _ssd(xbc, gc, gr, jnp.repeat(ssd_d[l], HEAD_DIM)[None, :])

        seq = _merge(seq, mod[l], norm1_w[l][None, :], hm, mo, mlstm_norm_w[l][None, :], ys, su,
                     s5_d[l][None, :], s5_glu_w[l].astype(MXU_DTYPE), yna, yd, dz, ssd_norm_w[l][None, :],
                     w_gate, w_branch_a[l].astype(MXU_DTYPE), w_branch_b[l].astype(MXU_DTYPE),
                     w_branch_c[l].astype(MXU_DTYPE), w_branch_d[l].astype(MXU_DTYPE), w_out[l].astype(MXU_DTYPE),
                     with_ctx)
        seq = _ffn(seq, mod[l], norm2_w[l][None, :], ffn_w_in[l].astype(MXU_DTYPE), ffn_w_out[l].astype(MXU_DTYPE),
                   final_norm_w[None, :], not with_ctx)
    return seq
```

```python
import functools
import math

import jax
import jax.numpy as jnp
import numpy as np
from jax import lax
from jax.experimental import pallas as pl
from jax.experimental.pallas import tpu as pltpu

F32 = jnp.float32
MXU_DTYPE = jnp.bfloat16
HIGHEST = lax.Precision.HIGHEST

GRID_W = 64
EPS = 1e-6
CONV_K = 7
ROPE_THETA = 10000.0
HEAD_DIM = 64
MLSTM_HEADS = 4
S5_GROUPS = 16
S5_GROUP = 16
S5_STATE = 64
NA_HEADS = 4
NA_KH = 8
NA_KW = 16
SSD_HEADS = 8
SSD_GROUPS = 2
SSD_STATE = 128

TM = 256
NB = 2
NBS = 4
SUB = 128
HALO = 8
S5_SUB = 32
NEG = -1e30
VMEM_LIMIT = 56 * 1024 * 1024

W_QK, W_XBC, W_V, W_O, W_SMALL, W_SU, W_NQKV, W_DZ = 512, 1024, 256, 256, 128, 256, 768, 512
W_CV = W_QK + W_XBC
PROJ_WIDTHS = (W_CV, W_V, W_O, W_SMALL, W_SU, W_NQKV, W_DZ)
SU_POS = 4
L_MI, L_MF, L_DDT = 0, 8, 16
G_LI, G_BCUM, G_ACS, G_DT = 0, 4, 8, 16


def _mm(a, b):
    return jnp.dot(a.astype(MXU_DTYPE), b.astype(MXU_DTYPE), preferred_element_type=F32)


def _mm_nt(a, b):
    return lax.dot_general(a.astype(MXU_DTYPE), b.astype(MXU_DTYPE), (((1,), (1,)), ((), ())),
                           preferred_element_type=F32)


def _mm_tn(a, b):
    return lax.dot_general(a.astype(MXU_DTYPE), b.astype(MXU_DTYPE), (((0,), (0,)), ((), ())),
                           preferred_element_type=F32)


def _mm_f32(a, b):
    return jnp.dot(a, b, preferred_element_type=F32, precision=HIGHEST)


def _split3(x):
    hi = x.astype(MXU_DTYPE)
    r1 = x - hi.astype(F32)
    mid = r1.astype(MXU_DTYPE)
    lo = (r1 - mid.astype(F32)).astype(MXU_DTYPE)
    return hi, mid, lo


def _as_01(sel):
    if sel.dtype == jnp.bool_:
        sel = jnp.where(sel, 1.0, 0.0)
    return sel.astype(MXU_DTYPE)


def _mm_exact_rhs(x, sel):
    sel = _as_01(sel)
    return sum(jnp.dot(p, sel, preferred_element_type=F32) for p in _split3(x))


def _mm_exact_lhs(sel, x):
    sel = _as_01(sel)
    return sum(jnp.dot(sel, p, preferred_element_type=F32) for p in _split3(x))


def _sigmoid(x):
    return 1.0 / (1.0 + jnp.exp(-x))


def _silu(x):
    return x * _sigmoid(x)


def _softplus(x):
    return jnp.maximum(x, 0.0) + jnp.log(1.0 + jnp.exp(-jnp.abs(x)))


def _iota(shape, dim):
    return lax.broadcasted_iota(jnp.int32, shape, dim)


def _head_of_lane(width):
    return jnp.right_shift(_iota((1, width), 1), 6)


def _same_head(rows, cols):
    return jnp.right_shift(_iota((rows, cols), 0), 6) == jnp.right_shift(_iota((rows, cols), 1), 6)


def _expand_heads(cols, width):
    head = _head_of_lane(width)
    out = jnp.broadcast_to(cols[0], (cols[0].shape[0], width))
    for h in range(1, len(cols)):
        out = jnp.where(head == h, cols[h], out)
    return out


def _stack(fn):
    return jnp.concatenate([fn(i) for i in range(NB)], axis=0)


def _round_robin(chains):
    live = list(chains)
    while live:
        for chain in list(live):
            try:
                next(chain)
            except StopIteration:
                live.remove(chain)


def _tile_of(d, j, nt):
    return jnp.where(d == 0, j, jnp.where(j == 0, 0, nt - j))


def _norm_mod(x, w, scale, shift):
    y = x * lax.rsqrt(jnp.mean(x * x, axis=-1, keepdims=True) + EPS) * w
    return y * (1.0 + scale) + shift


def _order_mask(rev, n):
    diff = (_iota((n, n), 1) - _iota((n, n), 0)) * jnp.where(rev, -1, 1)
    return diff <= 0


def _const_spec(arr, n_grid, single=False):
    kwargs = {"pipeline_mode": pl.Buffered(1)} if single else {}
    return pl.BlockSpec(arr.shape, lambda *_: (0,) * arr.ndim, **kwargs)


def _mod_spec(d, first):
    return pl.BlockSpec((NB, None, 6, d), lambda bp, j: (bp, jnp.minimum(first + j, 1), 0, 0))


def _adaln_kernel(c_ref, w_ref, b_ref, o_ref):
    o_ref[...] = _mm_f32(_silu(c_ref[...]), w_ref[...]) + b_ref[...]


def _adaln(cc, ada_w, ada_b):
    depth, d, n = ada_w.shape
    tn = 768
    return pl.pallas_call(
        _adaln_kernel,
        grid=(depth, n // tn),
        in_specs=[pl.BlockSpec(cc.shape, lambda l, i: (0, 0)),
                  pl.BlockSpec((None, d, tn), lambda l, i: (l, 0, i)),
                  pl.BlockSpec((None, 1, tn), lambda l, i: (l, 0, i))],
        out_specs=pl.BlockSpec((None, cc.shape[0], tn), lambda l, i: (l, 0, i)),
        out_shape=jax.ShapeDtypeStruct((depth, cc.shape[0], n), F32),
        name="adaln",
    )(cc, ada_w, ada_b.reshape(depth, 1, n))


def _proj_kernel(x_ref, mod_ref, nw_ref, w_ref, *out_refs):
    h = _stack(lambda i: _norm_mod(x_ref[i], nw_ref[...], mod_ref[i, 1:2, :], mod_ref[i, 0:1, :]))
    h = h.astype(MXU_DTYPE)
    off = 0
    for pos, (ref, n) in enumerate(zip(out_refs, PROJ_WIDTHS)):
        res = jnp.dot(h, w_ref[:, off:off + n], preferred_element_type=F32)
        if pos == SU_POS:
            ref[...] = jnp.concatenate([res[i * TM:(i + 1) * TM, :] for i in range(NB)], axis=1)
        else:
            for i in range(NB):
                ref[i] = res[i * TM:(i + 1) * TM, :]
        off += n


def _proj(seq, mod, norm_w, w_proj):
    b, s, d = seq.shape
    nt = s // TM

    def tok(width):
        return pl.BlockSpec((NB, TM, width), lambda bp, j: (bp, j, 0))

    out_specs = [tok(w) for w in PROJ_WIDTHS]
    out_shape = [jax.ShapeDtypeStruct((b, s, w), F32) for w in PROJ_WIDTHS]
    out_specs[SU_POS] = pl.BlockSpec((TM, NB * W_SU), lambda bp, j: (j, bp))
    out_shape[SU_POS] = jax.ShapeDtypeStruct((s, b * W_SU), F32)
    return pl.pallas_call(
        _proj_kernel,
        grid=(b // NB, nt),
        in_specs=[tok(d), _mod_spec(d, 0), _const_spec(norm_w, 2), _const_spec(w_proj, 2, single=True)],
        out_specs=out_specs,
        out_shape=out_shape,
        compiler_params=pltpu.CompilerParams(vmem_limit_bytes=VMEM_LIMIT),
        name="proj",
    )(seq, mod, norm_w, w_proj)


def _prep_kernel(cv_ref, cvp_ref, cvn_ref, cw_ref, cb_ref, cos_ref, sin_ref, sm_ref, gb_ref, al_ref, perm_ref,
                 pa_ref, pb_ref, one_ref, qk_ref, xbc_ref, gc_ref, gr_ref, la_ref, lb_ref, ext, *, nt):
    t = pl.program_id(1)
    prev_ok = t > 1
    next_ok = (t >= 1) & (t < nt - 1)
    lane = _iota((1, W_SMALL), 1)
    first = jnp.bitwise_and(_iota((1, W_QK), 1), 31) < 16
    r = _iota((TM, TM), 0)
    c = _iota((TM, TM), 1)
    same_chunk = jnp.right_shift(r, 7) == jnp.right_shift(c, 7)
    cum = (lane >= G_BCUM) & (lane < G_DT)

    for i in range(NB):
        ext[i, 0:HALO, :] = jnp.where(prev_ok, cvp_ref[i], 0.0)
        ext[i, HALO:HALO + TM, :] = cv_ref[i]
        ext[i, HALO + TM:, :] = jnp.where(next_ok, cvn_ref[i], 0.0)
        acc = cb_ref[...] + cw_ref[0:1, :] * ext[i, pl.ds(HALO - CONV_K // 2, TM), :]
        for k in range(1, CONV_K):
            acc = acc + cw_ref[k:k + 1, :] * ext[i, pl.ds(HALO - CONV_K // 2 + k, TM), :]
        acc = _silu(acc)
        xbc_ref[i] = acc[:, W_QK:]
        qk = acc[:, :W_QK]
        partner = jnp.where(first, pltpu.roll(qk, W_QK - 16, 1), pltpu.roll(qk, 16, 1))
        qk_ref[i] = qk * cos_ref[...] + partner * sin_ref[...]

        g_all = sm_ref[i] + gb_ref[...]
        dt = _softplus(g_all)
        src = jnp.where(lane < L_MF, g_all,
                        jnp.where(lane < L_DDT, -_softplus(-g_all), dt * (-jnp.exp(al_ref[...]))))
        src = jnp.concatenate([src, dt], axis=1)
        for d in range(2):
            tri = same_chunk & ((c <= r) if d == 0 else (c >= r))
            g = _mm_exact_rhs(src, perm_ref[d])
            g = jnp.where(cum, _mm_exact_lhs(tri, jnp.where(cum, g, 0.0)), g)
            gc_ref[d, i] = g
            a_mat = one_ref[0:1, :] + sum(jnp.dot(p, pa_ref[k], preferred_element_type=F32)
                                           for k, p in enumerate(_split3(g)))
            cv = jnp.where(lane < MLSTM_HEADS, g - pltpu.roll(g, W_SMALL - G_BCUM, 1), -g)
            b_mat = one_ref[1:2, :] + sum(jnp.dot(p, pb_ref[k], preferred_element_type=F32)
                                           for k, p in enumerate(_split3(cv)))
            la_ref[d, i] = a_mat.astype(MXU_DTYPE)
            for ci in range(TM // SUB):
                gr_ref[d, i, ci] = g[ci * SUB:(ci + 1) * SUB, :].T
                lb_ref[d, i, ci] = b_mat[ci * SUB:(ci + 1) * SUB, :].T.astype(MXU_DTYPE)


def _prep(cv, small, conv_w, conv_b, cos_t, sin_t, gbias, alog, perm, pa, pb, ones_ab):
    b, s, _ = cv.shape
    nt = s // TM
    per = TM // HALO
    last = s // HALO - 1
    nsub = TM // SUB
    return pl.pallas_call(
        functools.partial(_prep_kernel, nt=nt),
        grid=(b // NB, nt),
        in_specs=[pl.BlockSpec((NB, TM, W_CV), lambda bp, j: (bp, j, 0)),
                  pl.BlockSpec((NB, HALO, W_CV), lambda bp, j: (bp, jnp.maximum(j * per - 1, 0), 0)),
                  pl.BlockSpec((NB, HALO, W_CV), lambda bp, j: (bp, jnp.minimum((j + 1) * per, last), 0)),
                  _const_spec(conv_w, 2), _const_spec(conv_b, 2),
                  pl.BlockSpec((TM, W_QK), lambda bp, j: (j, 0)),
                  pl.BlockSpec((TM, W_QK), lambda bp, j: (j, 0)),
                  pl.BlockSpec((NB, TM, W_SMALL), lambda bp, j: (bp, j, 0)),
                  _const_spec(gbias, 2), _const_spec(alog, 2), _const_spec(perm, 2),
                  _const_spec(pa, 2), _const_spec(pb, 2), _const_spec(ones_ab, 2)],
        out_specs=[pl.BlockSpec((NB, TM, W_QK), lambda bp, j: (bp, j, 0)),
                   pl.BlockSpec((NB, TM, W_XBC), lambda bp, j: (bp, j, 0)),
                   pl.BlockSpec((2, NB, TM, W_SMALL), lambda bp, j: (0, bp, j, 0)),
                   pl.BlockSpec((2, NB, nsub, W_SMALL, SUB), lambda bp, j: (0, bp, j, 0, 0)),
                   pl.BlockSpec((2, NB, TM, W_SMALL), lambda bp, j: (0, bp, j, 0)),
                   pl.BlockSpec((2, NB, nsub, W_SMALL, SUB), lambda bp, j: (0, bp, j, 0, 0))],
        out_shape=[jax.ShapeDtypeStruct((b, s, W_QK), F32),
                   jax.ShapeDtypeStruct((b, s, W_XBC), F32),
                   jax.ShapeDtypeStruct((2, b, s, W_SMALL), F32),
                   jax.ShapeDtypeStruct((2, b, s // SUB, W_SMALL, SUB), F32),
                   jax.ShapeDtypeStruct((2, b, s, W_SMALL), MXU_DTYPE),
                   jax.ShapeDtypeStruct((2, b, s // SUB, W_SMALL, SUB), MXU_DTYPE)],
        scratch_shapes=[pltpu.VMEM((NB, TM + 2 * HALO, W_CV), F32)],
        compiler_params=pltpu.CompilerParams(vmem_limit_bytes=VMEM_LIMIT),
        name="prep",
    )(cv, cv, cv, conv_w, conv_b, cos_t, sin_t, small, gbias, alog, perm, pa, pb, ones_ab)


def _scan_specs(nt, widths, kinds):
    nsub = TM // SUB
    specs = [pl.BlockSpec((NBS, TM, w), lambda d, j, bp: (bp, _tile_of(d, j, nt), 0)) for w in widths]
    for kind in kinds:
        if kind == "col":
            specs.append(pl.BlockSpec((None, NBS, TM, W_SMALL), lambda d, j, bp: (d, bp, _tile_of(d, j, nt), 0)))
        else:
            specs.append(pl.BlockSpec((None, NBS, nsub, W_SMALL, SUB),
                                      lambda d, j, bp: (d, bp, _tile_of(d, j, nt), 0, 0)))
    return specs


def _mlstm_kernel(qk_ref, v_ref, gc_ref, la_ref, lb_ref, h_ref, c_scr, n_scr, m_scr):
    d = pl.program_id(0)
    j = pl.program_id(1)
    bp = pl.program_id(2)
    rev = d == 1
    width = MLSTM_HEADS * HEAD_DIM
    nsub = TM // SUB
    heads = list(range(MLSTM_HEADS))
    stacked = (MLSTM_HEADS * SUB, SUB)
    t_minus_s = (jnp.bitwise_and(_iota(stacked, 0), SUB - 1) - _iota(stacked, 1)) * jnp.where(rev, -1, 1)
    mask_s = t_minus_s >= 0
    head = _head_of_lane(width)
    same_head = _same_head(width, width)
    lane = _iota((1, W_SMALL), 1)
    valid = lane < MLSTM_HEADS
    group = jnp.right_shift(lane, 3)
    to_heads = _as_01(jnp.right_shift(_iota((W_SMALL, width), 1), 6) == _iota((W_SMALL, width), 0))
    from_heads = _as_01(jnp.right_shift(_iota((width, W_SMALL), 0), 6) == _iota((width, W_SMALL), 1))
    ones_sq = jnp.ones((SUB, W_SMALL), MXU_DTYPE)
    ones_rows = jnp.ones((8, SUB), MXU_DTYPE)

    @pl.when(j == 0)
    def _():
        for i in range(NBS):
            c_scr[bp * NBS + i] = jnp.zeros(c_scr.shape[1:], F32)
            n_scr[bp * NBS + i] = jnp.zeros(n_scr.shape[1:], F32)
            m_scr[bp * NBS + i] = jnp.zeros(m_scr.shape[1:], F32)

    def chain(i):
        bi = bp * NBS + i
        c_st = c_scr[bi]
        n_st = n_scr[bi, 0:1, :]
        m_st = m_scr[bi, 0:1, :]
        for step in range(nsub):
            ci = jnp.where(rev, nsub - 1 - step, step)
            rows = pl.ds(pl.multiple_of(ci * SUB, SUB), SUB)
            q = qk_ref[i, rows, 0:width]
            k = qk_ref[i, rows, width:2 * width]
            qb = q.astype(MXU_DTYPE)
            kb = k.astype(MXU_DTYPE)
            vb = v_ref[i, rows, :].astype(MXU_DTYPE)
            li = gc_ref[i, rows, :]
            bc = pltpu.roll(li, W_SMALL - G_BCUM, 1)
            la = la_ref[i, rows, :]
            lb = lb_ref[i, ci]

            la_s = jnp.concatenate([jnp.where(group == h, la, 0) for h in heads], axis=0)
            q_s = jnp.concatenate([jnp.where(head == h, qb, 0) for h in heads], axis=0)
            log_w = jnp.where(mask_s, jnp.dot(la_s, lb, preferred_element_type=F32), NEG)
            yield
            inter = bc + m_st
            inter_s = jnp.concatenate([inter[:, h:h + 1] for h in heads], axis=0)
            m_t_s = jnp.maximum(inter_s, jnp.max(log_w, axis=1, keepdims=True))
            sm = (_mm_nt(q_s, kb) * jnp.exp(log_w - m_t_s)).astype(MXU_DTYPE)
            yield
            pv = jnp.dot(sm, vb, preferred_element_type=F32)
            rs = jnp.dot(sm, ones_sq, preferred_element_type=F32)
            yield
            num = pv[0:SUB, :]
            den = rs[0:SUB, :]
            m_t = jnp.broadcast_to(m_t_s[0:SUB, :], (SUB, W_SMALL))
            for h in heads[1:]:
                blk = slice(h * SUB, (h + 1) * SUB)
                num = jnp.where(head == h, pv[blk, :], num)
                den = jnp.where(lane == h, rs[blk, :], den)
                m_t = jnp.where(lane == h, m_t_s[blk, :], m_t)
            g = jnp.exp(inter - m_t)
            den = den + g * jnp.dot((q * n_st).astype(MXU_DTYPE), from_heads, preferred_element_type=F32)
            inv = 1.0 / jnp.maximum(jnp.abs(den), jnp.exp(-m_t))
            yield
            inv_full = _mm_exact_rhs(jnp.where(valid, inv, 0.0), to_heads)
            ig_full = _mm_exact_rhs(jnp.where(valid, inv * g, 0.0), to_heads)
            h_ref[i, rows, :] = num * inv_full + _mm(qb, c_st) * ig_full
            yield

            b_last = jnp.where(rev, bc[0:1, :], bc[SUB - 1:SUB, :])
            log_k = b_last - bc + li
            m_new = jnp.maximum(b_last + m_st, jnp.max(log_k, axis=0, keepdims=True))
            wk_full = _mm_exact_rhs(jnp.where(valid, jnp.exp(log_k - m_new), 0.0), to_heads)
            decay = jnp.broadcast_to(jnp.where(valid, jnp.exp(b_last + m_st - m_new), 0.0), (8, W_SMALL))
            decay_full = _mm_exact_rhs(decay, to_heads)[0:1, :]
            kw = (k * wk_full).astype(MXU_DTYPE)
            yield
            c_st = c_st * decay_full + jnp.where(same_head, _mm_tn(kw, vb), 0.0)
            n_st = n_st * decay_full + jnp.dot(ones_rows, kw, preferred_element_type=F32)[0:1, :]
            m_st = jnp.where(valid, m_new, 0.0)
            yield

        c_scr[bi] = c_st
        n_scr[bi, 0:1, :] = n_st
        m_scr[bi, 0:1, :] = m_st

    _round_robin([chain(i) for i in range(NBS)])


def _mlstm(qk, v, gc, la, lb):
    b, s, _ = qk.shape
    nt = s // TM
    width = MLSTM_HEADS * HEAD_DIM
    return pl.pallas_call(
        _mlstm_kernel,
        grid=(2, nt, b // NBS),
        in_specs=_scan_specs(nt, (2 * width, width), ("col", "col", "row")),
        out_specs=pl.BlockSpec((None, NBS, TM, width), lambda d, j, bp: (d, bp, _tile_of(d, j, nt), 0)),
        out_shape=jax.ShapeDtypeStruct((2, b, s, width), F32),
        scratch_shapes=[pltpu.VMEM((b, width, width), F32), pltpu.VMEM((b, 8, width), F32),
                        pltpu.VMEM((b, 8, W_SMALL), F32)],
        compiler_params=pltpu.CompilerParams(vmem_limit_bytes=VMEM_LIMIT),
        name="mlstm",
    )(qk, v, gc, la, lb)


def _s5_kernel(u_ref, lre_ref, lim_ref, ldt_ref, bre_ref, bim_ref, cw_ref, y_ref,
               wbu_scr, are_scr, aim_scr, hre_scr, him_scr, hbuf, *, nb):
    d = pl.program_id(0)
    j = pl.program_id(1)
    rev = d == 1
    n_state = S5_GROUPS * S5_STATE

    @pl.when(j == 0)
    def _():
        lre = lre_ref[...]
        lim = lim_ref[...]
        dt = jnp.exp(ldt_ref[...])
        mag = jnp.exp(lre * dt)
        a_re = mag * jnp.cos(lim * dt)
        a_im = mag * jnp.sin(lim * dt)
        den = lre * lre + lim * lim
        nr = a_re - 1.0
        coef_re = (nr * lre + a_im * lim) / den
        coef_im = (a_im * lre - nr * lim) / den
        wbu_scr[:, :n_state] = (coef_re * bre_ref[...] - coef_im * bim_ref[...]).astype(MXU_DTYPE)
        wbu_scr[:, n_state:] = (coef_re * bim_ref[...] + coef_im * bre_ref[...]).astype(MXU_DTYPE)
        are_scr[...] = jnp.broadcast_to(a_re, are_scr.shape)
        aim_scr[...] = jnp.broadcast_to(a_im, aim_scr.shape)
        hre_scr[...] = jnp.zeros(hre_scr.shape, F32)
        him_scr[...] = jnp.zeros(him_scr.shape, F32)

    a_re = are_scr[...]
    a_im = aim_scr[...]
    rows_sub = S5_SUB * nb
    n_sub = TM // S5_SUB
    carry = (hre_scr[...], him_scr[...])
    for sb in range(n_sub):
        sbi = jnp.where(rev, n_sub - 1 - sb, sb)
        rows = pl.ds(pl.multiple_of(sbi * rows_sub, rows_sub), rows_sub)
        hbuf[...] = jnp.dot(u_ref[rows, :].astype(MXU_DTYPE), wbu_scr[...], preferred_element_type=F32)

        def step(i, hc):
            ti = jnp.where(rev, S5_SUB - 1 - i, i)
            r = pl.ds(pl.multiple_of(ti * nb, nb), nb)
            h_re, h_im = hc
            n_re = a_re * h_re - a_im * h_im + hbuf[r, :n_state]
            n_im = a_re * h_im + a_im * h_re + hbuf[r, n_state:]
            hbuf[r, :n_state] = n_re
            hbuf[r, n_state:] = n_im
            return n_re, n_im

        carry = lax.fori_loop(0, S5_SUB, step, carry)
        y_ref[rows, :] = jnp.dot(hbuf[...].astype(MXU_DTYPE), cw_ref[...], preferred_element_type=F32)
    hre_scr[...] = carry[0]
    him_scr[...] = carry[1]


def _s5(u_tm, lam_re, lam_im, log_dt, braw_re, braw_im, cw, nb):
    rows, width = u_tm.shape
    s = rows // nb
    nt = s // TM
    n_state = S5_GROUPS * S5_STATE

    def per_dir():
        return pl.BlockSpec((None, 1, n_state), lambda d, j: (d, 0, 0))

    return pl.pallas_call(
        functools.partial(_s5_kernel, nb=nb),
        grid=(2, nt),
        in_specs=[pl.BlockSpec((TM * nb, width), lambda d, j: (_tile_of(d, j, nt), 0)),
                  per_dir(), per_dir(), per_dir(),
                  _const_spec(braw_re, 2), _const_spec(braw_im, 2), _const_spec(cw, 2)],
        out_specs=pl.BlockSpec((None, TM * nb, width), lambda d, j: (d, _tile_of(d, j, nt), 0)),
        out_shape=jax.ShapeDtypeStruct((2, rows, width), F32),
        scratch_shapes=[pltpu.VMEM((width, 2 * n_state), MXU_DTYPE),
                        pltpu.VMEM((nb, n_state), F32), pltpu.VMEM((nb, n_state), F32),
                        pltpu.VMEM((nb, n_state), F32), pltpu.VMEM((nb, n_state), F32),
                        pltpu.VMEM((S5_SUB * nb, 2 * n_state), F32)],
        compiler_params=pltpu.CompilerParams(vmem_limit_bytes=VMEM_LIMIT),
        name="s5",
    )(u_tm, lam_re, lam_im, log_dt, braw_re, braw_im, cw)


def _na_kernel(q_ref, k_ref, v_ref, tbl_ref, o_ref, *, first_tile, n_rows):
    t = first_tile + pl.program_id(1)
    width = NA_HEADS * HEAD_DIM
    head = _head_of_lane(width)
    scale = HEAD_DIM ** -0.5
    k_ctx = k_ref[0:TM, :].astype(MXU_DTYPE)
    v_ctx = v_ref[0:TM, :].astype(MXU_DTYPE)

    def stack_heads(q):
        return jnp.concatenate([jnp.where(head == h, q, 0.0) for h in range(NA_HEADS)], axis=0).astype(MXU_DTYPE)

    def unstack_heads(o, n):
        acc = o[0:n, :]
        for h in range(1, NA_HEADS):
            acc = jnp.where(head == h, o[h * n:(h + 1) * n, :], acc)
        return acc

    @pl.when(t == 0)
    def _():
        s = _mm_nt(stack_heads(q_ref[...] * scale), k_ctx)
        p = jnp.exp(s - jnp.max(s, axis=1, keepdims=True))
        o = _mm(p, v_ctx) / jnp.sum(p, axis=1, keepdims=True)
        o_ref[...] = unstack_heads(o, TM)

    @pl.when(t > 0)
    def _():
        rows_per_tile = TM // GRID_W
        n_lat = NA_KH * GRID_W
        for rr in range(rows_per_tile):
            r = (t - 1) * rows_per_tile + rr
            row_start = jnp.clip(r - NA_KH // 2, 0, n_rows - NA_KH)
            off = (NA_KH - 1) - (r - row_start)
            win = pl.ds(pl.multiple_of(TM + row_start * GRID_W, GRID_W), n_lat)
            qs = stack_heads(q_ref[rr * GRID_W:(rr + 1) * GRID_W, :] * scale)
            s_lat = _mm_nt(qs, k_ref[win, :]) + tbl_ref[off]
            s_ctx = _mm_nt(qs, k_ctx)
            m = jnp.maximum(jnp.max(s_lat, axis=1, keepdims=True), jnp.max(s_ctx, axis=1, keepdims=True))
            p_lat = jnp.exp(s_lat - m)
            p_ctx = jnp.exp(s_ctx - m)
            den = jnp.sum(p_lat, axis=1, keepdims=True) + jnp.sum(p_ctx, axis=1, keepdims=True)
            o = (_mm(p_lat, v_ref[win, :]) + _mm(p_ctx, v_ctx)) / den
            o_ref[rr * GRID_W:(rr + 1) * GRID_W, :] = unstack_heads(o, GRID_W)


def _na(nqkv, tbl, with_ctx):
    b, s, _ = nqkv.shape
    nt = s // TM
    width = NA_HEADS * HEAD_DIM
    first_tile = 0 if with_ctx else 1
    n_rows = (s - TM) // GRID_W
    return pl.pallas_call(
        functools.partial(_na_kernel, first_tile=first_tile, n_rows=n_rows),
        grid=(b, nt - first_tile),
        in_specs=[pl.BlockSpec((None, TM, width), lambda bi, j: (bi, first_tile + j, 0)),
                  pl.BlockSpec((None, s, width), lambda bi, j: (bi, 0, 1)),
                  pl.BlockSpec((None, s, width), lambda bi, j: (bi, 0, 2)),
                  _const_spec(tbl, 2)],
        out_specs=pl.BlockSpec((None, TM, width), lambda bi, j: (bi, first_tile + j, 0)),
        out_shape=jax.ShapeDtypeStruct((b, s, width), F32),
        compiler_params=pltpu.CompilerParams(vmem_limit_bytes=VMEM_LIMIT),
        name="na",
    )(nqkv, nqkv, nqkv, tbl)


def _ssd_kernel(x_ref, gc_ref, gr_ref, dsk_ref, y_ref, st_scr):
    d = pl.program_id(0)
    j = pl.program_id(1)
    bp = pl.program_id(2)
    rev = d == 1
    width = SSD_HEADS * HEAD_DIM
    gn = SSD_GROUPS * SSD_STATE
    per_group = SSD_HEADS // SSD_GROUPS
    nsub = TM // SUB
    mask = _order_mask(rev, SUB)
    head = _head_of_lane(width)
    skip = jnp.where(rev, 0.0, 1.0) * dsk_ref[...]

    @pl.when(j == 0)
    def _():
        for i in range(NBS):
            st_scr[bp * NBS + i] = jnp.zeros(st_scr.shape[1:], F32)

    def chain(i):
        bi = bp * NBS + i
        st = st_scr[bi]
        for step in range(nsub):
            ci = jnp.where(rev, nsub - 1 - step, step)
            rows = pl.ds(pl.multiple_of(ci * SUB, SUB), SUB)
            xs = x_ref[i, rows, 0:width]
            bm = x_ref[i, rows, width:width + gn].astype(MXU_DTYPE)
            cm = x_ref[i, rows, width + gn:width + 2 * gn].astype(MXU_DTYPE)
            gcol = gc_ref[i, rows, :]
            grow = gr_ref[i, ci]
            xsb = xs.astype(MXU_DTYPE)

            e_cols, w_cols, last_cols, dmats = [], [], [], []
            for h in range(SSD_HEADS):
                acs_col = gcol[:, G_ACS + h:G_ACS + h + 1]
                acs_row = grow[G_ACS + h:G_ACS + h + 1, :]
                a_last = jnp.where(rev, acs_col[0:1, :], acs_col[SUB - 1:SUB, :])
                e_cols.append(jnp.exp(acs_col))
                w_cols.append(jnp.exp(a_last - acs_col) * gcol[:, G_DT + h:G_DT + h + 1])
                last_cols.append(jnp.exp(a_last))
                dmats.append(jnp.exp(jnp.where(mask, acs_col - acs_row, NEG)) * grow[G_DT + h:G_DT + h + 1, :])
                if h % 2 == 1:
                    yield

            e_full = _expand_heads(e_cols, width)
            xw = xs * _expand_heads(w_cols, width)
            last_full = _expand_heads(last_cols, width)
            yield

            y_parts, st_parts = [], []
            for g in range(SSD_GROUPS):
                b_g = bm[:, g * SSD_STATE:(g + 1) * SSD_STATE]
                c_g = cm[:, g * SSD_STATE:(g + 1) * SSD_STATE]
                lanes = slice(g * per_group * HEAD_DIM, (g + 1) * per_group * HEAD_DIM)
                cb = _mm_nt(c_g, b_g)
                y_g = _mm(c_g, st[:, lanes]) * e_full[:, lanes]
                head_g = head[:, lanes]
                for hh in range(per_group):
                    h = g * per_group + hh
                    y_g = y_g + jnp.where(head_g == h, _mm(cb * dmats[h], xsb[:, lanes]), 0.0)
                    yield
                st_parts.append(st[:, lanes] * last_full[:, lanes] + _mm_tn(b_g, xw[:, lanes]))
                y_parts.append(y_g)
            st = jnp.concatenate(st_parts, axis=1)
            y_ref[i, rows, :] = jnp.concatenate(y_parts, axis=1) + skip * xs
            yield
        st_scr[bi] = st

    _round_robin([chain(i) for i in range(NBS)])


def _ssd(xbc, gc, gr, dskip):
    b, s, cw = xbc.shape
    nt = s // TM
    width = SSD_HEADS * HEAD_DIM
    return pl.pallas_call(
        _ssd_kernel,
        grid=(2, nt, b // NBS),
        in_specs=_scan_specs(nt, (cw,), ("col", "row")) + [_const_spec(dskip, 3)],
        out_specs=pl.BlockSpec((None, NBS, TM, width), lambda d, j, bp: (d, bp, _tile_of(d, j, nt), 0)),
        out_shape=jax.ShapeDtypeStruct((2, b, s, width), F32),
        scratch_shapes=[pltpu.VMEM((b, SSD_STATE, width), F32)],
        compiler_params=pltpu.CompilerParams(vmem_limit_bytes=VMEM_LIMIT),
        name="ssd",
    )(xbc, gc, gr, dskip)


def _merge_kernel(x_ref, mod_ref, nw_ref, hm_ref, mo_ref, mnw_ref, ys_ref, su_ref, s5d_ref, glu_ref, na_ref,
                  yd_ref, dz_ref, dnw_ref, wg_ref, wa_ref, wb_ref, wc_ref, wd_ref, wo_ref, o_ref):
    d_model = x_ref.shape[2]
    h = _stack(lambda i: _norm_mod(x_ref[i], nw_ref[...], mod_ref[i, 1:2, :], mod_ref[i, 0:1, :]))
    h = h.astype(MXU_DTYPE)

    wm = MLSTM_HEADS * HEAD_DIM
    hm = _stack(lambda i: (hm_ref[0, i] + hm_ref[1, i]) * _sigmoid(mo_ref[i]))
    ms = _mm_exact_rhs(hm * hm, _same_head(wm, wm)) * (1.0 / HEAD_DIM)
    ya = hm * lax.rsqrt(ms + EPS) * mnw_ref[...]

    ws = S5_GROUPS * S5_GROUP
    ys = _stack(lambda i: (ys_ref[0, :, i * ws:(i + 1) * ws] + ys_ref[1, :, i * ws:(i + 1) * ws]
                           + s5d_ref[...] * su_ref[:, i * ws:(i + 1) * ws]))
    ys = 0.5 * ys * (1.0 + jnp.tanh(math.sqrt(2.0 / math.pi) * (ys + 0.044715 * (ys * ys * ys))))
    ab = _mm(ys, glu_ref[...])
    yb = ab[:, :ws] * _sigmoid(ab[:, ws:])

    yc = _stack(lambda i: na_ref[i])

    yd = _stack(lambda i: (yd_ref[0, i] + yd_ref[1, i]) * _silu(dz_ref[i]))
    yd = yd * lax.rsqrt(jnp.mean(yd * yd, axis=-1, keepdims=True) + EPS) * dnw_ref[...]

    m = None
    for k, (y, w_ref) in enumerate(((ya, wa_ref), (yb, wb_ref), (yc, wc_ref), (yd, wd_ref))):
        gate = _sigmoid(jnp.dot(h, wg_ref[:, k * d_model:(k + 1) * d_model], preferred_element_type=F32))
        term = gate * _mm(y, w_ref[...])
        m = term if m is None else m + term
    out = _mm(m, wo_ref[...])
    for i in range(NB):
        o_ref[i] = x_ref[i] + mod_ref[i, 2:3, :] * out[i * TM:(i + 1) * TM, :]


def _merge(seq, mod, norm_w, hm, mo, mnw, ys, su, s5d, glu_w, yna, yd, dz, dnw, wg, wa, wb, wc, wd, wo,
           with_ctx):
    b, s, d = seq.shape
    nt = s // TM
    first = 0 if with_ctx else 1

    def tok(width):
        return pl.BlockSpec((NB, TM, width), lambda bp, j: (bp, first + j, 0))

    def tok2(width):
        return pl.BlockSpec((2, NB, TM, width), lambda bp, j: (0, bp, first + j, 0))

    def const(arr):
        return _const_spec(arr, 2, single=True)

    ws = S5_GROUPS * S5_GROUP
    return pl.pallas_call(
        _merge_kernel,
        grid=(b // NB, nt - first),
        in_specs=[tok(d), _mod_spec(d, first), const(norm_w),
                  tok2(MLSTM_HEADS * HEAD_DIM), tok(W_O), const(mnw),
                  pl.BlockSpec((2, TM, NB * ws), lambda bp, j: (0, first + j, bp)),
                  pl.BlockSpec((TM, NB * ws), lambda bp, j: (first + j, bp)),
                  const(s5d), const(glu_w),
                  tok(NA_HEADS * HEAD_DIM),
                  tok2(SSD_HEADS * HEAD_DIM), tok(W_DZ), const(dnw),
                  const(wg), const(wa), const(wb), const(wc), const(wd), const(wo)],
        out_specs=tok(d),
        out_shape=jax.ShapeDtypeStruct((b, s, d), F32),
        input_output_aliases={0: 0},
        compiler_params=pltpu.CompilerParams(vmem_limit_bytes=VMEM_LIMIT),
        name="merge",
    )(seq, mod, norm_w, hm, mo, mnw, ys, su, s5d, glu_w, yna, yd, dz, dnw, wg, wa, wb, wc, wd, wo)


def _ffn_kernel(x_ref, mod_ref, nw_ref, wi_ref, wo_ref, fw_ref, o_ref, *, final):
    hidden = wo_ref.shape[0]
    h = _stack(lambda i: _norm_mod(x_ref[i], nw_ref[...], mod_ref[i, 4:5, :], mod_ref[i, 3:4, :]))
    h = h.astype(MXU_DTYPE)
    a = jnp.dot(h, wi_ref[:, :hidden], preferred_element_type=F32)
    g = jnp.dot(h, wi_ref[:, hidden:], preferred_element_type=F32)
    out = _mm(_silu(a) * g, wo_ref[...])
    for i in range(NB):
        y = x_ref[i] + mod_ref[i, 5:6, :] * out[i * TM:(i + 1) * TM, :]
        if final:
            y = y * lax.rsqrt(jnp.mean(y * y, axis=-1, keepdims=True) + EPS) * fw_ref[...]
        o_ref[i] = y


def _ffn(seq, mod, norm_w, wi, wo, final_w, final):
    b, s, d = seq.shape
    nt = s // TM
    first = 1 if final else 0

    def const(arr):
        return _const_spec(arr, 2, single=True)

    tok_in = pl.BlockSpec((NB, TM, d), lambda bp, j: (bp, first + j, 0))
    if final:
        out_spec = pl.BlockSpec((NB, TM, d), lambda bp, j: (bp, j, 0))
        out_shape = jax.ShapeDtypeStruct((b, s - TM, d), F32)
        aliases = {}
    else:
        out_spec = tok_in
        out_shape = jax.ShapeDtypeStruct((b, s, d), F32)
        aliases = {0: 0}
    return pl.pallas_call(
        functools.partial(_ffn_kernel, final=final),
        grid=(b // NB, nt - first),
        in_specs=[tok_in, _mod_spec(d, first), const(norm_w), const(wi), const(wo), const(final_w)],
        out_specs=out_spec,
        out_shape=out_shape,
        input_output_aliases=aliases,
        compiler_params=pltpu.CompilerParams(vmem_limit_bytes=VMEM_LIMIT),
        name="ffn_final" if final else "ffn",
    )(seq, mod, norm_w, wi, wo, final_w)


def _rope_tables(t, width):
    nf = HEAD_DIM // 4
    inv = (ROPE_THETA ** (-np.arange(nf, dtype=np.float32) / nf)).astype(np.float32)
    tok = np.arange(t)
    ang_r = (tok // GRID_W).astype(np.float32)[:, None] * inv
    ang_c = (tok % GRID_W).astype(np.float32)[:, None] * inv
    cos_h = np.concatenate([np.cos(ang_r)] * 2 + [np.cos(ang_c)] * 2, axis=1)
    sin_h = np.concatenate([-np.sin(ang_r), np.sin(ang_r), -np.sin(ang_c), np.sin(ang_c)], axis=1)
    cos_x = np.tile(cos_h, (1, width // HEAD_DIM))
    sin_x = np.tile(sin_h, (1, width // HEAD_DIM))
    cos_t = np.concatenate([np.ones((TM, width), np.float32), cos_x], axis=0)
    sin_t = np.concatenate([np.zeros((TM, width), np.float32), sin_x], axis=0)
    scale = np.float32(HEAD_DIM ** -0.5)
    return (jnp.asarray(np.concatenate([cos_t * scale, cos_t], axis=1), F32),
            jnp.asarray(np.concatenate([sin_t * scale, sin_t], axis=1), F32))


def _na_bias_tables(rpb):
    col = jnp.arange(GRID_W)
    col0 = jnp.clip(col - NA_KW // 2, 0, GRID_W - NA_KW)
    in_win = (col[None, :] >= col0[:, None]) & (col[None, :] < col0[:, None] + NA_KW)
    dc = jnp.clip(col[None, :] - col[:, None], -(NA_KW - 1), NA_KW - 1) + (NA_KW - 1)
    per_row = jnp.where(in_win, rpb.astype(F32)[:, :, dc], NEG)
    bias = jnp.stack([per_row[:, off:off + NA_KH] for off in range(NA_KH)], axis=0)
    return bias.transpose(0, 1, 3, 2, 4).reshape(NA_KH, NA_HEADS * GRID_W, NA_KH * GRID_W)


def _block_diag(blocks):
    g, r, c = blocks.shape
    eye = jnp.eye(g, dtype=blocks.dtype)
    return (eye[:, None, :, None] * blocks[:, :, None, :]).reshape(g * r, g * c)


def _lanes(vec, at, width=W_SMALL):
    return jnp.zeros((1, width), F32).at[0, at:at + vec.shape[0]].set(vec.astype(F32))


def _gate_perm():
    perm = np.zeros((2, 2 * W_SMALL, W_SMALL), np.float32)
    for d in range(2):
        for h in range(MLSTM_HEADS):
            perm[d, L_MI + d * MLSTM_HEADS + h, G_LI + h] = 1.0
            perm[d, L_MF + d * MLSTM_HEADS + h, G_BCUM + h] = 1.0
        for h in range(SSD_HEADS):
            perm[d, L_DDT + d * SSD_HEADS + h, G_ACS + h] = 1.0
            perm[d, W_SMALL + L_DDT + d * SSD_HEADS + h, G_DT + h] = 1.0
    return jnp.asarray(perm, MXU_DTYPE)


def _log_decay_perms():
    pa = np.zeros((3, W_SMALL, W_SMALL), np.float32)
    pb = np.zeros((3, W_SMALL, W_SMALL), np.float32)
    ones_ab = np.zeros((2, W_SMALL), np.float32)
    slots = ([(G_BCUM + h, G_LI + h) for h in range(MLSTM_HEADS)]
             + [(G_ACS + h, G_ACS + h) for h in range(SSD_HEADS)])
    for n, (cum_lane, x_lane) in enumerate(slots):
        for k in range(3):
            pa[k, cum_lane, 8 * n + k] = 1.0
            pb[k, x_lane, 8 * n + 3 + k] = 1.0
            ones_ab[0, 8 * n + 3 + k] = 1.0
            ones_ab[1, 8 * n + k] = 1.0
    return jnp.asarray(pa, MXU_DTYPE), jnp.asarray(pb, MXU_DTYPE), jnp.asarray(ones_ab, F32)


def kernel(x, c, ctx, c_ctx, ada_w, ada_b, norm1_w, norm2_w, w_in, mlstm_conv_w, mlstm_conv_b, mlstm_ib, mlstm_fb, mlstm_norm_w, s5_lam_re, s5_lam_im, s5_log_dt, s5_b_re, s5_b_im, s5_c_re, s5_c_im, s5_d, s5_glu_w, na_rpb, ssd_conv_w, ssd_conv_b, ssd_a_log, ssd_dt_bias, ssd_d, ssd_norm_w, w_branch_a, w_branch_b, w_branch_c, w_branch_d, w_out, ffn_w_in, ffn_w_out, final_norm_w):
    b, t, d = x.shape
    depth = w_in.shape[0]
    assert ctx.shape[1] == TM and t % TM == 0 and t % GRID_W == 0 and b % 8 == 0 and b % NB == 0
    assert t // GRID_W >= NA_KH

    seq = jnp.concatenate([ctx, x], axis=1)

    pad = (-(b + 1)) % 8
    cc = jnp.concatenate([c, c_ctx[None, :], jnp.zeros((pad, d), F32)], axis=0)
    mod_all = _adaln(cc, ada_w, ada_b)
    mod_x = mod_all[:, :b].reshape(depth, b, 1, 6, d)
    mod_c = jnp.broadcast_to(mod_all[:, b].reshape(depth, 1, 1, 6, d), (depth, b, 1, 6, d))
    mod = jnp.concatenate([mod_c, mod_x], axis=2)

    cos_t, sin_t = _rope_tables(t, MLSTM_HEADS * HEAD_DIM)
    perm = _gate_perm()
    pa, pb, ones_ab = _log_decay_perms()
    n_state = S5_GROUPS * S5_STATE

    for l in range(depth):
        with_ctx = l < depth - 1
        wl = w_in[l]
        w_small = jnp.concatenate([wl[:, 1024:1040], wl[:, 3600:3616], jnp.zeros((d, W_SMALL - 32), F32)], axis=1)
        w_proj = jnp.concatenate([wl[:, 0:512], wl[:, 2576:3600], wl[:, 512:768], wl[:, 768:1024], w_small,
                                  wl[:, 1040:1296], wl[:, 1296:2064], wl[:, 2064:2576]], axis=1).astype(MXU_DTYPE)
        w_gate = wl[:, 3616:].astype(MXU_DTYPE)

        cv, v, mo, small, su, nqkv, dz = _proj(seq, mod[l], norm1_w[l][None, :], w_proj)

        conv_w = jnp.concatenate([mlstm_conv_w[l], ssd_conv_w[l]], axis=1)
        conv_w = jnp.concatenate([conv_w, jnp.zeros((1, W_CV), F32)], axis=0)
        conv_b = jnp.concatenate([mlstm_conv_b[l], ssd_conv_b[l]])[None, :]
        gbias = (_lanes(mlstm_ib[l].reshape(-1), L_MI) + _lanes(mlstm_fb[l].reshape(-1), L_MF)
                 + _lanes(ssd_dt_bias[l].reshape(-1), L_DDT))
        qk, xbc, gc, gr, la, lb = _prep(cv, small, conv_w, conv_b, cos_t, sin_t, gbias,
                                        _lanes(ssd_a_log[l].reshape(-1), L_DDT), perm, pa, pb, ones_ab)

        hm = _mlstm(qk, v, gc, la, lb)

        braw_re = _block_diag(jnp.swapaxes(s5_b_re[l], 1, 2))
        braw_im = _block_diag(jnp.swapaxes(s5_b_im[l], 1, 2))
        cw = jnp.concatenate([_block_diag(jnp.swapaxes(s5_c_re[l], 1, 2)),
                              -_block_diag(jnp.swapaxes(s5_c_im[l], 1, 2))], axis=0).astype(MXU_DTYPE)
        ys = _s5(su.reshape(-1, W_SU), s5_lam_re[l].reshape(2, 1, n_state), s5_lam_im[l].reshape(2, 1, n_state),
                 jnp.repeat(s5_log_dt[l], S5_STATE, axis=1).reshape(2, 1, n_state), braw_re, braw_im, cw, b)
        ys = ys.reshape(2, -1, b * W_SU)

        yna = _na(nqkv, _na_bias_tables(na_rpb[l]), with_ctx)

        yd = _ssd(xbc, gc, gr, jnp.repeat(ssd_d[l], HEAD_DIM)[None, :])

        seq = _merge(seq, mod[l], norm1_w[l][None, :], hm, mo, mlstm_norm_w[l][None, :], ys, su,
                     s5_d[l][None, :], s5_glu_w[l].astype(MXU_DTYPE), yna, yd, dz, ssd_norm_w[l][None, :],
                     w_gate, w_branch_a[l].astype(MXU_DTYPE), w_branch_b[l].astype(MXU_DTYPE),
                     w_branch_c[l].astype(MXU_DTYPE), w_branch_d[l].astype(MXU_DTYPE), w_out[l].astype(MXU_DTYPE),
                     with_ctx)
        seq = _ffn(seq, mod[l], norm2_w[l][None, :], ffn_w_in[l].astype(MXU_DTYPE), ffn_w_out[l].astype(MXU_DTYPE),
                   final_norm_w[None, :], not with_ctx)
    return seq
```

```python
import functools
import math

import jax
import jax.numpy as jnp
import numpy as np
from jax import lax
from jax.experimental import pallas as pl
from jax.experimental.pallas import tpu as pltpu

F32 = jnp.float32
MXU_DTYPE = jnp.bfloat16
HIGHEST = lax.Precision.HIGHEST

GRID_W = 64
EPS = 1e-6
CONV_K = 7
ROPE_THETA = 10000.0
HEAD_DIM = 64
MLSTM_HEADS = 4
S5_GROUPS = 16
S5_GROUP = 16
S5_STATE = 64
NA_HEADS = 4
NA_KH = 8
NA_KW = 16
SSD_HEADS = 8
SSD_GROUPS = 2
SSD_STATE = 128

TM = 256
NB = 2
NBS = 4
SUB = 128
HALO = 8
S5_SUB = 32
NEG = -1e30
VMEM_LIMIT = 56 * 1024 * 1024

W_QK, W_XBC, W_V, W_O, W_SMALL, W_SU, W_NQKV, W_DZ = 512, 1024, 256, 256, 128, 256, 768, 512
W_CV = W_QK + W_XBC
PROJ_WIDTHS = (W_CV, W_V, W_O, W_SMALL, W_SU, W_NQKV, W_DZ)
SU_POS = 4
L_MI, L_MF, L_DDT = 0, 8, 16
G_LI, G_BCUM, G_ACS, G_DT = 0, 4, 8, 16


def _mm(a, b):
    return jnp.dot(a.astype(MXU_DTYPE), b.astype(MXU_DTYPE), preferred_element_type=F32)


def _mm_nt(a, b):
    return lax.dot_general(a.astype(MXU_DTYPE), b.astype(MXU_DTYPE), (((1,), (1,)), ((), ())),
                           preferred_element_type=F32)


def _mm_tn(a, b):
    return lax.dot_general(a.astype(MXU_DTYPE), b.astype(MXU_DTYPE), (((0,), (0,)), ((), ())),
                           preferred_element_type=F32)


def _mm_f32(a, b):
    return jnp.dot(a, b, preferred_element_type=F32, precision=HIGHEST)


def _split3(x):
    hi = x.astype(MXU_DTYPE)
    r1 = x - hi.astype(F32)
    mid = r1.astype(MXU_DTYPE)
    lo = (r1 - mid.astype(F32)).astype(MXU_DTYPE)
    return hi, mid, lo


def _as_01(sel):
    if sel.dtype == jnp.bool_:
        sel = jnp.where(sel, 1.0, 0.0)
    return sel.astype(MXU_DTYPE)


def _mm_exact_rhs(x, sel, terms=3):
    sel = _as_01(sel)
    return sum(jnp.dot(p, sel, preferred_element_type=F32) for p in _split3(x)[:terms])


def _mm_exact_lhs(sel, x):
    sel = _as_01(sel)
    return sum(jnp.dot(sel, p, preferred_element_type=F32) for p in _split3(x))


def _sigmoid(x):
    return 1.0 / (1.0 + jnp.exp(-x))


def _silu(x):
    return x * _sigmoid(x)


def _softplus(x):
    return jnp.maximum(x, 0.0) + jnp.log(1.0 + jnp.exp(-jnp.abs(x)))


def _iota(shape, dim):
    return lax.broadcasted_iota(jnp.int32, shape, dim)


def _head_of_lane(width):
    return jnp.right_shift(_iota((1, width), 1), 6)


def _same_head(rows, cols):
    return jnp.right_shift(_iota((rows, cols), 0), 6) == jnp.right_shift(_iota((rows, cols), 1), 6)


def _expand_heads(cols, width):
    head = _head_of_lane(width)
    out = jnp.broadcast_to(cols[0], (cols[0].shape[0], width))
    for h in range(1, len(cols)):
        out = jnp.where(head == h, cols[h], out)
    return out


def _stack(fn):
    return jnp.concatenate([fn(i) for i in range(NB)], axis=0)


def _round_robin(chains):
    live = list(chains)
    while live:
        for chain in list(live):
            try:
                next(chain)
            except StopIteration:
                live.remove(chain)


def _tile_of(d, j, nt):
    return jnp.where(d == 0, j, jnp.where(j == 0, 0, nt - j))


def _norm_mod(x, w, scale, shift):
    y = x * lax.rsqrt(jnp.mean(x * x, axis=-1, keepdims=True) + EPS) * w
    return y * (1.0 + scale) + shift


def _order_mask(rev, n):
    diff = (_iota((n, n), 1) - _iota((n, n), 0)) * jnp.where(rev, -1, 1)
    return diff <= 0


def _const_spec(arr, n_grid, single=False):
    kwargs = {"pipeline_mode": pl.Buffered(1)} if single else {}
    return pl.BlockSpec(arr.shape, lambda *_: (0,) * arr.ndim, **kwargs)


def _mod_spec(d, first):
    return pl.BlockSpec((NB, None, 6, d), lambda bp, j: (bp, jnp.minimum(first + j, 1), 0, 0))


def _adaln_kernel(c_ref, w_ref, b_ref, o_ref):
    o_ref[...] = _mm_f32(_silu(c_ref[...]), w_ref[...]) + b_ref[...]


def _adaln(cc, ada_w, ada_b):
    depth, d, n = ada_w.shape
    tn = 768
    return pl.pallas_call(
        _adaln_kernel,
        grid=(depth, n // tn),
        in_specs=[pl.BlockSpec(cc.shape, lambda l, i: (0, 0)),
                  pl.BlockSpec((None, d, tn), lambda l, i: (l, 0, i)),
                  pl.BlockSpec((None, 1, tn), lambda l, i: (l, 0, i))],
        out_specs=pl.BlockSpec((None, cc.shape[0], tn), lambda l, i: (l, 0, i)),
        out_shape=jax.ShapeDtypeStruct((depth, cc.shape[0], n), F32),
        name="adaln",
    )(cc, ada_w, ada_b.reshape(depth, 1, n))


def _proj_kernel(x_ref, mod_ref, nw_ref, w_ref, *out_refs):
    h = _stack(lambda i: _norm_mod(x_ref[i], nw_ref[...], mod_ref[i, 1:2, :], mod_ref[i, 0:1, :]))
    h = h.astype(MXU_DTYPE)
    off = 0
    for pos, (ref, n) in enumerate(zip(out_refs, PROJ_WIDTHS)):
        res = jnp.dot(h, w_ref[:, off:off + n], preferred_element_type=F32)
        if pos == SU_POS:
            ref[...] = jnp.concatenate([res[i * TM:(i + 1) * TM, :] for i in range(NB)], axis=1)
        else:
            for i in range(NB):
                ref[i] = res[i * TM:(i + 1) * TM, :]
        off += n


def _proj(seq, mod, norm_w, w_proj):
    b, s, d = seq.shape
    nt = s // TM

    def tok(width):
        return pl.BlockSpec((NB, TM, width), lambda bp, j: (bp, j, 0))

    out_specs = [tok(w) for w in PROJ_WIDTHS]
    out_shape = [jax.ShapeDtypeStruct((b, s, w), F32) for w in PROJ_WIDTHS]
    out_specs[SU_POS] = pl.BlockSpec((TM, NB * W_SU), lambda bp, j: (j, bp))
    out_shape[SU_POS] = jax.ShapeDtypeStruct((s, b * W_SU), F32)
    return pl.pallas_call(
        _proj_kernel,
        grid=(b // NB, nt),
        in_specs=[tok(d), _mod_spec(d, 0), _const_spec(norm_w, 2), _const_spec(w_proj, 2, single=True)],
        out_specs=out_specs,
        out_shape=out_shape,
        compiler_params=pltpu.CompilerParams(vmem_limit_bytes=VMEM_LIMIT),
        name="proj",
    )(seq, mod, norm_w, w_proj)


def _prep_kernel(cv_ref, cvp_ref, cvn_ref, cw_ref, cb_ref, cos_ref, sin_ref, sm_ref, gb_ref, al_ref, perm_ref,
                 pa_ref, pb_ref, one_ref, qk_ref, xbc_ref, gc_ref, gr_ref, la_ref, lb_ref, ext, *, nt):
    t = pl.program_id(1)
    prev_ok = t > 1
    next_ok = (t >= 1) & (t < nt - 1)
    lane = _iota((1, W_SMALL), 1)
    first = jnp.bitwise_and(_iota((1, W_QK), 1), 31) < 16
    r = _iota((TM, TM), 0)
    c = _iota((TM, TM), 1)
    same_chunk = jnp.right_shift(r, 7) == jnp.right_shift(c, 7)
    cum = (lane >= G_BCUM) & (lane < G_DT)

    for i in range(NB):
        ext[i, 0:HALO, :] = jnp.where(prev_ok, cvp_ref[i], 0.0)
        ext[i, HALO:HALO + TM, :] = cv_ref[i]
        ext[i, HALO + TM:, :] = jnp.where(next_ok, cvn_ref[i], 0.0)
        acc = cb_ref[...] + cw_ref[0:1, :] * ext[i, pl.ds(HALO - CONV_K // 2, TM), :]
        for k in range(1, CONV_K):
            acc = acc + cw_ref[k:k + 1, :] * ext[i, pl.ds(HALO - CONV_K // 2 + k, TM), :]
        acc = _silu(acc)
        xbc_ref[i] = acc[:, W_QK:]
        qk = acc[:, :W_QK]
        partner = jnp.where(first, pltpu.roll(qk, W_QK - 16, 1), pltpu.roll(qk, 16, 1))
        qk_ref[i] = qk * cos_ref[...] + partner * sin_ref[...]

        g_all = sm_ref[i] + gb_ref[...]
        dt = _softplus(g_all)
        src = jnp.where(lane < L_MF, g_all,
                        jnp.where(lane < L_DDT, -_softplus(-g_all), dt * (-jnp.exp(al_ref[...]))))
        src = jnp.concatenate([src, dt], axis=1)
        for d in range(2):
            tri = same_chunk & ((c <= r) if d == 0 else (c >= r))
            g = _mm_exact_rhs(src, perm_ref[d])
            g = jnp.where(cum, _mm_exact_lhs(tri, jnp.where(cum, g, 0.0)), g)
            gc_ref[d, i] = g
            a_mat = one_ref[0:1, :] + sum(jnp.dot(p, pa_ref[k], preferred_element_type=F32)
                                           for k, p in enumerate(_split3(g)))
            cv = jnp.where(lane < MLSTM_HEADS, g - pltpu.roll(g, W_SMALL - G_BCUM, 1), -g)
            b_mat = one_ref[1:2, :] + sum(jnp.dot(p, pb_ref[k], preferred_element_type=F32)
                                           for k, p in enumerate(_split3(cv)))
            la_ref[d, i] = a_mat.astype(MXU_DTYPE)
            for ci in range(TM // SUB):
                gr_ref[d, i, ci] = g[ci * SUB:(ci + 1) * SUB, :].T
                lb_ref[d, i, ci] = b_mat[ci * SUB:(ci + 1) * SUB, :].T.astype(MXU_DTYPE)


def _prep(cv, small, conv_w, conv_b, cos_t, sin_t, gbias, alog, perm, pa, pb, ones_ab):
    b, s, _ = cv.shape
    nt = s // TM
    per = TM // HALO
    last = s // HALO - 1
    nsub = TM // SUB
    return pl.pallas_call(
        functools.partial(_prep_kernel, nt=nt),
        grid=(b // NB, nt),
        in_specs=[pl.BlockSpec((NB, TM, W_CV), lambda bp, j: (bp, j, 0)),
                  pl.BlockSpec((NB, HALO, W_CV), lambda bp, j: (bp, jnp.maximum(j * per - 1, 0), 0)),
                  pl.BlockSpec((NB, HALO, W_CV), lambda bp, j: (bp, jnp.minimum((j + 1) * per, last), 0)),
                  _const_spec(conv_w, 2), _const_spec(conv_b, 2),
                  pl.BlockSpec((TM, W_QK), lambda bp, j: (j, 0)),
                  pl.BlockSpec((TM, W_QK), lambda bp, j: (j, 0)),
                  pl.BlockSpec((NB, TM, W_SMALL), lambda bp, j: (bp, j, 0)),
                  _const_spec(gbias, 2), _const_spec(alog, 2), _const_spec(perm, 2),
                  _const_spec(pa, 2), _const_spec(pb, 2), _const_spec(ones_ab, 2)],
        out_specs=[pl.BlockSpec((NB, TM, W_QK), lambda bp, j: (bp, j, 0)),
                   pl.BlockSpec((NB, TM, W_XBC), lambda bp, j: (bp, j, 0)),
                   pl.BlockSpec((2, NB, TM, W_SMALL), lambda bp, j: (0, bp, j, 0)),
                   pl.BlockSpec((2, NB, nsub, W_SMALL, SUB), lambda bp, j: (0, bp, j, 0, 0)),
                   pl.BlockSpec((2, NB, TM, W_SMALL), lambda bp, j: (0, bp, j, 0)),
                   pl.BlockSpec((2, NB, nsub, W_SMALL, SUB), lambda bp, j: (0, bp, j, 0, 0))],
        out_shape=[jax.ShapeDtypeStruct((b, s, W_QK), F32),
                   jax.ShapeDtypeStruct((b, s, W_XBC), F32),
                   jax.ShapeDtypeStruct((2, b, s, W_SMALL), F32),
                   jax.ShapeDtypeStruct((2, b, s // SUB, W_SMALL, SUB), F32),
                   jax.ShapeDtypeStruct((2, b, s, W_SMALL), MXU_DTYPE),
                   jax.ShapeDtypeStruct((2, b, s // SUB, W_SMALL, SUB), MXU_DTYPE)],
        scratch_shapes=[pltpu.VMEM((NB, TM + 2 * HALO, W_CV), F32)],
        compiler_params=pltpu.CompilerParams(vmem_limit_bytes=VMEM_LIMIT),
        name="prep",
    )(cv, cv, cv, conv_w, conv_b, cos_t, sin_t, small, gbias, alog, perm, pa, pb, ones_ab)


def _scan_specs(nt, widths, kinds):
    nsub = TM // SUB
    specs = [pl.BlockSpec((NBS, TM, w), lambda d, j, bp: (bp, _tile_of(d, j, nt), 0)) for w in widths]
    for kind in kinds:
        if kind == "col":
            specs.append(pl.BlockSpec((None, NBS, TM, W_SMALL), lambda d, j, bp: (d, bp, _tile_of(d, j, nt), 0)))
        else:
            specs.append(pl.BlockSpec((None, NBS, nsub, W_SMALL, SUB),
                                      lambda d, j, bp: (d, bp, _tile_of(d, j, nt), 0, 0)))
    return specs


def _mlstm_kernel(qk_ref, v_ref, gc_ref, la_ref, lb_ref, h_ref, c_scr, n_scr, m_scr):
    d = pl.program_id(0)
    j = pl.program_id(1)
    bp = pl.program_id(2)
    rev = d == 1
    width = MLSTM_HEADS * HEAD_DIM
    nsub = TM // SUB
    heads = list(range(MLSTM_HEADS))
    stacked = (MLSTM_HEADS * SUB, SUB)
    t_minus_s = (jnp.bitwise_and(_iota(stacked, 0), SUB - 1) - _iota(stacked, 1)) * jnp.where(rev, -1, 1)
    mask_s = t_minus_s >= 0
    head = _head_of_lane(width)
    same_head = _same_head(width, width)
    lane = _iota((1, W_SMALL), 1)
    valid = lane < MLSTM_HEADS
    group = jnp.right_shift(lane, 3)
    to_heads = _as_01(jnp.right_shift(_iota((W_SMALL, width), 1), 6) == _iota((W_SMALL, width), 0))
    from_heads = _as_01(jnp.right_shift(_iota((width, W_SMALL), 0), 6) == _iota((width, W_SMALL), 1))
    ones_rows = jnp.ones((8, SUB), MXU_DTYPE)

    @pl.when(j == 0)
    def _():
        for i in range(NBS):
            c_scr[bp * NBS + i] = jnp.zeros(c_scr.shape[1:], F32)
            n_scr[bp * NBS + i] = jnp.zeros(n_scr.shape[1:], F32)
            m_scr[bp * NBS + i] = jnp.zeros(m_scr.shape[1:], F32)

    def chain(i):
        bi = bp * NBS + i
        c_st = c_scr[bi]
        n_st = n_scr[bi, 0:1, :]
        m_st = m_scr[bi, 0:1, :]
        chunks = []
        for step in range(nsub):
            ci = jnp.where(rev, nsub - 1 - step, step)
            rows = pl.ds(pl.multiple_of(ci * SUB, SUB), SUB)
            q = qk_ref[i, rows, 0:width]
            k = qk_ref[i, rows, width:2 * width]
            qb = q.astype(MXU_DTYPE)
            kb = k.astype(MXU_DTYPE)
            vb = v_ref[i, rows, :].astype(MXU_DTYPE)
            li = gc_ref[i, rows, :]
            bc = pltpu.roll(li, W_SMALL - G_BCUM, 1)
            la = la_ref[i, rows, :]
            lb = lb_ref[i, ci]

            la_s = jnp.concatenate([jnp.where(group == h, la, 0) for h in heads], axis=0)
            tiles = [slice((h // 2) * 128, (h // 2 + 1) * 128) for h in heads]
            q_s = [jnp.where(head[:, tiles[h]] == h, qb[:, tiles[h]], 0) for h in heads]
            log_w = jnp.where(mask_s, jnp.dot(la_s, lb, preferred_element_type=F32), NEG)
            yield
            m_in_s = jnp.max(log_w, axis=1, keepdims=True)
            qk = jnp.concatenate([_mm_nt(jnp.concatenate(q_s[2 * p:2 * p + 2], axis=0), kb[:, tiles[2 * p]])
                                  for p in range(MLSTM_HEADS // 2)], axis=0)
            sm = qk * jnp.exp(log_w - m_in_s)
            rs = jnp.sum(sm, axis=1, keepdims=True)
            sm = sm.astype(MXU_DTYPE)
            yield
            pv = jnp.dot(sm, vb, preferred_element_type=F32)
            yield
            num0 = pv[0:SUB, :]
            den0 = jnp.broadcast_to(rs[0:SUB, :], (SUB, W_SMALL))
            m_in = jnp.broadcast_to(m_in_s[0:SUB, :], (SUB, W_SMALL))
            for h in heads[1:]:
                blk = slice(h * SUB, (h + 1) * SUB)
                num0 = jnp.where(head == h, pv[blk, :], num0)
                den0 = jnp.where(lane == h, rs[blk, :], den0)
                m_in = jnp.where(lane == h, m_in_s[blk, :], m_in)
            b_last = jnp.where(rev, bc[0:1, :], bc[SUB - 1:SUB, :])
            log_k = b_last - bc + li
            mk = jnp.max(log_k, axis=0, keepdims=True)
            wk_full = _mm_exact_rhs(jnp.where(valid, jnp.exp(log_k - mk), 0.0), to_heads, terms=2)
            kw = (k * wk_full).astype(MXU_DTYPE)
            yield
            kv0 = jnp.where(same_head, _mm_tn(kw, vb), 0.0)
            ks0 = jnp.dot(ones_rows, kw, preferred_element_type=F32)[0:1, :]
            chunks.append((rows, q, qb, bc, num0, den0, m_in, b_last, mk, kv0, ks0))
            yield

        for rows, q, qb, bc, num0, den0, m_in, b_last, mk, kv0, ks0 in chunks:
            inter = bc + m_st
            m_t = jnp.maximum(inter, m_in)
            r = jnp.exp(m_in - m_t)
            g = jnp.exp(inter - m_t)
            den = r * den0 + g * jnp.dot((q * n_st).astype(MXU_DTYPE), from_heads, preferred_element_type=F32)
            inv = 1.0 / jnp.maximum(jnp.abs(den), jnp.exp(-m_t))
            ir_full = _mm_exact_rhs(jnp.where(valid, inv * r, 0.0), to_heads, terms=2)
            ig_full = _mm_exact_rhs(jnp.where(valid, inv * g, 0.0), to_heads, terms=2)
            h_ref[i, rows, :] = num0 * ir_full + _mm(qb, c_st) * ig_full
            yield
            m_new = jnp.maximum(b_last + m_st, mk)
            scales = jnp.concatenate([jnp.exp(b_last + m_st - m_new), jnp.exp(mk - m_new),
                                      jnp.zeros((6, W_SMALL), F32)], axis=0)
            scales_full = _mm_exact_rhs(jnp.where(valid, scales, 0.0), to_heads)
            c_st = c_st * scales_full[0:1, :] + kv0 * scales_full[1:2, :]
            n_st = n_st * scales_full[0:1, :] + ks0 * scales_full[1:2, :]
            m_st = jnp.where(valid, m_new, 0.0)
            yield

        c_scr[bi] = c_st
        n_scr[bi, 0:1, :] = n_st
        m_scr[bi, 0:1, :] = m_st

    _round_robin([chain(i) for i in range(NBS)])


def _mlstm(qk, v, gc, la, lb):
    b, s, _ = qk.shape
    nt = s // TM
    width = MLSTM_HEADS * HEAD_DIM
    return pl.pallas_call(
        _mlstm_kernel,
        grid=(2, nt, b // NBS),
        in_specs=_scan_specs(nt, (2 * width, width), ("col", "col", "row")),
        out_specs=pl.BlockSpec((None, NBS, TM, width), lambda d, j, bp: (d, bp, _tile_of(d, j, nt), 0)),
        out_shape=jax.ShapeDtypeStruct((2, b, s, width), F32),
        scratch_shapes=[pltpu.VMEM((b, width, width), F32), pltpu.VMEM((b, 8, width), F32),
                        pltpu.VMEM((b, 8, W_SMALL), F32)],
        compiler_params=pltpu.CompilerParams(vmem_limit_bytes=VMEM_LIMIT),
        name="mlstm",
    )(qk, v, gc, la, lb)


def _s5_kernel(u_ref, lre_ref, lim_ref, ldt_ref, bre_ref, bim_ref, cw_ref, y_ref,
               wbu_scr, are_scr, aim_scr, hre_scr, him_scr, hbuf, *, nb):
    d = pl.program_id(0)
    j = pl.program_id(1)
    rev = d == 1
    n_state = S5_GROUPS * S5_STATE

    @pl.when(j == 0)
    def _():
        lre = lre_ref[...]
        lim = lim_ref[...]
        dt = jnp.exp(ldt_ref[...])
        mag = jnp.exp(lre * dt)
        a_re = mag * jnp.cos(lim * dt)
        a_im = mag * jnp.sin(lim * dt)
        den = lre * lre + lim * lim
        nr = a_re - 1.0
        coef_re = (nr * lre + a_im * lim) / den
        coef_im = (a_im * lre - nr * lim) / den
        wbu_scr[:, :n_state] = (coef_re * bre_ref[...] - coef_im * bim_ref[...]).astype(MXU_DTYPE)
        wbu_scr[:, n_state:] = (coef_re * bim_ref[...] + coef_im * bre_ref[...]).astype(MXU_DTYPE)
        are_scr[...] = jnp.broadcast_to(a_re, are_scr.shape)
        aim_scr[...] = jnp.broadcast_to(a_im, aim_scr.shape)
        hre_scr[...] = jnp.zeros(hre_scr.shape, F32)
        him_scr[...] = jnp.zeros(him_scr.shape, F32)

    a_re = are_scr[...]
    a_im = aim_scr[...]
    rows_sub = S5_SUB * nb
    n_sub = TM // S5_SUB
    carry = (hre_scr[...], him_scr[...])
    for sb in range(n_sub):
        sbi = jnp.where(rev, n_sub - 1 - sb, sb)
        rows = pl.ds(pl.multiple_of(sbi * rows_sub, rows_sub), rows_sub)
        hbuf[...] = jnp.dot(u_ref[rows, :].astype(MXU_DTYPE), wbu_scr[...], preferred_element_type=F32)

        def step(i, hc):
            ti = jnp.where(rev, S5_SUB - 1 - i, i)
            r = pl.ds(pl.multiple_of(ti * nb, nb), nb)
            h_re, h_im = hc
            n_re = a_re * h_re - a_im * h_im + hbuf[r, :n_state]
            n_im = a_re * h_im + a_im * h_re + hbuf[r, n_state:]
            hbuf[r, :n_state] = n_re
            hbuf[r, n_state:] = n_im
            return n_re, n_im

        carry = lax.fori_loop(0, S5_SUB, step, carry)
        y_ref[rows, :] = jnp.dot(hbuf[...].astype(MXU_DTYPE), cw_ref[...], preferred_element_type=F32)
    hre_scr[...] = carry[0]
    him_scr[...] = carry[1]


def _s5(u_tm, lam_re, lam_im, log_dt, braw_re, braw_im, cw, nb):
    rows, width = u_tm.shape
    s = rows // nb
    nt = s // TM
    n_state = S5_GROUPS * S5_STATE

    def per_dir():
        return pl.BlockSpec((None, 1, n_state), lambda d, j: (d, 0, 0))

    return pl.pallas_call(
        functools.partial(_s5_kernel, nb=nb),
        grid=(2, nt),
        in_specs=[pl.BlockSpec((TM * nb, width), lambda d, j: (_tile_of(d, j, nt), 0)),
                  per_dir(), per_dir(), per_dir(),
                  _const_spec(braw_re, 2), _const_spec(braw_im, 2), _const_spec(cw, 2)],
        out_specs=pl.BlockSpec((None, TM * nb, width), lambda d, j: (d, _tile_of(d, j, nt), 0)),
        out_shape=jax.ShapeDtypeStruct((2, rows, width), F32),
        scratch_shapes=[pltpu.VMEM((width, 2 * n_state), MXU_DTYPE),
                        pltpu.VMEM((nb, n_state), F32), pltpu.VMEM((nb, n_state), F32),
                        pltpu.VMEM((nb, n_state), F32), pltpu.VMEM((nb, n_state), F32),
                        pltpu.VMEM((S5_SUB * nb, 2 * n_state), F32)],
        compiler_params=pltpu.CompilerParams(vmem_limit_bytes=VMEM_LIMIT),
        name="s5",
    )(u_tm, lam_re, lam_im, log_dt, braw_re, braw_im, cw)


def _na_kernel(q_ref, k_ref, v_ref, tbl_ref, o_ref, *, first_tile, n_rows):
    t = first_tile + pl.program_id(1)
    width = NA_HEADS * HEAD_DIM
    head = _head_of_lane(width)
    scale = HEAD_DIM ** -0.5
    k_ctx = k_ref[0:TM, :].astype(MXU_DTYPE)
    v_ctx = v_ref[0:TM, :].astype(MXU_DTYPE)

    def stack_heads(q):
        return jnp.concatenate([jnp.where(head == h, q, 0.0) for h in range(NA_HEADS)], axis=0).astype(MXU_DTYPE)

    def unstack_heads(o, n):
        acc = o[0:n, :]
        for h in range(1, NA_HEADS):
            acc = jnp.where(head == h, o[h * n:(h + 1) * n, :], acc)
        return acc

    @pl.when(t == 0)
    def _():
        s = _mm_nt(stack_heads(q_ref[...] * scale), k_ctx)
        p = jnp.exp(s - jnp.max(s, axis=1, keepdims=True))
        o = _mm(p, v_ctx) / jnp.sum(p, axis=1, keepdims=True)
        o_ref[...] = unstack_heads(o, TM)

    @pl.when(t > 0)
    def _():
        rows_per_tile = TM // GRID_W
        n_lat = NA_KH * GRID_W
        for rr in range(rows_per_tile):
            r = (t - 1) * rows_per_tile + rr
            row_start = jnp.clip(r - NA_KH // 2, 0, n_rows - NA_KH)
            off = (NA_KH - 1) - (r - row_start)
            win = pl.ds(pl.multiple_of(TM + row_start * GRID_W, GRID_W), n_lat)
            qs = stack_heads(q_ref[rr * GRID_W:(rr + 1) * GRID_W, :] * scale)
            s_lat = _mm_nt(qs, k_ref[win, :]) + tbl_ref[off]
            s_ctx = _mm_nt(qs, k_ctx)
            m = jnp.maximum(jnp.max(s_lat, axis=1, keepdims=True), jnp.max(s_ctx, axis=1, keepdims=True))
            p_lat = jnp.exp(s_lat - m)
            p_ctx = jnp.exp(s_ctx - m)
            den = jnp.sum(p_lat, axis=1, keepdims=True) + jnp.sum(p_ctx, axis=1, keepdims=True)
            o = (_mm(p_lat, v_ref[win, :]) + _mm(p_ctx, v_ctx)) / den
            o_ref[rr * GRID_W:(rr + 1) * GRID_W, :] = unstack_heads(o, GRID_W)


def _na(nqkv, tbl, with_ctx):
    b, s, _ = nqkv.shape
    nt = s // TM
    width = NA_HEADS * HEAD_DIM
    first_tile = 0 if with_ctx else 1
    n_rows = (s - TM) // GRID_W
    return pl.pallas_call(
        functools.partial(_na_kernel, first_tile=first_tile, n_rows=n_rows),
        grid=(b, nt - first_tile),
        in_specs=[pl.BlockSpec((None, TM, width), lambda bi, j: (bi, first_tile + j, 0)),
                  pl.BlockSpec((None, s, width), lambda bi, j: (bi, 0, 1)),
                  pl.BlockSpec((None, s, width), lambda bi, j: (bi, 0, 2)),
                  _const_spec(tbl, 2)],
        out_specs=pl.BlockSpec((None, TM, width), lambda bi, j: (bi, first_tile + j, 0)),
        out_shape=jax.ShapeDtypeStruct((b, s, width), F32),
        compiler_params=pltpu.CompilerParams(vmem_limit_bytes=VMEM_LIMIT),
        name="na",
    )(nqkv, nqkv, nqkv, tbl)


def _ssd_kernel(x_ref, gc_ref, gr_ref, la_ref, lb_ref, dsk_ref, y_ref, st_scr):
    d = pl.program_id(0)
    j = pl.program_id(1)
    bp = pl.program_id(2)
    rev = d == 1
    width = SSD_HEADS * HEAD_DIM
    gn = SSD_GROUPS * SSD_STATE
    per_group = SSD_HEADS // SSD_GROUPS
    gw = per_group * HEAD_DIM
    nsub = TM // SUB
    stacked = (per_group * SUB, SUB)
    t_minus_s = (jnp.bitwise_and(_iota(stacked, 0), SUB - 1) - _iota(stacked, 1)) * jnp.where(rev, -1, 1)
    mask_s = t_minus_s >= 0
    head = _head_of_lane(width)
    skip = jnp.where(rev, 0.0, 1.0) * dsk_ref[...]
    lane = _iota((1, W_SMALL), 1)
    valid = (lane >= G_ACS) & (lane < G_ACS + SSD_HEADS)
    group = jnp.right_shift(lane, 3)
    to_heads = _as_01(jnp.right_shift(_iota((W_SMALL, width), 1), 6) == _iota((W_SMALL, width), 0) - G_ACS)

    @pl.when(j == 0)
    def _():
        for i in range(NBS):
            st_scr[bp * NBS + i] = jnp.zeros(st_scr.shape[1:], F32)

    def chain(i):
        bi = bp * NBS + i
        st = st_scr[bi]
        chunks = []
        for step in range(nsub):
            ci = jnp.where(rev, nsub - 1 - step, step)
            rows = pl.ds(pl.multiple_of(ci * SUB, SUB), SUB)
            xs = x_ref[i, rows, 0:width]
            bm = x_ref[i, rows, width:width + gn].astype(MXU_DTYPE)
            cm = x_ref[i, rows, width + gn:width + 2 * gn].astype(MXU_DTYPE)
            xsb = xs.astype(MXU_DTYPE)
            acs = gc_ref[i, rows, :]
            dt = pltpu.roll(acs, W_SMALL - (G_DT - G_ACS), 1)
            grow = gr_ref[i, ci]
            la = la_ref[i, rows, :]
            lb = lb_ref[i, ci]

            a_last = jnp.where(rev, acs[0:1, :], acs[SUB - 1:SUB, :])
            e_full = _mm_exact_rhs(jnp.where(valid, jnp.exp(acs), 0.0), to_heads, terms=2)
            xw = xs * _mm_exact_rhs(jnp.where(valid, jnp.exp(a_last - acs) * dt, 0.0), to_heads, terms=2)
            last = jnp.broadcast_to(jnp.where(valid, jnp.exp(a_last), 0.0), (8, W_SMALL))
            last_full = _mm_exact_rhs(last, to_heads)[0:1, :]
            yield

            y_parts, st_parts, c_parts = [], [], []
            for g in range(SSD_GROUPS):
                b_g = bm[:, g * SSD_STATE:(g + 1) * SSD_STATE]
                c_g = cm[:, g * SSD_STATE:(g + 1) * SSD_STATE]
                lanes = slice(g * gw, (g + 1) * gw)
                hs = range(g * per_group, (g + 1) * per_group)
                la_s = jnp.concatenate([jnp.where(group == MLSTM_HEADS + h, la, 0) for h in hs], axis=0)
                dt_s = jnp.concatenate([jnp.broadcast_to(grow[G_DT + h:G_DT + h + 1, :], (SUB, SUB)) for h in hs],
                                       axis=0)
                decay = jnp.exp(jnp.where(mask_s, jnp.dot(la_s, lb, preferred_element_type=F32), NEG)) * dt_s
                cb = _mm_nt(c_g, b_g)
                m_s = (jnp.concatenate([cb] * per_group, axis=0) * decay).astype(MXU_DTYPE)
                yield
                yd = jnp.dot(m_s, xsb[:, lanes], preferred_element_type=F32)
                y_g = yd[0:SUB, :]
                for hh in range(1, per_group):
                    y_g = jnp.where(head[:, lanes] == g * per_group + hh, yd[hh * SUB:(hh + 1) * SUB, :], y_g)
                y_parts.append(y_g)
                st_parts.append(_mm_tn(b_g, xw[:, lanes]))
                c_parts.append(c_g)
                yield
            chunks.append((rows, jnp.concatenate(y_parts, axis=1) + skip * xs, jnp.concatenate(st_parts, axis=1),
                           c_parts, e_full, last_full))

        for rows, y_in, st_in, c_parts, e_full, last_full in chunks:
            y_off = jnp.concatenate([_mm(c_parts[g], st[:, g * gw:(g + 1) * gw]) for g in range(SSD_GROUPS)], axis=1)
            y_ref[i, rows, :] = y_in + y_off * e_full
            st = st * last_full + st_in
            yield
        st_scr[bi] = st

    _round_robin([chain(i) for i in range(NBS)])


def _ssd(xbc, gc, gr, la, lb, dskip):
    b, s, cw = xbc.shape
    nt = s // TM
    width = SSD_HEADS * HEAD_DIM
    return pl.pallas_call(
        _ssd_kernel,
        grid=(2, nt, b // NBS),
        in_specs=_scan_specs(nt, (cw,), ("col", "row", "col", "row")) + [_const_spec(dskip, 3)],
        out_specs=pl.BlockSpec((None, NBS, TM, width), lambda d, j, bp: (d, bp, _tile_of(d, j, nt), 0)),
        out_shape=jax.ShapeDtypeStruct((2, b, s, width), F32),
        scratch_shapes=[pltpu.VMEM((b, SSD_STATE, width), F32)],
        compiler_params=pltpu.CompilerParams(vmem_limit_bytes=VMEM_LIMIT),
        name="ssd",
    )(xbc, gc, gr, la, lb, dskip)


def _merge_kernel(x_ref, mod_ref, nw_ref, hm_ref, mo_ref, mnw_ref, ys_ref, su_ref, s5d_ref, glu_ref, na_ref,
                  yd_ref, dz_ref, dnw_ref, wg_ref, wa_ref, wb_ref, wc_ref, wd_ref, wo_ref, o_ref):
    d_model = x_ref.shape[2]
    h = _stack(lambda i: _norm_mod(x_ref[i], nw_ref[...], mod_ref[i, 1:2, :], mod_ref[i, 0:1, :]))
    h = h.astype(MXU_DTYPE)

    wm = MLSTM_HEADS * HEAD_DIM
    hm = _stack(lambda i: (hm_ref[0, i] + hm_ref[1, i]) * _sigmoid(mo_ref[i]))
    ms = _mm_exact_rhs(hm * hm, _same_head(wm, wm)) * (1.0 / HEAD_DIM)
    ya = hm * lax.rsqrt(ms + EPS) * mnw_ref[...]

    ws = S5_GROUPS * S5_GROUP
    ys = _stack(lambda i: (ys_ref[0, :, i * ws:(i + 1) * ws] + ys_ref[1, :, i * ws:(i + 1) * ws]
                           + s5d_ref[...] * su_ref[:, i * ws:(i + 1) * ws]))
    ys = 0.5 * ys * (1.0 + jnp.tanh(math.sqrt(2.0 / math.pi) * (ys + 0.044715 * (ys * ys * ys))))
    ab = _mm(ys, glu_ref[...])
    yb = ab[:, :ws] * _sigmoid(ab[:, ws:])

    yc = _stack(lambda i: na_ref[i])

    yd = _stack(lambda i: (yd_ref[0, i] + yd_ref[1, i]) * _silu(dz_ref[i]))
    yd = yd * lax.rsqrt(jnp.mean(yd * yd, axis=-1, keepdims=True) + EPS) * dnw_ref[...]

    m = None
    for k, (y, w_ref) in enumerate(((ya, wa_ref), (yb, wb_ref), (yc, wc_ref), (yd, wd_ref))):
        gate = _sigmoid(jnp.dot(h, wg_ref[:, k * d_model:(k + 1) * d_model], preferred_element_type=F32))
        term = gate * _mm(y, w_ref[...])
        m = term if m is None else m + term
    out = _mm(m, wo_ref[...])
    for i in range(NB):
        o_ref[i] = x_ref[i] + mod_ref[i, 2:3, :] * out[i * TM:(i + 1) * TM, :]


def _merge(seq, mod, norm_w, hm, mo, mnw, ys, su, s5d, glu_w, yna, yd, dz, dnw, wg, wa, wb, wc, wd, wo,
           with_ctx):
    b, s, d = seq.shape
    nt = s // TM
    first = 0 if with_ctx else 1

    def tok(width):
        return pl.BlockSpec((NB, TM, width), lambda bp, j: (bp, first + j, 0))

    def tok2(width):
        return pl.BlockSpec((2, NB, TM, width), lambda bp, j: (0, bp, first + j, 0))

    def const(arr):
        return _const_spec(arr, 2, single=True)

    ws = S5_GROUPS * S5_GROUP
    return pl.pallas_call(
        _merge_kernel,
        grid=(b // NB, nt - first),
        in_specs=[tok(d), _mod_spec(d, first), const(norm_w),
                  tok2(MLSTM_HEADS * HEAD_DIM), tok(W_O), const(mnw),
                  pl.BlockSpec((2, TM, NB * ws), lambda bp, j: (0, first + j, bp)),
                  pl.BlockSpec((TM, NB * ws), lambda bp, j: (first + j, bp)),
                  const(s5d), const(glu_w),
                  tok(NA_HEADS * HEAD_DIM),
                  tok2(SSD_HEADS * HEAD_DIM), tok(W_DZ), const(dnw),
                  const(wg), const(wa), const(wb), const(wc), const(wd), const(wo)],
        out_specs=tok(d),
        out_shape=jax.ShapeDtypeStruct((b, s, d), F32),
        input_output_aliases={0: 0},
        compiler_params=pltpu.CompilerParams(vmem_limit_bytes=VMEM_LIMIT),
        name="merge",
    )(seq, mod, norm_w, hm, mo, mnw, ys, su, s5d, glu_w, yna, yd, dz, dnw, wg, wa, wb, wc, wd, wo)


def _ffn_kernel(x_ref, mod_ref, nw_ref, wi_ref, wo_ref, fw_ref, o_ref, *, final):
    hidden = wo_ref.shape[0]
    h = _stack(lambda i: _norm_mod(x_ref[i], nw_ref[...], mod_ref[i, 4:5, :], mod_ref[i, 3:4, :]))
    h = h.astype(MXU_DTYPE)
    a = jnp.dot(h, wi_ref[:, :hidden], preferred_element_type=F32)
    g = jnp.dot(h, wi_ref[:, hidden:], preferred_element_type=F32)
    out = _mm(_silu(a) * g, wo_ref[...])
    for i in range(NB):
        y = x_ref[i] + mod_ref[i, 5:6, :] * out[i * TM:(i + 1) * TM, :]
        if final:
            y = y * lax.rsqrt(jnp.mean(y * y, axis=-1, keepdims=True) + EPS) * fw_ref[...]
        o_ref[i] = y


def _ffn(seq, mod, norm_w, wi, wo, final_w, final):
    b, s, d = seq.shape
    nt = s // TM
    first = 1 if final else 0

    def const(arr):
        return _const_spec(arr, 2, single=True)

    tok_in = pl.BlockSpec((NB, TM, d), lambda bp, j: (bp, first + j, 0))
    if final:
        out_spec = pl.BlockSpec((NB, TM, d), lambda bp, j: (bp, j, 0))
        out_shape = jax.ShapeDtypeStruct((b, s - TM, d), F32)
        aliases = {}
    else:
        out_spec = tok_in
        out_shape = jax.ShapeDtypeStruct((b, s, d), F32)
        aliases = {0: 0}
    return pl.pallas_call(
        functools.partial(_ffn_kernel, final=final),
        grid=(b // NB, nt - first),
        in_specs=[tok_in, _mod_spec(d, first), const(norm_w), const(wi), const(wo), const(final_w)],
        out_specs=out_spec,
        out_shape=out_shape,
        input_output_aliases=aliases,
        compiler_params=pltpu.CompilerParams(vmem_limit_bytes=VMEM_LIMIT),
        name="ffn_final" if final else "ffn",
    )(seq, mod, norm_w, wi, wo, final_w)


def _rope_tables(t, width):
    nf = HEAD_DIM // 4
    inv = (ROPE_THETA ** (-np.arange(nf, dtype=np.float32) / nf)).astype(np.float32)
    tok = np.arange(t)
    ang_r = (tok // GRID_W).astype(np.float32)[:, None] * inv
    ang_c = (tok % GRID_W).astype(np.float32)[:, None] * inv
    cos_h = np.concatenate([np.cos(ang_r)] * 2 + [np.cos(ang_c)] * 2, axis=1)
    sin_h = np.concatenate([-np.sin(ang_r), np.sin(ang_r), -np.sin(ang_c), np.sin(ang_c)], axis=1)
    cos_x = np.tile(cos_h, (1, width // HEAD_DIM))
    sin_x = np.tile(sin_h, (1, width // HEAD_DIM))
    cos_t = np.concatenate([np.ones((TM, width), np.float32), cos_x], axis=0)
    sin_t = np.concatenate([np.zeros((TM, width), np.float32), sin_x], axis=0)
    scale = np.float32(HEAD_DIM ** -0.5)
    return (jnp.asarray(np.concatenate([cos_t * scale, cos_t], axis=1), F32),
            jnp.asarray(np.concatenate([sin_t * scale, sin_t], axis=1), F32))


def _na_bias_tables(rpb):
    col = jnp.arange(GRID_W)
    col0 = jnp.clip(col - NA_KW // 2, 0, GRID_W - NA_KW)
    in_win = (col[None, :] >= col0[:, None]) & (col[None, :] < col0[:, None] + NA_KW)
    dc = jnp.clip(col[None, :] - col[:, None], -(NA_KW - 1), NA_KW - 1) + (NA_KW - 1)
    per_row = jnp.where(in_win, rpb.astype(F32)[:, :, dc], NEG)
    bias = jnp.stack([per_row[:, off:off + NA_KH] for off in range(NA_KH)], axis=0)
    return bias.transpose(0, 1, 3, 2, 4).reshape(NA_KH, NA_HEADS * GRID_W, NA_KH * GRID_W)


def _block_diag(blocks):
    g, r, c = blocks.shape
    eye = jnp.eye(g, dtype=blocks.dtype)
    return (eye[:, None, :, None] * blocks[:, :, None, :]).reshape(g * r, g * c)


def _lanes(vec, at, width=W_SMALL):
    return jnp.zeros((1, width), F32).at[0, at:at + vec.shape[0]].set(vec.astype(F32))


def _gate_perm():
    perm = np.zeros((2, 2 * W_SMALL, W_SMALL), np.float32)
    for d in range(2):
        for h in range(MLSTM_HEADS):
            perm[d, L_MI + d * MLSTM_HEADS + h, G_LI + h] = 1.0
            perm[d, L_MF + d * MLSTM_HEADS + h, G_BCUM + h] = 1.0
        for h in range(SSD_HEADS):
            perm[d, L_DDT + d * SSD_HEADS + h, G_ACS + h] = 1.0
            perm[d, W_SMALL + L_DDT + d * SSD_HEADS + h, G_DT + h] = 1.0
    return jnp.asarray(perm, MXU_DTYPE)


def _log_decay_perms():
    pa = np.zeros((3, W_SMALL, W_SMALL), np.float32)
    pb = np.zeros((3, W_SMALL, W_SMALL), np.float32)
    ones_ab = np.zeros((2, W_SMALL), np.float32)
    slots = ([(G_BCUM + h, G_LI + h) for h in range(MLSTM_HEADS)]
             + [(G_ACS + h, G_ACS + h) for h in range(SSD_HEADS)])
    for n, (cum_lane, x_lane) in enumerate(slots):
        for k in range(3):
            pa[k, cum_lane, 8 * n + k] = 1.0
            pb[k, x_lane, 8 * n + 3 + k] = 1.0
            ones_ab[0, 8 * n + 3 + k] = 1.0
            ones_ab[1, 8 * n + k] = 1.0
    return jnp.asarray(pa, MXU_DTYPE), jnp.asarray(pb, MXU_DTYPE), jnp.asarray(ones_ab, F32)


def kernel(x, c, ctx, c_ctx, ada_w, ada_b, norm1_w, norm2_w, w_in, mlstm_conv_w, mlstm_conv_b, mlstm_ib, mlstm_fb, mlstm_norm_w, s5_lam_re, s5_lam_im, s5_log_dt, s5_b_re, s5_b_im, s5_c_re, s5_c_im, s5_d, s5_glu_w, na_rpb, ssd_conv_w, ssd_conv_b, ssd_a_log, ssd_dt_bias, ssd_d, ssd_norm_w, w_branch_a, w_branch_b, w_branch_c, w_branch_d, w_out, ffn_w_in, ffn_w_out, final_norm_w):
    b, t, d = x.shape
    depth = w_in.shape[0]
    assert ctx.shape[1] == TM and t % TM == 0 and t % GRID_W == 0 and b % 8 == 0 and b % NB == 0
    assert t // GRID_W >= NA_KH

    seq = jnp.concatenate([ctx, x], axis=1)

    pad = (-(b + 1)) % 8
    cc = jnp.concatenate([c, c_ctx[None, :], jnp.zeros((pad, d), F32)], axis=0)
    mod_all = _adaln(cc, ada_w, ada_b)
    mod_x = mod_all[:, :b].reshape(depth, b, 1, 6, d)
    mod_c = jnp.broadcast_to(mod_all[:, b].reshape(depth, 1, 1, 6, d), (depth, b, 1, 6, d))
    mod = jnp.concatenate([mod_c, mod_x], axis=2)

    cos_t, sin_t = _rope_tables(t, MLSTM_HEADS * HEAD_DIM)
    perm = _gate_perm()
    pa, pb, ones_ab = _log_decay_perms()
    n_state = S5_GROUPS * S5_STATE

    for l in range(depth):
        with_ctx = l < depth - 1
        wl = w_in[l]
        w_small = jnp.concatenate([wl[:, 1024:1040], wl[:, 3600:3616], jnp.zeros((d, W_SMALL - 32), F32)], axis=1)
        w_proj = jnp.concatenate([wl[:, 0:512], wl[:, 2576:3600], wl[:, 512:768], wl[:, 768:1024], w_small,
                                  wl[:, 1040:1296], wl[:, 1296:2064], wl[:, 2064:2576]], axis=1).astype(MXU_DTYPE)
        w_gate = wl[:, 3616:].astype(MXU_DTYPE)

        cv, v, mo, small, su, nqkv, dz = _proj(seq, mod[l], norm1_w[l][None, :], w_proj)

        conv_w = jnp.concatenate([mlstm_conv_w[l], ssd_conv_w[l]], axis=1)
        conv_w = jnp.concatenate([conv_w, jnp.zeros((1, W_CV), F32)], axis=0)
        conv_b = jnp.concatenate([mlstm_conv_b[l], ssd_conv_b[l]])[None, :]
        gbias = (_lanes(mlstm_ib[l].reshape(-1), L_MI) + _lanes(mlstm_fb[l].reshape(-1), L_MF)
                 + _lanes(ssd_dt_bias[l].reshape(-1), L_DDT))
        qk, xbc, gc, gr, la, lb = _prep(cv, small, conv_w, conv_b, cos_t, sin_t, gbias,
                                        _lanes(ssd_a_log[l].reshape(-1), L_DDT), perm, pa, pb, ones_ab)

        hm = _mlstm(qk, v, gc, la, lb)

        braw_re = _block_diag(jnp.swapaxes(s5_b_re[l], 1, 2))
        braw_im = _block_diag(jnp.swapaxes(s5_b_im[l], 1, 2))
        cw = jnp.concatenate([_block_diag(jnp.swapaxes(s5_c_re[l], 1, 2)),
                              -_block_diag(jnp.swapaxes(s5_c_im[l], 1, 2))], axis=0).astype(MXU_DTYPE)
        ys = _s5(su.reshape(-1, W_SU), s5_lam_re[l].reshape(2, 1, n_state), s5_lam_im[l].reshape(2, 1, n_state),
                 jnp.repeat(s5_log_dt[l], S5_STATE, axis=1).reshape(2, 1, n_state), braw_re, braw_im, cw, b)
        ys = ys.reshape(2, -1, b * W_SU)

        yna = _na(nqkv, _na_bias_tables(na_rpb[l]), with_ctx)

        yd = _ssd(xbc, gc, gr, la, lb, jnp.repeat(ssd_d[l], HEAD_DIM)[None, :])

        seq = _merge(seq, mod[l], norm1_w[l][None, :], hm, mo, mlstm_norm_w[l][None, :], ys, su,
                     s5_d[l][None, :], s5_glu_w[l].astype(MXU_DTYPE), yna, yd, dz, ssd_norm_w[l][None, :],
                     w_gate, w_branch_a[l].astype(MXU_DTYPE), w_branch_b[l].astype(MXU_DTYPE),
                     w_branch_c[l].astype(MXU_DTYPE), w_branch_d[l].astype(MXU_DTYPE), w_out[l].astype(MXU_DTYPE),
                     with_ctx)
        seq = _ffn(seq, mod[l], norm2_w[l][None, :], ffn_w_in[l].astype(MXU_DTYPE), ffn_w_out[l].astype(MXU_DTYPE),
                   final_norm_w[None, :], not with_ctx)
    return seq
```

```python
import functools
import math

import jax
import jax.numpy as jnp
import numpy as np
from jax import lax
from jax.experimental import pallas as pl
from jax.experimental.pallas import tpu as pltpu

F32 = jnp.float32
MXU_DTYPE = jnp.bfloat16
HIGHEST = lax.Precision.HIGHEST

GRID_W = 64
EPS = 1e-6
CONV_K = 7
ROPE_THETA = 10000.0
HEAD_DIM = 64
MLSTM_HEADS = 4
S5_GROUPS = 16
S5_GROUP = 16
S5_STATE = 64
NA_HEADS = 4
NA_KH = 8
NA_KW = 16
SSD_HEADS = 8
SSD_GROUPS = 2
SSD_STATE = 128

TM = 256
NB = 2
NBS = 4
SUB = 128
HALO = 8
S5_SUB = 32
NEG = -1e30
VMEM_LIMIT = 56 * 1024 * 1024

W_QK, W_XBC, W_V, W_O, W_SMALL, W_SU, W_NQKV, W_DZ = 512, 1024, 256, 256, 128, 256, 768, 512
W_CV = W_QK + W_XBC
PROJ_WIDTHS = (W_CV, W_V, W_O, W_SMALL, W_SU, W_NQKV, W_DZ)
SU_POS = 4
L_MI, L_MF, L_DDT = 0, 8, 16
G_LI, G_BCUM, G_ACS, G_DT = 0, 4, 8, 16


def _mm(a, b):
    return jnp.dot(a.astype(MXU_DTYPE), b.astype(MXU_DTYPE), preferred_element_type=F32)


def _mm_nt(a, b):
    return lax.dot_general(a.astype(MXU_DTYPE), b.astype(MXU_DTYPE), (((1,), (1,)), ((), ())),
                           preferred_element_type=F32)


def _mm_tn(a, b):
    return lax.dot_general(a.astype(MXU_DTYPE), b.astype(MXU_DTYPE), (((0,), (0,)), ((), ())),
                           preferred_element_type=F32)


def _mm_f32(a, b):
    return jnp.dot(a, b, preferred_element_type=F32, precision=HIGHEST)


def _split3(x):
    hi = x.astype(MXU_DTYPE)
    r1 = x - hi.astype(F32)
    mid = r1.astype(MXU_DTYPE)
    lo = (r1 - mid.astype(F32)).astype(MXU_DTYPE)
    return hi, mid, lo


def _as_01(sel):
    if sel.dtype == jnp.bool_:
        sel = jnp.where(sel, 1.0, 0.0)
    return sel.astype(MXU_DTYPE)


def _mm_exact_rhs(x, sel, terms=3):
    sel = _as_01(sel)
    return sum(jnp.dot(p, sel, preferred_element_type=F32) for p in _split3(x)[:terms])


def _mm_exact_lhs(sel, x):
    sel = _as_01(sel)
    return sum(jnp.dot(sel, p, preferred_element_type=F32) for p in _split3(x))


def _sigmoid(x):
    return 1.0 / (1.0 + jnp.exp(-x))


def _silu(x):
    return x * _sigmoid(x)


def _softplus(x):
    return jnp.maximum(x, 0.0) + jnp.log(1.0 + jnp.exp(-jnp.abs(x)))


def _iota(shape, dim):
    return lax.broadcasted_iota(jnp.int32, shape, dim)


def _head_of_lane(width):
    return jnp.right_shift(_iota((1, width), 1), 6)


def _same_head(rows, cols):
    return jnp.right_shift(_iota((rows, cols), 0), 6) == jnp.right_shift(_iota((rows, cols), 1), 6)


def _expand_heads(cols, width):
    head = _head_of_lane(width)
    out = jnp.broadcast_to(cols[0], (cols[0].shape[0], width))
    for h in range(1, len(cols)):
        out = jnp.where(head == h, cols[h], out)
    return out


def _stack(fn):
    return jnp.concatenate([fn(i) for i in range(NB)], axis=0)


def _round_robin(chains):
    live = list(chains)
    while live:
        for chain in list(live):
            try:
                next(chain)
            except StopIteration:
                live.remove(chain)


def _tile_of(d, j, nt):
    return jnp.where(d == 0, j, jnp.where(j == 0, 0, nt - j))


def _norm_mod(x, w, scale, shift):
    y = x * lax.rsqrt(jnp.mean(x * x, axis=-1, keepdims=True) + EPS) * w
    return y * (1.0 + scale) + shift


def _order_mask(rev, n):
    diff = (_iota((n, n), 1) - _iota((n, n), 0)) * jnp.where(rev, -1, 1)
    return diff <= 0


def _const_spec(arr, n_grid, single=False):
    kwargs = {"pipeline_mode": pl.Buffered(1)} if single else {}
    return pl.BlockSpec(arr.shape, lambda *_: (0,) * arr.ndim, **kwargs)


def _mod_spec(d, first):
    return pl.BlockSpec((NB, None, 6, d), lambda bp, j: (bp, jnp.minimum(first + j, 1), 0, 0))


def _adaln_kernel(c_ref, w_ref, b_ref, o_ref):
    o_ref[...] = _mm_f32(_silu(c_ref[...]), w_ref[...]) + b_ref[...]


def _adaln(cc, ada_w, ada_b):
    depth, d, n = ada_w.shape
    tn = 768
    return pl.pallas_call(
        _adaln_kernel,
        grid=(depth, n // tn),
        in_specs=[pl.BlockSpec(cc.shape, lambda l, i: (0, 0)),
                  pl.BlockSpec((None, d, tn), lambda l, i: (l, 0, i)),
                  pl.BlockSpec((None, 1, tn), lambda l, i: (l, 0, i))],
        out_specs=pl.BlockSpec((None, cc.shape[0], tn), lambda l, i: (l, 0, i)),
        out_shape=jax.ShapeDtypeStruct((depth, cc.shape[0], n), F32),
        name="adaln",
    )(cc, ada_w, ada_b.reshape(depth, 1, n))


def _proj_kernel(x_ref, mod_ref, nw_ref, w_ref, *out_refs):
    h = _stack(lambda i: _norm_mod(x_ref[i], nw_ref[...], mod_ref[i, 1:2, :], mod_ref[i, 0:1, :]))
    h = h.astype(MXU_DTYPE)
    off = 0
    for pos, (ref, n) in enumerate(zip(out_refs, PROJ_WIDTHS)):
        res = jnp.dot(h, w_ref[:, off:off + n], preferred_element_type=F32)
        if pos == SU_POS:
            ref[...] = jnp.concatenate([res[i * TM:(i + 1) * TM, :] for i in range(NB)], axis=1)
        else:
            for i in range(NB):
                ref[i] = res[i * TM:(i + 1) * TM, :]
        off += n


def _proj(seq, mod, norm_w, w_proj):
    b, s, d = seq.shape
    nt = s // TM

    def tok(width):
        return pl.BlockSpec((NB, TM, width), lambda bp, j: (bp, j, 0))

    out_specs = [tok(w) for w in PROJ_WIDTHS]
    out_shape = [jax.ShapeDtypeStruct((b, s, w), F32) for w in PROJ_WIDTHS]
    out_specs[SU_POS] = pl.BlockSpec((TM, NB * W_SU), lambda bp, j: (j, bp))
    out_shape[SU_POS] = jax.ShapeDtypeStruct((s, b * W_SU), F32)
    return pl.pallas_call(
        _proj_kernel,
        grid=(b // NB, nt),
        in_specs=[tok(d), _mod_spec(d, 0), _const_spec(norm_w, 2), _const_spec(w_proj, 2, single=True)],
        out_specs=out_specs,
        out_shape=out_shape,
        compiler_params=pltpu.CompilerParams(vmem_limit_bytes=VMEM_LIMIT),
        name="proj",
    )(seq, mod, norm_w, w_proj)


def _prep_kernel(cv_ref, cvp_ref, cvn_ref, cw_ref, cb_ref, cos_ref, sin_ref, sm_ref, gb_ref, al_ref, perm_ref,
                 pa_ref, pb_ref, one_ref, shift_ref, qk_ref, xbc_ref, gc_ref, gr_ref, la_ref, lb_ref, *, nt):
    t = pl.program_id(1)
    prev_ok = t > 1
    next_ok = (t >= 1) & (t < nt - 1)
    lane = _iota((1, W_SMALL), 1)
    first = jnp.bitwise_and(_iota((1, W_QK), 1), 31) < 16
    r = _iota((TM, TM), 0)
    c = _iota((TM, TM), 1)
    same_chunk = jnp.right_shift(r, 7) == jnp.right_shift(c, 7)
    cum = (lane >= G_BCUM) & (lane < G_DT)

    mid = CONV_K // 2
    side_taps = [k for k in range(CONV_K) if k != mid]

    def edge_rows(ext24):
        out = cb_ref[...] + cw_ref[mid:mid + 1, :] * ext24[HALO:2 * HALO, :]
        for k in side_taps:
            out = out + cw_ref[k:k + 1, :] * pltpu.roll(ext24, (mid - k) % (3 * HALO), 0)[HALO:2 * HALO, :]
        return out

    for i in range(NB):
        cur = cv_ref[i]
        weighted = jnp.concatenate([(cw_ref[k:k + 1, :] * cur).astype(MXU_DTYPE) for k in side_taps], axis=0)
        acc = (cb_ref[...] + cw_ref[mid:mid + 1, :] * cur
               + jnp.dot(shift_ref[...], weighted, preferred_element_type=F32))
        top = edge_rows(jnp.concatenate([jnp.where(prev_ok, cvp_ref[i], 0.0), cur[0:2 * HALO, :]], axis=0))
        bot = edge_rows(jnp.concatenate([cur[TM - 2 * HALO:, :], jnp.where(next_ok, cvn_ref[i], 0.0)], axis=0))
        acc = _silu(jnp.concatenate([top, acc[HALO:TM - HALO, :], bot], axis=0))
        xbc_ref[i] = acc[:, W_QK:]
        qk = acc[:, :W_QK]
        partner = jnp.where(first, pltpu.roll(qk, W_QK - 16, 1), pltpu.roll(qk, 16, 1))
        qk_ref[i] = qk * cos_ref[...] + partner * sin_ref[...]

        g_all = sm_ref[i] + gb_ref[...]
        dt = _softplus(g_all)
        src = jnp.where(lane < L_MF, g_all,
                        jnp.where(lane < L_DDT, -_softplus(-g_all), dt * (-jnp.exp(al_ref[...]))))
        src = jnp.concatenate([src, dt], axis=1)
        for d in range(2):
            tri = same_chunk & ((c <= r) if d == 0 else (c >= r))
            g = _mm_exact_rhs(src, perm_ref[d])
            g = jnp.where(cum, _mm_exact_lhs(tri, jnp.where(cum, g, 0.0)), g)
            gc_ref[d, i] = g
            a_mat = one_ref[0:1, :] + sum(jnp.dot(p, pa_ref[k], preferred_element_type=F32)
                                           for k, p in enumerate(_split3(g)))
            cv = jnp.where(lane < MLSTM_HEADS, g - pltpu.roll(g, W_SMALL - G_BCUM, 1), -g)
            b_mat = one_ref[1:2, :] + sum(jnp.dot(p, pb_ref[k], preferred_element_type=F32)
                                           for k, p in enumerate(_split3(cv)))
            la_ref[d, i] = a_mat.astype(MXU_DTYPE)
            for ci in range(TM // SUB):
                gr_ref[d, i, ci] = g[ci * SUB:(ci + 1) * SUB, :].T
                lb_ref[d, i, ci] = b_mat[ci * SUB:(ci + 1) * SUB, :].T.astype(MXU_DTYPE)


def _prep(cv, small, conv_w, conv_b, cos_t, sin_t, gbias, alog, perm, pa, pb, ones_ab, shifts):
    b, s, _ = cv.shape
    nt = s // TM
    per = TM // HALO
    last = s // HALO - 1
    nsub = TM // SUB
    return pl.pallas_call(
        functools.partial(_prep_kernel, nt=nt),
        grid=(b // NB, nt),
        in_specs=[pl.BlockSpec((NB, TM, W_CV), lambda bp, j: (bp, j, 0)),
                  pl.BlockSpec((NB, HALO, W_CV), lambda bp, j: (bp, jnp.maximum(j * per - 1, 0), 0)),
                  pl.BlockSpec((NB, HALO, W_CV), lambda bp, j: (bp, jnp.minimum((j + 1) * per, last), 0)),
                  _const_spec(conv_w, 2), _const_spec(conv_b, 2),
                  pl.BlockSpec((TM, W_QK), lambda bp, j: (j, 0)),
                  pl.BlockSpec((TM, W_QK), lambda bp, j: (j, 0)),
                  pl.BlockSpec((NB, TM, W_SMALL), lambda bp, j: (bp, j, 0)),
                  _const_spec(gbias, 2), _const_spec(alog, 2), _const_spec(perm, 2),
                  _const_spec(pa, 2), _const_spec(pb, 2), _const_spec(ones_ab, 2), _const_spec(shifts, 2)],
        out_specs=[pl.BlockSpec((NB, TM, W_QK), lambda bp, j: (bp, j, 0)),
                   pl.BlockSpec((NB, TM, W_XBC), lambda bp, j: (bp, j, 0)),
                   pl.BlockSpec((2, NB, TM, W_SMALL), lambda bp, j: (0, bp, j, 0)),
                   pl.BlockSpec((2, NB, nsub, W_SMALL, SUB), lambda bp, j: (0, bp, j, 0, 0)),
                   pl.BlockSpec((2, NB, TM, W_SMALL), lambda bp, j: (0, bp, j, 0)),
                   pl.BlockSpec((2, NB, nsub, W_SMALL, SUB), lambda bp, j: (0, bp, j, 0, 0))],
        out_shape=[jax.ShapeDtypeStruct((b, s, W_QK), F32),
                   jax.ShapeDtypeStruct((b, s, W_XBC), F32),
                   jax.ShapeDtypeStruct((2, b, s, W_SMALL), F32),
                   jax.ShapeDtypeStruct((2, b, s // SUB, W_SMALL, SUB), F32),
                   jax.ShapeDtypeStruct((2, b, s, W_SMALL), MXU_DTYPE),
                   jax.ShapeDtypeStruct((2, b, s // SUB, W_SMALL, SUB), MXU_DTYPE)],
        compiler_params=pltpu.CompilerParams(vmem_limit_bytes=VMEM_LIMIT),
        name="prep",
    )(cv, cv, cv, conv_w, conv_b, cos_t, sin_t, small, gbias, alog, perm, pa, pb, ones_ab, shifts)


def _scan_specs(nt, widths, kinds):
    nsub = TM // SUB
    specs = [pl.BlockSpec((NBS, TM, w), lambda d, j, bp: (bp, _tile_of(d, j, nt), 0)) for w in widths]
    for kind in kinds:
        if kind == "col":
            specs.append(pl.BlockSpec((None, NBS, TM, W_SMALL), lambda d, j, bp: (d, bp, _tile_of(d, j, nt), 0)))
        else:
            specs.append(pl.BlockSpec((None, NBS, nsub, W_SMALL, SUB),
                                      lambda d, j, bp: (d, bp, _tile_of(d, j, nt), 0, 0)))
    return specs


def _mlstm_kernel(qk_ref, v_ref, gc_ref, la_ref, lb_ref, h_ref, c_scr, n_scr, m_scr):
    d = pl.program_id(0)
    j = pl.program_id(1)
    bp = pl.program_id(2)
    rev = d == 1
    width = MLSTM_HEADS * HEAD_DIM
    nsub = TM // SUB
    heads = list(range(MLSTM_HEADS))
    stacked = (MLSTM_HEADS * SUB, SUB)
    t_minus_s = (jnp.bitwise_and(_iota(stacked, 0), SUB - 1) - _iota(stacked, 1)) * jnp.where(rev, -1, 1)
    mask_s = t_minus_s >= 0
    head = _head_of_lane(width)
    same_head = _same_head(width, width)
    lane = _iota((1, W_SMALL), 1)
    valid = lane < MLSTM_HEADS
    group = jnp.right_shift(lane, 3)
    to_heads = _as_01(jnp.right_shift(_iota((W_SMALL, width), 1), 6) == _iota((W_SMALL, width), 0))
    from_heads = _as_01(jnp.right_shift(_iota((width, W_SMALL), 0), 6) == _iota((width, W_SMALL), 1))
    ones_rows = jnp.ones((8, SUB), MXU_DTYPE)

    @pl.when(j == 0)
    def _():
        for i in range(NBS):
            c_scr[bp * NBS + i] = jnp.zeros(c_scr.shape[1:], F32)
            n_scr[bp * NBS + i] = jnp.zeros(n_scr.shape[1:], F32)
            m_scr[bp * NBS + i] = jnp.zeros(m_scr.shape[1:], F32)

    def chain(i):
        bi = bp * NBS + i
        c_st = c_scr[bi]
        n_st = n_scr[bi, 0:1, :]
        m_st = m_scr[bi, 0:1, :]
        chunks = []
        for step in range(nsub):
            ci = jnp.where(rev, nsub - 1 - step, step)
            rows = pl.ds(pl.multiple_of(ci * SUB, SUB), SUB)
            q = qk_ref[i, rows, 0:width]
            k = qk_ref[i, rows, width:2 * width]
            qb = q.astype(MXU_DTYPE)
            kb = k.astype(MXU_DTYPE)
            vb = v_ref[i, rows, :].astype(MXU_DTYPE)
            li = gc_ref[i, rows, :]
            bc = pltpu.roll(li, W_SMALL - G_BCUM, 1)
            la = la_ref[i, rows, :]
            lb = lb_ref[i, ci]

            la_s = jnp.concatenate([jnp.where(group == h, la, 0) for h in heads], axis=0)
            tiles = [slice((h // 2) * 128, (h // 2 + 1) * 128) for h in heads]
            q_s = [jnp.where(head[:, tiles[h]] == h, qb[:, tiles[h]], 0) for h in heads]
            log_w = jnp.where(mask_s, jnp.dot(la_s, lb, preferred_element_type=F32), NEG)
            yield
            m_in_s = jnp.max(log_w, axis=1, keepdims=True)
            qk = jnp.concatenate([_mm_nt(jnp.concatenate(q_s[2 * p:2 * p + 2], axis=0), kb[:, tiles[2 * p]])
                                  for p in range(MLSTM_HEADS // 2)], axis=0)
            sm = qk * jnp.exp(log_w - m_in_s)
            rs = jnp.sum(sm, axis=1, keepdims=True)
            sm = sm.astype(MXU_DTYPE)
            yield
            pv = jnp.dot(sm, vb, preferred_element_type=F32)
            yield
            num0 = pv[0:SUB, :]
            den0 = jnp.broadcast_to(rs[0:SUB, :], (SUB, W_SMALL))
            m_in = jnp.broadcast_to(m_in_s[0:SUB, :], (SUB, W_SMALL))
            for h in heads[1:]:
                blk = slice(h * SUB, (h + 1) * SUB)
                num0 = jnp.where(head == h, pv[blk, :], num0)
                den0 = jnp.where(lane == h, rs[blk, :], den0)
                m_in = jnp.where(lane == h, m_in_s[blk, :], m_in)
            b_last = jnp.where(rev, bc[0:1, :], bc[SUB - 1:SUB, :])
            log_k = b_last - bc + li
            mk = jnp.max(log_k, axis=0, keepdims=True)
            wk_full = _mm_exact_rhs(jnp.where(valid, jnp.exp(log_k - mk), 0.0), to_heads, terms=2)
            kw = (k * wk_full).astype(MXU_DTYPE)
            yield
            kv0 = jnp.where(same_head, _mm_tn(kw, vb), 0.0)
            ks0 = jnp.dot(ones_rows, kw, preferred_element_type=F32)[0:1, :]
            chunks.append((rows, q, qb, bc, num0, den0, m_in, b_last, mk, kv0, ks0))
            yield

        for rows, q, qb, bc, num0, den0, m_in, b_last, mk, kv0, ks0 in chunks:
            inter = bc + m_st
            m_t = jnp.maximum(inter, m_in)
            r = jnp.exp(m_in - m_t)
            g = jnp.exp(inter - m_t)
            den = r * den0 + g * jnp.dot((q * n_st).astype(MXU_DTYPE), from_heads, preferred_element_type=F32)
            inv = 1.0 / jnp.maximum(jnp.abs(den), jnp.exp(-m_t))
            ir_full = _mm_exact_rhs(jnp.where(valid, inv * r, 0.0), to_heads, terms=2)
            ig_full = _mm_exact_rhs(jnp.where(valid, inv * g, 0.0), to_heads, terms=2)
            h_ref[i, rows, :] = num0 * ir_full + _mm(qb, c_st) * ig_full
            yield
            m_new = jnp.maximum(b_last + m_st, mk)
            scales = jnp.concatenate([jnp.exp(b_last + m_st - m_new), jnp.exp(mk - m_new),
                                      jnp.zeros((6, W_SMALL), F32)], axis=0)
            scales_full = _mm_exact_rhs(jnp.where(valid, scales, 0.0), to_heads)
            c_st = c_st * scales_full[0:1, :] + kv0 * scales_full[1:2, :]
            n_st = n_st * scales_full[0:1, :] + ks0 * scales_full[1:2, :]
            m_st = jnp.where(valid, m_new, 0.0)
            yield

        c_scr[bi] = c_st
        n_scr[bi, 0:1, :] = n_st
        m_scr[bi, 0:1, :] = m_st

    _round_robin([chain(i) for i in range(NBS)])


def _mlstm(qk, v, gc, la, lb):
    b, s, _ = qk.shape
    nt = s // TM
    width = MLSTM_HEADS * HEAD_DIM
    return pl.pallas_call(
        _mlstm_kernel,
        grid=(2, nt, b // NBS),
        in_specs=_scan_specs(nt, (2 * width, width), ("col", "col", "row")),
        out_specs=pl.BlockSpec((None, NBS, TM, width), lambda d, j, bp: (d, bp, _tile_of(d, j, nt), 0)),
        out_shape=jax.ShapeDtypeStruct((2, b, s, width), F32),
        scratch_shapes=[pltpu.VMEM((b, width, width), F32), pltpu.VMEM((b, 8, width), F32),
                        pltpu.VMEM((b, 8, W_SMALL), F32)],
        compiler_params=pltpu.CompilerParams(vmem_limit_bytes=VMEM_LIMIT),
        name="mlstm",
    )(qk, v, gc, la, lb)


def _s5_kernel(u_ref, lre_ref, lim_ref, ldt_ref, bre_ref, bim_ref, cw_ref, y_ref,
               wbu_scr, are_scr, aim_scr, hre_scr, him_scr, hbuf, *, nb):
    d = pl.program_id(0)
    j = pl.program_id(1)
    rev = d == 1
    n_state = S5_GROUPS * S5_STATE

    @pl.when(j == 0)
    def _():
        lre = lre_ref[...]
        lim = lim_ref[...]
        dt = jnp.exp(ldt_ref[...])
        mag = jnp.exp(lre * dt)
        a_re = mag * jnp.cos(lim * dt)
        a_im = mag * jnp.sin(lim * dt)
        den = lre * lre + lim * lim
        nr = a_re - 1.0
        coef_re = (nr * lre + a_im * lim) / den
        coef_im = (a_im * lre - nr * lim) / den
        wbu_scr[:, :n_state] = (coef_re * bre_ref[...] - coef_im * bim_ref[...]).astype(MXU_DTYPE)
        wbu_scr[:, n_state:] = (coef_re * bim_ref[...] + coef_im * bre_ref[...]).astype(MXU_DTYPE)
        are_scr[...] = jnp.broadcast_to(a_re, are_scr.shape)
        aim_scr[...] = jnp.broadcast_to(a_im, aim_scr.shape)
        hre_scr[...] = jnp.zeros(hre_scr.shape, F32)
        him_scr[...] = jnp.zeros(him_scr.shape, F32)

    a_re = are_scr[...]
    a_im = aim_scr[...]
    rows_sub = S5_SUB * nb
    n_sub = TM // S5_SUB
    carry = (hre_scr[...], him_scr[...])
    for sb in range(n_sub):
        sbi = jnp.where(rev, n_sub - 1 - sb, sb)
        rows = pl.ds(pl.multiple_of(sbi * rows_sub, rows_sub), rows_sub)
        hbuf[...] = jnp.dot(u_ref[rows, :].astype(MXU_DTYPE), wbu_scr[...], preferred_element_type=F32)

        def step(i, hc):
            ti = jnp.where(rev, S5_SUB - 1 - i, i)
            r = pl.ds(pl.multiple_of(ti * nb, nb), nb)
            h_re, h_im = hc
            n_re = a_re * h_re - a_im * h_im + hbuf[r, :n_state]
            n_im = a_re * h_im + a_im * h_re + hbuf[r, n_state:]
            hbuf[r, :n_state] = n_re
            hbuf[r, n_state:] = n_im
            return n_re, n_im

        carry = lax.fori_loop(0, S5_SUB, step, carry)
        y_ref[rows, :] = jnp.dot(hbuf[...].astype(MXU_DTYPE), cw_ref[...], preferred_element_type=F32)
    hre_scr[...] = carry[0]
    him_scr[...] = carry[1]


def _s5(u_tm, lam_re, lam_im, log_dt, braw_re, braw_im, cw, nb):
    rows, width = u_tm.shape
    s = rows // nb
    nt = s // TM
    n_state = S5_GROUPS * S5_STATE

    def per_dir():
        return pl.BlockSpec((None, 1, n_state), lambda d, j: (d, 0, 0))

    return pl.pallas_call(
        functools.partial(_s5_kernel, nb=nb),
        grid=(2, nt),
        in_specs=[pl.BlockSpec((TM * nb, width), lambda d, j: (_tile_of(d, j, nt), 0)),
                  per_dir(), per_dir(), per_dir(),
                  _const_spec(braw_re, 2), _const_spec(braw_im, 2), _const_spec(cw, 2)],
        out_specs=pl.BlockSpec((None, TM * nb, width), lambda d, j: (d, _tile_of(d, j, nt), 0)),
        out_shape=jax.ShapeDtypeStruct((2, rows, width), F32),
        scratch_shapes=[pltpu.VMEM((width, 2 * n_state), MXU_DTYPE),
                        pltpu.VMEM((nb, n_state), F32), pltpu.VMEM((nb, n_state), F32),
                        pltpu.VMEM((nb, n_state), F32), pltpu.VMEM((nb, n_state), F32),
                        pltpu.VMEM((S5_SUB * nb, 2 * n_state), F32)],
        compiler_params=pltpu.CompilerParams(vmem_limit_bytes=VMEM_LIMIT),
        name="s5",
    )(u_tm, lam_re, lam_im, log_dt, braw_re, braw_im, cw)


def _na_kernel(q_ref, k_ref, v_ref, tbl_ref, o_ref, *, first_tile, n_rows):
    t = first_tile + pl.program_id(1)
    width = NA_HEADS * HEAD_DIM
    head = _head_of_lane(width)
    scale = HEAD_DIM ** -0.5
    k_ctx = k_ref[0:TM, :].astype(MXU_DTYPE)
    v_ctx = v_ref[0:TM, :].astype(MXU_DTYPE)

    def stack_heads(q):
        return jnp.concatenate([jnp.where(head == h, q, 0.0) for h in range(NA_HEADS)], axis=0).astype(MXU_DTYPE)

    def unstack_heads(o, n):
        acc = o[0:n, :]
        for h in range(1, NA_HEADS):
            acc = jnp.where(head == h, o[h * n:(h + 1) * n, :], acc)
        return acc

    @pl.when(t == 0)
    def _():
        s = _mm_nt(stack_heads(q_ref[...] * scale), k_ctx)
        p = jnp.exp(s - jnp.max(s, axis=1, keepdims=True))
        o = _mm(p, v_ctx) / jnp.sum(p, axis=1, keepdims=True)
        o_ref[...] = unstack_heads(o, TM)

    @pl.when(t > 0)
    def _():
        rows_per_tile = TM // GRID_W
        n_lat = NA_KH * GRID_W
        def grid_row(rr):
            r = (t - 1) * rows_per_tile + rr
            row_start = jnp.clip(r - NA_KH // 2, 0, n_rows - NA_KH)
            off = (NA_KH - 1) - (r - row_start)
            win = pl.ds(pl.multiple_of(TM + row_start * GRID_W, GRID_W), n_lat)
            qs = stack_heads(q_ref[rr * GRID_W:(rr + 1) * GRID_W, :] * scale)
            s_lat = _mm_nt(qs, k_ref[win, :]) + tbl_ref[off]
            s_ctx = _mm_nt(qs, k_ctx)
            yield
            m = jnp.maximum(jnp.max(s_lat, axis=1, keepdims=True), jnp.max(s_ctx, axis=1, keepdims=True))
            p_lat = jnp.exp(s_lat - m)
            p_ctx = jnp.exp(s_ctx - m)
            den = jnp.sum(p_lat, axis=1, keepdims=True) + jnp.sum(p_ctx, axis=1, keepdims=True)
            yield
            o = (_mm(p_lat, v_ref[win, :]) + _mm(p_ctx, v_ctx)) / den
            o_ref[rr * GRID_W:(rr + 1) * GRID_W, :] = unstack_heads(o, GRID_W)
            yield

        _round_robin([grid_row(rr) for rr in range(rows_per_tile)])


def _na(nqkv, tbl, with_ctx):
    b, s, _ = nqkv.shape
    nt = s // TM
    width = NA_HEADS * HEAD_DIM
    first_tile = 0 if with_ctx else 1
    n_rows = (s - TM) // GRID_W
    return pl.pallas_call(
        functools.partial(_na_kernel, first_tile=first_tile, n_rows=n_rows),
        grid=(b, nt - first_tile),
        in_specs=[pl.BlockSpec((None, TM, width), lambda bi, j: (bi, first_tile + j, 0)),
                  pl.BlockSpec((None, s, width), lambda bi, j: (bi, 0, 1)),
                  pl.BlockSpec((None, s, width), lambda bi, j: (bi, 0, 2)),
                  _const_spec(tbl, 2)],
        out_specs=pl.BlockSpec((None, TM, width), lambda bi, j: (bi, first_tile + j, 0)),
        out_shape=jax.ShapeDtypeStruct((b, s, width), F32),
        compiler_params=pltpu.CompilerParams(vmem_limit_bytes=VMEM_LIMIT),
        name="na",
    )(nqkv, nqkv, nqkv, tbl)


def _ssd_kernel(x_ref, gc_ref, gr_ref, la_ref, lb_ref, dsk_ref, y_ref, st_scr):
    d = pl.program_id(0)
    j = pl.program_id(1)
    bp = pl.program_id(2)
    rev = d == 1
    width = SSD_HEADS * HEAD_DIM
    gn = SSD_GROUPS * SSD_STATE
    per_group = SSD_HEADS // SSD_GROUPS
    gw = per_group * HEAD_DIM
    nsub = TM // SUB
    stacked = (per_group * SUB, SUB)
    t_minus_s = (jnp.bitwise_and(_iota(stacked, 0), SUB - 1) - _iota(stacked, 1)) * jnp.where(rev, -1, 1)
    mask_s = t_minus_s >= 0
    head = _head_of_lane(width)
    skip = jnp.where(rev, 0.0, 1.0) * dsk_ref[...]
    lane = _iota((1, W_SMALL), 1)
    valid = (lane >= G_ACS) & (lane < G_ACS + SSD_HEADS)
    group = jnp.right_shift(lane, 3)
    to_heads = _as_01(jnp.right_shift(_iota((W_SMALL, width), 1), 6) == _iota((W_SMALL, width), 0) - G_ACS)

    @pl.when(j == 0)
    def _():
        for i in range(NBS):
            st_scr[bp * NBS + i] = jnp.zeros(st_scr.shape[1:], F32)

    def chain(i):
        bi = bp * NBS + i
        st = st_scr[bi]
        chunks = []
        for step in range(nsub):
            ci = jnp.where(rev, nsub - 1 - step, step)
            rows = pl.ds(pl.multiple_of(ci * SUB, SUB), SUB)
            xs = x_ref[i, rows, 0:width]
            bm = x_ref[i, rows, width:width + gn].astype(MXU_DTYPE)
            cm = x_ref[i, rows, width + gn:width + 2 * gn].astype(MXU_DTYPE)
            xsb = xs.astype(MXU_DTYPE)
            acs = gc_ref[i, rows, :]
            dt = pltpu.roll(acs, W_SMALL - (G_DT - G_ACS), 1)
            grow = gr_ref[i, ci]
            la = la_ref[i, rows, :]
            lb = lb_ref[i, ci]

            a_last = jnp.where(rev, acs[0:1, :], acs[SUB - 1:SUB, :])
            e_full = _mm_exact_rhs(jnp.where(valid, jnp.exp(acs), 0.0), to_heads, terms=2)
            xw = xs * _mm_exact_rhs(jnp.where(valid, jnp.exp(a_last - acs) * dt, 0.0), to_heads, terms=2)
            last = jnp.broadcast_to(jnp.where(valid, jnp.exp(a_last), 0.0), (8, W_SMALL))
            last_full = _mm_exact_rhs(last, to_heads)[0:1, :]
            yield

            y_parts, st_parts, c_parts = [], [], []
            for g in range(SSD_GROUPS):
                b_g = bm[:, g * SSD_STATE:(g + 1) * SSD_STATE]
                c_g = cm[:, g * SSD_STATE:(g + 1) * SSD_STATE]
                lanes = slice(g * gw, (g + 1) * gw)
                hs = range(g * per_group, (g + 1) * per_group)
                la_s = jnp.concatenate([jnp.where(group == MLSTM_HEADS + h, la, 0) for h in hs], axis=0)
                dt_s = jnp.concatenate([jnp.broadcast_to(grow[G_DT + h:G_DT + h + 1, :], (SUB, SUB)) for h in hs],
                                       axis=0)
                decay = jnp.exp(jnp.where(mask_s, jnp.dot(la_s, lb, preferred_element_type=F32), NEG)) * dt_s
                cb = _mm_nt(c_g, b_g)
                m_s = (jnp.concatenate([cb] * per_group, axis=0) * decay).astype(MXU_DTYPE)
                yield
                yd = jnp.dot(m_s, xsb[:, lanes], preferred_element_type=F32)
                y_g = yd[0:SUB, :]
                for hh in range(1, per_group):
                    y_g = jnp.where(head[:, lanes] == g * per_group + hh, yd[hh * SUB:(hh + 1) * SUB, :], y_g)
                y_parts.append(y_g)
                st_parts.append(_mm_tn(b_g, xw[:, lanes]))
                c_parts.append(c_g)
                yield
            chunks.append((rows, jnp.concatenate(y_parts, axis=1) + skip * xs, jnp.concatenate(st_parts, axis=1),
                           c_parts, e_full, last_full))

        for rows, y_in, st_in, c_parts, e_full, last_full in chunks:
            y_off = jnp.concatenate([_mm(c_parts[g], st[:, g * gw:(g + 1) * gw]) for g in range(SSD_GROUPS)], axis=1)
            y_ref[i, rows, :] = y_in + y_off * e_full
            st = st * last_full + st_in
            yield
        st_scr[bi] = st

    _round_robin([chain(i) for i in range(NBS)])


def _ssd(xbc, gc, gr, la, lb, dskip):
    b, s, cw = xbc.shape
    nt = s // TM
    width = SSD_HEADS * HEAD_DIM
    return pl.pallas_call(
        _ssd_kernel,
        grid=(2, nt, b // NBS),
        in_specs=_scan_specs(nt, (cw,), ("col", "row", "col", "row")) + [_const_spec(dskip, 3)],
        out_specs=pl.BlockSpec((None, NBS, TM, width), lambda d, j, bp: (d, bp, _tile_of(d, j, nt), 0)),
        out_shape=jax.ShapeDtypeStruct((2, b, s, width), F32),
        scratch_shapes=[pltpu.VMEM((b, SSD_STATE, width), F32)],
        compiler_params=pltpu.CompilerParams(vmem_limit_bytes=VMEM_LIMIT),
        name="ssd",
    )(xbc, gc, gr, la, lb, dskip)


def _merge_kernel(x_ref, mod_ref, nw_ref, hm_ref, mo_ref, mnw_ref, ys_ref, su_ref, s5d_ref, glu_ref, na_ref,
                  yd_ref, dz_ref, dnw_ref, wg_ref, wa_ref, wb_ref, wc_ref, wd_ref, wo_ref, o_ref):
    d_model = x_ref.shape[2]
    h = _stack(lambda i: _norm_mod(x_ref[i], nw_ref[...], mod_ref[i, 1:2, :], mod_ref[i, 0:1, :]))
    h = h.astype(MXU_DTYPE)

    wm = MLSTM_HEADS * HEAD_DIM
    hm = _stack(lambda i: (hm_ref[0, i] + hm_ref[1, i]) * _sigmoid(mo_ref[i]))
    ms = _mm_exact_rhs(hm * hm, _same_head(wm, wm)) * (1.0 / HEAD_DIM)
    ya = hm * lax.rsqrt(ms + EPS) * mnw_ref[...]

    ws = S5_GROUPS * S5_GROUP
    ys = _stack(lambda i: (ys_ref[0, :, i * ws:(i + 1) * ws] + ys_ref[1, :, i * ws:(i + 1) * ws]
                           + s5d_ref[...] * su_ref[:, i * ws:(i + 1) * ws]))
    ys = 0.5 * ys * (1.0 + jnp.tanh(math.sqrt(2.0 / math.pi) * (ys + 0.044715 * (ys * ys * ys))))
    ab = _mm(ys, glu_ref[...])
    yb = ab[:, :ws] * _sigmoid(ab[:, ws:])

    yc = _stack(lambda i: na_ref[i])

    yd = _stack(lambda i: (yd_ref[0, i] + yd_ref[1, i]) * _silu(dz_ref[i]))
    yd = yd * lax.rsqrt(jnp.mean(yd * yd, axis=-1, keepdims=True) + EPS) * dnw_ref[...]

    m = None
    for k, (y, w_ref) in enumerate(((ya, wa_ref), (yb, wb_ref), (yc, wc_ref), (yd, wd_ref))):
        gate = _sigmoid(jnp.dot(h, wg_ref[:, k * d_model:(k + 1) * d_model], preferred_element_type=F32))
        term = gate * _mm(y, w_ref[...])
        m = term if m is None else m + term
    out = _mm(m, wo_ref[...])
    for i in range(NB):
        o_ref[i] = x_ref[i] + mod_ref[i, 2:3, :] * out[i * TM:(i + 1) * TM, :]


def _merge(seq, mod, norm_w, hm, mo, mnw, ys, su, s5d, glu_w, yna, yd, dz, dnw, wg, wa, wb, wc, wd, wo,
           with_ctx):
    b, s, d = seq.shape
    nt = s // TM
    first = 0 if with_ctx else 1

    def tok(width):
        return pl.BlockSpec((NB, TM, width), lambda bp, j: (bp, first + j, 0))

    def tok2(width):
        return pl.BlockSpec((2, NB, TM, width), lambda bp, j: (0, bp, first + j, 0))

    def const(arr):
        return _const_spec(arr, 2, single=True)

    ws = S5_GROUPS * S5_GROUP
    return pl.pallas_call(
        _merge_kernel,
        grid=(b // NB, nt - first),
        in_specs=[tok(d), _mod_spec(d, first), const(norm_w),
                  tok2(MLSTM_HEADS * HEAD_DIM), tok(W_O), const(mnw),
                  pl.BlockSpec((2, TM, NB * ws), lambda bp, j: (0, first + j, bp)),
                  pl.BlockSpec((TM, NB * ws), lambda bp, j: (first + j, bp)),
                  const(s5d), const(glu_w),
                  tok(NA_HEADS * HEAD_DIM),
                  tok2(SSD_HEADS * HEAD_DIM), tok(W_DZ), const(dnw),
                  const(wg), const(wa), const(wb), const(wc), const(wd), const(wo)],
        out_specs=tok(d),
        out_shape=jax.ShapeDtypeStruct((b, s, d), F32),
        input_output_aliases={0: 0},
        compiler_params=pltpu.CompilerParams(vmem_limit_bytes=VMEM_LIMIT),
        name="merge",
    )(seq, mod, norm_w, hm, mo, mnw, ys, su, s5d, glu_w, yna, yd, dz, dnw, wg, wa, wb, wc, wd, wo)


def _ffn_kernel(x_ref, mod_ref, nw_ref, wi_ref, wo_ref, fw_ref, o_ref, *, final):
    hidden = wo_ref.shape[0]
    h = _stack(lambda i: _norm_mod(x_ref[i], nw_ref[...], mod_ref[i, 4:5, :], mod_ref[i, 3:4, :]))
    h = h.astype(MXU_DTYPE)
    a = jnp.dot(h, wi_ref[:, :hidden], preferred_element_type=F32)
    g = jnp.dot(h, wi_ref[:, hidden:], preferred_element_type=F32)
    out = _mm(_silu(a) * g, wo_ref[...])
    for i in range(NB):
        y = x_ref[i] + mod_ref[i, 5:6, :] * out[i * TM:(i + 1) * TM, :]
        if final:
            y = y * lax.rsqrt(jnp.mean(y * y, axis=-1, keepdims=True) + EPS) * fw_ref[...]
        o_ref[i] = y


def _ffn(seq, mod, norm_w, wi, wo, final_w, final):
    b, s, d = seq.shape
    nt = s // TM
    first = 1 if final else 0

    def const(arr):
        return _const_spec(arr, 2, single=True)

    tok_in = pl.BlockSpec((NB, TM, d), lambda bp, j: (bp, first + j, 0))
    if final:
        out_spec = pl.BlockSpec((NB, TM, d), lambda bp, j: (bp, j, 0))
        out_shape = jax.ShapeDtypeStruct((b, s - TM, d), F32)
        aliases = {}
    else:
        out_spec = tok_in
        out_shape = jax.ShapeDtypeStruct((b, s, d), F32)
        aliases = {0: 0}
    return pl.pallas_call(
        functools.partial(_ffn_kernel, final=final),
        grid=(b // NB, nt - first),
        in_specs=[tok_in, _mod_spec(d, first), const(norm_w), const(wi), const(wo), const(final_w)],
        out_specs=out_spec,
        out_shape=out_shape,
        input_output_aliases=aliases,
        compiler_params=pltpu.CompilerParams(vmem_limit_bytes=VMEM_LIMIT),
        name="ffn_final" if final else "ffn",
    )(seq, mod, norm_w, wi, wo, final_w)


def _rope_tables(t, width):
    nf = HEAD_DIM // 4
    inv = (ROPE_THETA ** (-np.arange(nf, dtype=np.float32) / nf)).astype(np.float32)
    tok = np.arange(t)
    ang_r = (tok // GRID_W).astype(np.float32)[:, None] * inv
    ang_c = (tok % GRID_W).astype(np.float32)[:, None] * inv
    cos_h = np.concatenate([np.cos(ang_r)] * 2 + [np.cos(ang_c)] * 2, axis=1)
    sin_h = np.concatenate([-np.sin(ang_r), np.sin(ang_r), -np.sin(ang_c), np.sin(ang_c)], axis=1)
    cos_x = np.tile(cos_h, (1, width // HEAD_DIM))
    sin_x = np.tile(sin_h, (1, width // HEAD_DIM))
    cos_t = np.concatenate([np.ones((TM, width), np.float32), cos_x], axis=0)
    sin_t = np.concatenate([np.zeros((TM, width), np.float32), sin_x], axis=0)
    scale = np.float32(HEAD_DIM ** -0.5)
    return (jnp.asarray(np.concatenate([cos_t * scale, cos_t], axis=1), F32),
            jnp.asarray(np.concatenate([sin_t * scale, sin_t], axis=1), F32))


def _na_bias_tables(rpb):
    col = np.arange(GRID_W)
    col0 = np.clip(col - NA_KW // 2, 0, GRID_W - NA_KW)
    in_win = (col[None, :] >= col0[:, None]) & (col[None, :] < col0[:, None] + NA_KW)
    dc = np.clip(col[None, :] - col[:, None], -(NA_KW - 1), NA_KW - 1) + (NA_KW - 1)
    pick = jnp.asarray(dc[None] == np.arange(2 * NA_KW - 1)[:, None, None], F32)
    per_row = jnp.einsum('hrd,dqk->hrqk', rpb.astype(F32), pick, precision=HIGHEST)
    per_row = jnp.where(in_win, per_row, NEG)
    bias = jnp.stack([per_row[:, off:off + NA_KH] for off in range(NA_KH)], axis=0)
    return bias.transpose(0, 1, 3, 2, 4).reshape(NA_KH, NA_HEADS * GRID_W, NA_KH * GRID_W)


def _block_diag(blocks):
    g, r, c = blocks.shape
    eye = jnp.eye(g, dtype=blocks.dtype)
    return (eye[:, None, :, None] * blocks[:, :, None, :]).reshape(g * r, g * c)


def _lanes(vec, at, width=W_SMALL):
    return jnp.zeros((1, width), F32).at[0, at:at + vec.shape[0]].set(vec.astype(F32))


def _gate_perm():
    perm = np.zeros((2, 2 * W_SMALL, W_SMALL), np.float32)
    for d in range(2):
        for h in range(MLSTM_HEADS):
            perm[d, L_MI + d * MLSTM_HEADS + h, G_LI + h] = 1.0
            perm[d, L_MF + d * MLSTM_HEADS + h, G_BCUM + h] = 1.0
        for h in range(SSD_HEADS):
            perm[d, L_DDT + d * SSD_HEADS + h, G_ACS + h] = 1.0
            perm[d, W_SMALL + L_DDT + d * SSD_HEADS + h, G_DT + h] = 1.0
    return jnp.asarray(perm, MXU_DTYPE)


def _conv_shifts():
    t = np.arange(TM)[:, None]
    s = np.arange(TM)[None, :]
    blocks = [(s == t + k - CONV_K // 2) for k in range(CONV_K) if k != CONV_K // 2]
    return jnp.asarray(np.concatenate(blocks, axis=1), MXU_DTYPE)


def _log_decay_perms():
    pa = np.zeros((3, W_SMALL, W_SMALL), np.float32)
    pb = np.zeros((3, W_SMALL, W_SMALL), np.float32)
    ones_ab = np.zeros((2, W_SMALL), np.float32)
    slots = ([(G_BCUM + h, G_LI + h) for h in range(MLSTM_HEADS)]
             + [(G_ACS + h, G_ACS + h) for h in range(SSD_HEADS)])
    for n, (cum_lane, x_lane) in enumerate(slots):
        for k in range(3):
            pa[k, cum_lane, 8 * n + k] = 1.0
            pb[k, x_lane, 8 * n + 3 + k] = 1.0
            ones_ab[0, 8 * n + 3 + k] = 1.0
            ones_ab[1, 8 * n + k] = 1.0
    return jnp.asarray(pa, MXU_DTYPE), jnp.asarray(pb, MXU_DTYPE), jnp.asarray(ones_ab, F32)


def kernel(x, c, ctx, c_ctx, ada_w, ada_b, norm1_w, norm2_w, w_in, mlstm_conv_w, mlstm_conv_b, mlstm_ib, mlstm_fb, mlstm_norm_w, s5_lam_re, s5_lam_im, s5_log_dt, s5_b_re, s5_b_im, s5_c_re, s5_c_im, s5_d, s5_glu_w, na_rpb, ssd_conv_w, ssd_conv_b, ssd_a_log, ssd_dt_bias, ssd_d, ssd_norm_w, w_branch_a, w_branch_b, w_branch_c, w_branch_d, w_out, ffn_w_in, ffn_w_out, final_norm_w):
    b, t, d = x.shape
    depth = w_in.shape[0]
    assert ctx.shape[1] == TM and t % TM == 0 and t % GRID_W == 0 and b % 8 == 0 and b % NB == 0
    assert t // GRID_W >= NA_KH

    seq = jnp.concatenate([ctx, x], axis=1)

    pad = (-(b + 1)) % 8
    cc = jnp.concatenate([c, c_ctx[None, :], jnp.zeros((pad, d), F32)], axis=0)
    mod_all = _adaln(cc, ada_w, ada_b)
    mod_x = mod_all[:, :b].reshape(depth, b, 1, 6, d)
    mod_c = jnp.broadcast_to(mod_all[:, b].reshape(depth, 1, 1, 6, d), (depth, b, 1, 6, d))
    mod = jnp.concatenate([mod_c, mod_x], axis=2)

    cos_t, sin_t = _rope_tables(t, MLSTM_HEADS * HEAD_DIM)
    perm = _gate_perm()
    pa, pb, ones_ab = _log_decay_perms()
    shifts = _conv_shifts()
    n_state = S5_GROUPS * S5_STATE

    for l in range(depth):
        with_ctx = l < depth - 1
        wl = w_in[l]
        w_small = jnp.concatenate([wl[:, 1024:1040], wl[:, 3600:3616], jnp.zeros((d, W_SMALL - 32), F32)], axis=1)
        w_proj = jnp.concatenate([wl[:, 0:512], wl[:, 2576:3600], wl[:, 512:768], wl[:, 768:1024], w_small,
                                  wl[:, 1040:1296], wl[:, 1296:2064], wl[:, 2064:2576]], axis=1).astype(MXU_DTYPE)
        w_gate = wl[:, 3616:].astype(MXU_DTYPE)

        cv, v, mo, small, su, nqkv, dz = _proj(seq, mod[l], norm1_w[l][None, :], w_proj)

        conv_w = jnp.concatenate([mlstm_conv_w[l], ssd_conv_w[l]], axis=1)
        conv_w = jnp.concatenate([conv_w, jnp.zeros((1, W_CV), F32)], axis=0)
        conv_b = jnp.concatenate([mlstm_conv_b[l], ssd_conv_b[l]])[None, :]
        gbias = (_lanes(mlstm_ib[l].reshape(-1), L_MI) + _lanes(mlstm_fb[l].reshape(-1), L_MF)
                 + _lanes(ssd_dt_bias[l].reshape(-1), L_DDT))
        qk, xbc, gc, gr, la, lb = _prep(cv, small, conv_w, conv_b, cos_t, sin_t, gbias,
                                        _lanes(ssd_a_log[l].reshape(-1), L_DDT), perm, pa, pb, ones_ab, shifts)

        hm = _mlstm(qk, v, gc, la, lb)

        braw_re = _block_diag(jnp.swapaxes(s5_b_re[l], 1, 2))
        braw_im = _block_diag(jnp.swapaxes(s5_b_im[l], 1, 2))
        cw = jnp.concatenate([_block_diag(jnp.swapaxes(s5_c_re[l], 1, 2)),
                              -_block_diag(jnp.swapaxes(s5_c_im[l], 1, 2))], axis=0).astype(MXU_DTYPE)
        ys = _s5(su.reshape(-1, W_SU), s5_lam_re[l].reshape(2, 1, n_state), s5_lam_im[l].reshape(2, 1, n_state),
                 jnp.repeat(s5_log_dt[l], S5_STATE, axis=1).reshape(2, 1, n_state), braw_re, braw_im, cw, b)
        ys = ys.reshape(2, -1, b * W_SU)

        yna = _na(nqkv, _na_bias_tables(na_rpb[l]), with_ctx)

        yd = _ssd(xbc, gc, gr, la, lb, jnp.repeat(ssd_d[l], HEAD_DIM)[None, :])

        seq = _merge(seq, mod[l], norm1_w[l][None, :], hm, mo, mlstm_norm_w[l][None, :], ys, su,
                     s5_d[l][None, :], s5_glu_w[l].astype(MXU_DTYPE), yna, yd, dz, ssd_norm_w[l][None, :],
                     w_gate, w_branch_a[l].astype(MXU_DTYPE), w_branch_b[l].astype(MXU_DTYPE),
                     w_branch_c[l].astype(MXU_DTYPE), w_branch_d[l].astype(MXU_DTYPE), w_out[l].astype(MXU_DTYPE),
                     with_ctx)
        seq = _ffn(seq, mod[l], norm2_w[l][None, :], ffn_w_in[l].astype(MXU_DTYPE), ffn_w_out[l].astype(MXU_DTYPE),
                   final_norm_w[None, :], not with_ctx)
    return seq
```

```python
import functools
import math

import jax
import jax.numpy as jnp
import numpy as np
from jax import lax
from jax.experimental import pallas as pl
from jax.experimental.pallas import tpu as pltpu

F32 = jnp.float32
MXU_DTYPE = jnp.bfloat16
HIGHEST = lax.Precision.HIGHEST

GRID_W = 64
EPS = 1e-6
CONV_K = 7
ROPE_THETA = 10000.0
HEAD_DIM = 64
MLSTM_HEADS = 4
S5_GROUPS = 16
S5_GROUP = 16
S5_STATE = 64
NA_HEADS = 4
NA_KH = 8
NA_KW = 16
SSD_HEADS = 8
SSD_GROUPS = 2
SSD_STATE = 128

TM = 256
NB = 2
NBS = 4
SUB = 128
HALO = 8
S5_SUB = 32
NEG = -1e30
VMEM_LIMIT = 56 * 1024 * 1024

W_QK, W_XBC, W_V, W_O, W_SMALL, W_SU, W_NQKV, W_DZ = 512, 1024, 256, 256, 128, 256, 768, 512
W_CV = W_QK + W_XBC
PROJ_WIDTHS = (W_CV, W_V, W_O, W_SMALL, W_SU, W_NQKV, W_DZ)
SU_POS = 4
L_MI, L_MF, L_DDT = 0, 8, 16
G_LI, G_BCUM, G_ACS, G_DT = 0, 4, 8, 16


def _mm(a, b):
    return jnp.dot(a.astype(MXU_DTYPE), b.astype(MXU_DTYPE), preferred_element_type=F32)


def _mm_nt(a, b):
    return lax.dot_general(a.astype(MXU_DTYPE), b.astype(MXU_DTYPE), (((1,), (1,)), ((), ())),
                           preferred_element_type=F32)


def _mm_tn(a, b):
    return lax.dot_general(a.astype(MXU_DTYPE), b.astype(MXU_DTYPE), (((0,), (0,)), ((), ())),
                           preferred_element_type=F32)


def _mm_f32(a, b):
    return jnp.dot(a, b, preferred_element_type=F32, precision=HIGHEST)


def _split3(x):
    hi = x.astype(MXU_DTYPE)
    r1 = x - hi.astype(F32)
    mid = r1.astype(MXU_DTYPE)
    lo = (r1 - mid.astype(F32)).astype(MXU_DTYPE)
    return hi, mid, lo


def _as_01(sel):
    if sel.dtype == jnp.bool_:
        sel = jnp.where(sel, 1.0, 0.0)
    return sel.astype(MXU_DTYPE)


def _mm_exact_rhs(x, sel, terms=3):
    sel = _as_01(sel)
    return sum(jnp.dot(p, sel, preferred_element_type=F32) for p in _split3(x)[:terms])


def _mm_exact_lhs(sel, x):
    sel = _as_01(sel)
    return sum(jnp.dot(sel, p, preferred_element_type=F32) for p in _split3(x))


def _sigmoid(x):
    return 1.0 / (1.0 + jnp.exp(-x))


def _silu(x):
    return x * _sigmoid(x)


def _softplus(x):
    return jnp.maximum(x, 0.0) + jnp.log(1.0 + jnp.exp(-jnp.abs(x)))


def _iota(shape, dim):
    return lax.broadcasted_iota(jnp.int32, shape, dim)


def _head_of_lane(width):
    return jnp.right_shift(_iota((1, width), 1), 6)


def _same_head(rows, cols):
    return jnp.right_shift(_iota((rows, cols), 0), 6) == jnp.right_shift(_iota((rows, cols), 1), 6)


def _expand_heads(cols, width):
    head = _head_of_lane(width)
    out = jnp.broadcast_to(cols[0], (cols[0].shape[0], width))
    for h in range(1, len(cols)):
        out = jnp.where(head == h, cols[h], out)
    return out


def _stack(fn):
    return jnp.concatenate([fn(i) for i in range(NB)], axis=0)


def _round_robin(chains):
    live = list(chains)
    while live:
        for chain in list(live):
            try:
                next(chain)
            except StopIteration:
                live.remove(chain)


def _tile_of(d, j, nt):
    return jnp.where(d == 0, j, jnp.where(j == 0, 0, nt - j))


def _norm_mod(x, w, scale, shift):
    y = x * lax.rsqrt(jnp.mean(x * x, axis=-1, keepdims=True) + EPS) * w
    return y * (1.0 + scale) + shift


def _order_mask(rev, n):
    diff = (_iota((n, n), 1) - _iota((n, n), 0)) * jnp.where(rev, -1, 1)
    return diff <= 0


def _const_spec(arr, n_grid, single=False):
    kwargs = {"pipeline_mode": pl.Buffered(1)} if single else {}
    return pl.BlockSpec(arr.shape, lambda *_: (0,) * arr.ndim, **kwargs)


def _mod_spec(d, first):
    return pl.BlockSpec((NB, None, 6, d), lambda bp, j: (bp, jnp.minimum(first + j, 1), 0, 0))


def _adaln_kernel(c_ref, w_ref, b_ref, o_ref):
    o_ref[...] = _mm_f32(_silu(c_ref[...]), w_ref[...]) + b_ref[...]


def _adaln(cc, ada_w, ada_b):
    depth, d, n = ada_w.shape
    tn = 768
    return pl.pallas_call(
        _adaln_kernel,
        grid=(depth, n // tn),
        in_specs=[pl.BlockSpec(cc.shape, lambda l, i: (0, 0)),
                  pl.BlockSpec((None, d, tn), lambda l, i: (l, 0, i)),
                  pl.BlockSpec((None, 1, tn), lambda l, i: (l, 0, i))],
        out_specs=pl.BlockSpec((None, cc.shape[0], tn), lambda l, i: (l, 0, i)),
        out_shape=jax.ShapeDtypeStruct((depth, cc.shape[0], n), F32),
        name="adaln",
    )(cc, ada_w, ada_b.reshape(depth, 1, n))


def _proj_kernel(x_ref, mod_ref, nw_ref, w_ref, *out_refs):
    h = _stack(lambda i: _norm_mod(x_ref[i], nw_ref[...], mod_ref[i, 1:2, :], mod_ref[i, 0:1, :]))
    h = h.astype(MXU_DTYPE)
    off = 0
    for pos, (ref, n) in enumerate(zip(out_refs, PROJ_WIDTHS)):
        res = jnp.dot(h, w_ref[:, off:off + n], preferred_element_type=F32)
        if pos == SU_POS:
            ref[...] = jnp.concatenate([res[i * TM:(i + 1) * TM, :] for i in range(NB)], axis=1)
        else:
            for i in range(NB):
                ref[i] = res[i * TM:(i + 1) * TM, :]
        off += n


def _proj(seq, mod, norm_w, w_proj):
    b, s, d = seq.shape
    nt = s // TM

    def tok(width):
        return pl.BlockSpec((NB, TM, width), lambda bp, j: (bp, j, 0))

    out_specs = [tok(w) for w in PROJ_WIDTHS]
    out_shape = [jax.ShapeDtypeStruct((b, s, w), F32) for w in PROJ_WIDTHS]
    out_specs[SU_POS] = pl.BlockSpec((TM, NB * W_SU), lambda bp, j: (j, bp))
    out_shape[SU_POS] = jax.ShapeDtypeStruct((s, b * W_SU), F32)
    return pl.pallas_call(
        _proj_kernel,
        grid=(b // NB, nt),
        in_specs=[tok(d), _mod_spec(d, 0), _const_spec(norm_w, 2), _const_spec(w_proj, 2, single=True)],
        out_specs=out_specs,
        out_shape=out_shape,
        compiler_params=pltpu.CompilerParams(vmem_limit_bytes=VMEM_LIMIT),
        name="proj",
    )(seq, mod, norm_w, w_proj)


def _prep_kernel(cv_ref, cvp_ref, cvn_ref, cw_ref, cb_ref, cos_ref, sin_ref, sm_ref, gb_ref, al_ref, perm_ref,
                 pa_ref, pb_ref, one_ref, shift_ref, qk_ref, xbc_ref, gc_ref, gr_ref, la_ref, lb_ref, *, nt):
    t = pl.program_id(1)
    prev_ok = t > 1
    next_ok = (t >= 1) & (t < nt - 1)
    lane = _iota((1, W_SMALL), 1)
    first = jnp.bitwise_and(_iota((1, W_QK), 1), 31) < 16
    r = _iota((TM, TM), 0)
    c = _iota((TM, TM), 1)
    same_chunk = jnp.right_shift(r, 7) == jnp.right_shift(c, 7)
    cum = (lane >= G_BCUM) & (lane < G_DT)

    mid = CONV_K // 2
    side_taps = [k for k in range(CONV_K) if k != mid]

    def edge_rows(ext24):
        out = cb_ref[...] + cw_ref[mid:mid + 1, :] * ext24[HALO:2 * HALO, :]
        for k in side_taps:
            out = out + cw_ref[k:k + 1, :] * pltpu.roll(ext24, (mid - k) % (3 * HALO), 0)[HALO:2 * HALO, :]
        return out

    for i in range(NB):
        cur = cv_ref[i]
        weighted = jnp.concatenate([(cw_ref[k:k + 1, :] * cur).astype(MXU_DTYPE) for k in side_taps], axis=0)
        acc = (cb_ref[...] + cw_ref[mid:mid + 1, :] * cur
               + jnp.dot(shift_ref[...], weighted, preferred_element_type=F32))
        top = edge_rows(jnp.concatenate([jnp.where(prev_ok, cvp_ref[i], 0.0), cur[0:2 * HALO, :]], axis=0))
        bot = edge_rows(jnp.concatenate([cur[TM - 2 * HALO:, :], jnp.where(next_ok, cvn_ref[i], 0.0)], axis=0))
        acc = _silu(jnp.concatenate([top, acc[HALO:TM - HALO, :], bot], axis=0))
        xbc_ref[i] = acc[:, W_QK:]
        qk = acc[:, :W_QK]
        partner = jnp.where(first, pltpu.roll(qk, W_QK - 16, 1), pltpu.roll(qk, 16, 1))
        qk_ref[i] = qk * cos_ref[...] + partner * sin_ref[...]

        g_all = sm_ref[i] + gb_ref[...]
        dt = _softplus(g_all)
        src = jnp.where(lane < L_MF, g_all,
                        jnp.where(lane < L_DDT, -_softplus(-g_all), dt * (-jnp.exp(al_ref[...]))))
        src = jnp.concatenate([src, dt], axis=1)
        for d in range(2):
            tri = same_chunk & ((c <= r) if d == 0 else (c >= r))
            g = _mm_exact_rhs(src, perm_ref[d])
            g = jnp.where(cum, _mm_exact_lhs(tri, jnp.where(cum, g, 0.0)), g)
            gc_ref[d, i] = g
            a_mat = one_ref[0:1, :] + sum(jnp.dot(p, pa_ref[k], preferred_element_type=F32)
                                           for k, p in enumerate(_split3(g)))
            cv = jnp.where(lane < MLSTM_HEADS, g - pltpu.roll(g, W_SMALL - G_BCUM, 1), -g)
            b_mat = one_ref[1:2, :] + sum(jnp.dot(p, pb_ref[k], preferred_element_type=F32)
                                           for k, p in enumerate(_split3(cv)))
            la_ref[d, i] = a_mat.astype(MXU_DTYPE)
            for ci in range(TM // SUB):
                gr_ref[d, i, ci] = g[ci * SUB:(ci + 1) * SUB, :].T
                lb_ref[d, i, ci] = b_mat[ci * SUB:(ci + 1) * SUB, :].T.astype(MXU_DTYPE)


def _prep(cv, small, conv_w, conv_b, cos_t, sin_t, gbias, alog, perm, pa, pb, ones_ab, shifts):
    b, s, _ = cv.shape
    nt = s // TM
    per = TM // HALO
    last = s // HALO - 1
    nsub = TM // SUB
    return pl.pallas_call(
        functools.partial(_prep_kernel, nt=nt),
        grid=(b // NB, nt),
        in_specs=[pl.BlockSpec((NB, TM, W_CV), lambda bp, j: (bp, j, 0)),
                  pl.BlockSpec((NB, HALO, W_CV), lambda bp, j: (bp, jnp.maximum(j * per - 1, 0), 0)),
                  pl.BlockSpec((NB, HALO, W_CV), lambda bp, j: (bp, jnp.minimum((j + 1) * per, last), 0)),
                  _const_spec(conv_w, 2), _const_spec(conv_b, 2),
                  pl.BlockSpec((TM, W_QK), lambda bp, j: (j, 0)),
                  pl.BlockSpec((TM, W_QK), lambda bp, j: (j, 0)),
                  pl.BlockSpec((NB, TM, W_SMALL), lambda bp, j: (bp, j, 0)),
                  _const_spec(gbias, 2), _const_spec(alog, 2), _const_spec(perm, 2),
                  _const_spec(pa, 2), _const_spec(pb, 2), _const_spec(ones_ab, 2), _const_spec(shifts, 2)],
        out_specs=[pl.BlockSpec((NB, TM, W_QK), lambda bp, j: (bp, j, 0)),
                   pl.BlockSpec((NB, TM, W_XBC), lambda bp, j: (bp, j, 0)),
                   pl.BlockSpec((2, NB, TM, W_SMALL), lambda bp, j: (0, bp, j, 0)),
                   pl.BlockSpec((2, NB, nsub, W_SMALL, SUB), lambda bp, j: (0, bp, j, 0, 0)),
                   pl.BlockSpec((2, NB, TM, W_SMALL), lambda bp, j: (0, bp, j, 0)),
                   pl.BlockSpec((2, NB, nsub, W_SMALL, SUB), lambda bp, j: (0, bp, j, 0, 0))],
        out_shape=[jax.ShapeDtypeStruct((b, s, W_QK), F32),
                   jax.ShapeDtypeStruct((b, s, W_XBC), F32),
                   jax.ShapeDtypeStruct((2, b, s, W_SMALL), F32),
                   jax.ShapeDtypeStruct((2, b, s // SUB, W_SMALL, SUB), F32),
                   jax.ShapeDtypeStruct((2, b, s, W_SMALL), MXU_DTYPE),
                   jax.ShapeDtypeStruct((2, b, s // SUB, W_SMALL, SUB), MXU_DTYPE)],
        compiler_params=pltpu.CompilerParams(vmem_limit_bytes=VMEM_LIMIT),
        name="prep",
    )(cv, cv, cv, conv_w, conv_b, cos_t, sin_t, small, gbias, alog, perm, pa, pb, ones_ab, shifts)


def _scan_specs(nt, widths, kinds):
    nsub = TM // SUB
    specs = [pl.BlockSpec((NBS, TM, w), lambda d, j, bp: (bp, _tile_of(d, j, nt), 0)) for w in widths]
    for kind in kinds:
        if kind == "col":
            specs.append(pl.BlockSpec((None, NBS, TM, W_SMALL), lambda d, j, bp: (d, bp, _tile_of(d, j, nt), 0)))
        else:
            specs.append(pl.BlockSpec((None, NBS, nsub, W_SMALL, SUB),
                                      lambda d, j, bp: (d, bp, _tile_of(d, j, nt), 0, 0)))
    return specs


def _mlstm_kernel(qk_ref, v_ref, gc_ref, la_ref, lb_ref, h_ref, c_scr, n_scr, m_scr):
    d = pl.program_id(0)
    j = pl.program_id(1)
    bp = pl.program_id(2)
    rev = d == 1
    width = MLSTM_HEADS * HEAD_DIM
    nsub = TM // SUB
    heads = list(range(MLSTM_HEADS))
    stacked = (MLSTM_HEADS * SUB, SUB)
    t_minus_s = (jnp.bitwise_and(_iota(stacked, 0), SUB - 1) - _iota(stacked, 1)) * jnp.where(rev, -1, 1)
    mask_s = t_minus_s >= 0
    head = _head_of_lane(width)
    same_head = _same_head(width, width)
    lane = _iota((1, W_SMALL), 1)
    valid = lane < MLSTM_HEADS
    group = jnp.right_shift(lane, 3)
    to_heads = _as_01(jnp.right_shift(_iota((W_SMALL, width), 1), 6) == _iota((W_SMALL, width), 0))
    from_heads = _as_01(jnp.right_shift(_iota((width, W_SMALL), 0), 6) == _iota((width, W_SMALL), 1))
    ones_rows = jnp.ones((8, SUB), MXU_DTYPE)

    @pl.when(j == 0)
    def _():
        for i in range(NBS):
            c_scr[bp * NBS + i] = jnp.zeros(c_scr.shape[1:], F32)
            n_scr[bp * NBS + i] = jnp.zeros(n_scr.shape[1:], F32)
            m_scr[bp * NBS + i] = jnp.zeros(m_scr.shape[1:], F32)

    def chain(i):
        bi = bp * NBS + i
        c_st = c_scr[bi]
        n_st = n_scr[bi, 0:1, :]
        m_st = m_scr[bi, 0:1, :]
        chunks = []
        for step in range(nsub):
            ci = jnp.where(rev, nsub - 1 - step, step)
            rows = pl.ds(pl.multiple_of(ci * SUB, SUB), SUB)
            q = qk_ref[i, rows, 0:width]
            k = qk_ref[i, rows, width:2 * width]
            qb = q.astype(MXU_DTYPE)
            kb = k.astype(MXU_DTYPE)
            vb = v_ref[i, rows, :].astype(MXU_DTYPE)
            li = gc_ref[i, rows, :]
            bc = pltpu.roll(li, W_SMALL - G_BCUM, 1)
            la = la_ref[i, rows, :]
            lb = lb_ref[i, ci]

            la_s = jnp.concatenate([jnp.where(group == h, la, 0) for h in heads], axis=0)
            tiles = [slice((h // 2) * 128, (h // 2 + 1) * 128) for h in heads]
            q_s = [jnp.where(head[:, tiles[h]] == h, qb[:, tiles[h]], 0) for h in heads]
            log_w = jnp.where(mask_s, jnp.dot(la_s, lb, preferred_element_type=F32), NEG)
            yield
            m_in_s = jnp.max(log_w, axis=1, keepdims=True)
            qk = jnp.concatenate([_mm_nt(jnp.concatenate(q_s[2 * p:2 * p + 2], axis=0), kb[:, tiles[2 * p]])
                                  for p in range(MLSTM_HEADS // 2)], axis=0)
            sm = qk * jnp.exp(log_w - m_in_s)
            rs = jnp.sum(sm, axis=1, keepdims=True)
            sm = sm.astype(MXU_DTYPE)
            yield
            pv = jnp.dot(sm, vb, preferred_element_type=F32)
            yield
            num0 = pv[0:SUB, :]
            den0 = jnp.broadcast_to(rs[0:SUB, :], (SUB, W_SMALL))
            m_in = jnp.broadcast_to(m_in_s[0:SUB, :], (SUB, W_SMALL))
            for h in heads[1:]:
                blk = slice(h * SUB, (h + 1) * SUB)
                num0 = jnp.where(head == h, pv[blk, :], num0)
                den0 = jnp.where(lane == h, rs[blk, :], den0)
                m_in = jnp.where(lane == h, m_in_s[blk, :], m_in)
            b_last = jnp.where(rev, bc[0:1, :], bc[SUB - 1:SUB, :])
            log_k = b_last - bc + li
            mk = jnp.max(log_k, axis=0, keepdims=True)
            wk_full = _mm_exact_rhs(jnp.where(valid, jnp.exp(log_k - mk), 0.0), to_heads, terms=2)
            kw = (k * wk_full).astype(MXU_DTYPE)
            yield
            kv0 = jnp.where(same_head, _mm_tn(kw, vb), 0.0)
            ks0 = jnp.dot(ones_rows, kw, preferred_element_type=F32)[0:1, :]
            chunks.append((rows, q, qb, bc, num0, den0, m_in, b_last, mk, kv0, ks0))
            yield

        for rows, q, qb, bc, num0, den0, m_in, b_last, mk, kv0, ks0 in chunks:
            inter = bc + m_st
            m_t = jnp.maximum(inter, m_in)
            r = jnp.exp(m_in - m_t)
            g = jnp.exp(inter - m_t)
            den = r * den0 + g * jnp.dot((q * n_st).astype(MXU_DTYPE), from_heads, preferred_element_type=F32)
            inv = 1.0 / jnp.maximum(jnp.abs(den), jnp.exp(-m_t))
            ir_full = _mm_exact_rhs(jnp.where(valid, inv * r, 0.0), to_heads, terms=2)
            ig_full = _mm_exact_rhs(jnp.where(valid, inv * g, 0.0), to_heads, terms=2)
            h_ref[i, rows, :] = num0 * ir_full + _mm(qb, c_st) * ig_full
            yield
            m_new = jnp.maximum(b_last + m_st, mk)
            scales = jnp.concatenate([jnp.exp(b_last + m_st - m_new), jnp.exp(mk - m_new),
                                      jnp.zeros((6, W_SMALL), F32)], axis=0)
            scales_full = _mm_exact_rhs(jnp.where(valid, scales, 0.0), to_heads)
            c_st = c_st * scales_full[0:1, :] + kv0 * scales_full[1:2, :]
            n_st = n_st * scales_full[0:1, :] + ks0 * scales_full[1:2, :]
            m_st = jnp.where(valid, m_new, 0.0)
            yield

        c_scr[bi] = c_st
        n_scr[bi, 0:1, :] = n_st
        m_scr[bi, 0:1, :] = m_st

    _round_robin([chain(i) for i in range(NBS)])


def _mlstm(qk, v, gc, la, lb):
    b, s, _ = qk.shape
    nt = s // TM
    width = MLSTM_HEADS * HEAD_DIM
    return pl.pallas_call(
        _mlstm_kernel,
        grid=(2, nt, b // NBS),
        in_specs=_scan_specs(nt, (2 * width, width), ("col", "col", "row")),
        out_specs=pl.BlockSpec((None, NBS, TM, width), lambda d, j, bp: (d, bp, _tile_of(d, j, nt), 0)),
        out_shape=jax.ShapeDtypeStruct((2, b, s, width), F32),
        scratch_shapes=[pltpu.VMEM((b, width, width), F32), pltpu.VMEM((b, 8, width), F32),
                        pltpu.VMEM((b, 8, W_SMALL), F32)],
        compiler_params=pltpu.CompilerParams(vmem_limit_bytes=VMEM_LIMIT),
        name="mlstm",
    )(qk, v, gc, la, lb)


def _s5_kernel(u_ref, lre_ref, lim_ref, ldt_ref, bre_ref, bim_ref, cw_ref, *rest, nb, rev):
    acc_ref = rest[0] if rev else None
    y_ref, wbu_scr, are_scr, aim_scr, hre_scr, him_scr, hb0, hb1, hb2 = rest[1:] if rev else rest
    j = pl.program_id(0)
    n_state = S5_GROUPS * S5_STATE

    @pl.when(j == 0)
    def _():
        lre = lre_ref[...]
        lim = lim_ref[...]
        dt = jnp.exp(ldt_ref[...])
        mag = jnp.exp(lre * dt)
        a_re = mag * jnp.cos(lim * dt)
        a_im = mag * jnp.sin(lim * dt)
        den = lre * lre + lim * lim
        nr = a_re - 1.0
        coef_re = (nr * lre + a_im * lim) / den
        coef_im = (a_im * lre - nr * lim) / den
        wbu_scr[:, :n_state] = (coef_re * bre_ref[...] - coef_im * bim_ref[...]).astype(MXU_DTYPE)
        wbu_scr[:, n_state:] = (coef_re * bim_ref[...] + coef_im * bre_ref[...]).astype(MXU_DTYPE)
        are_scr[...] = jnp.broadcast_to(a_re, are_scr.shape)
        aim_scr[...] = jnp.broadcast_to(a_im, aim_scr.shape)
        hre_scr[...] = jnp.zeros(hre_scr.shape, F32)
        him_scr[...] = jnp.zeros(him_scr.shape, F32)

    a_re = are_scr[...]
    a_im = aim_scr[...]
    rows_sub = S5_SUB * nb
    n_sub = TM // S5_SUB
    n_part = 1
    part = rows_sub // n_part
    bufs = (hb0, hb1, hb2)
    state = [hre_scr[...], him_scr[...]]

    def first_row(sb):
        return (n_sub - 1 - sb if rev else sb) * rows_sub

    def project(sb):
        for p in range(n_part):
            lhs = u_ref[first_row(sb) + p * part:first_row(sb) + (p + 1) * part, :].astype(MXU_DTYPE)
            bufs[sb % 3][p * part:(p + 1) * part, :] = jnp.dot(lhs, wbu_scr[...], preferred_element_type=F32)
            yield

    def recur(sb):
        buf = bufs[sb % 3]
        h_re, h_im = state
        for t in range(S5_SUB):
            ti = S5_SUB - 1 - t if rev else t
            r = slice(ti * nb, (ti + 1) * nb)
            n_re = a_re * h_re - a_im * h_im + buf[r, :n_state]
            n_im = a_re * h_im + a_im * h_re + buf[r, n_state:]
            buf[r, :n_state] = n_re
            buf[r, n_state:] = n_im
            h_re, h_im = n_re, n_im
            if t % 4 == 3:
                yield
        state[0], state[1] = h_re, h_im

    def readout(sb):
        for p in range(n_part):
            lhs = bufs[sb % 3][p * part:(p + 1) * part, :].astype(MXU_DTYPE)
            rows = slice(first_row(sb) + p * part, first_row(sb) + (p + 1) * part)
            y = jnp.dot(lhs, cw_ref[...], preferred_element_type=F32)
            y_ref[rows, :] = y + acc_ref[rows, :] if rev else y
            yield

    for stage in range(n_sub + 2):
        live = []
        if stage < n_sub:
            live.append(project(stage))
        if 1 <= stage <= n_sub:
            live.append(recur(stage - 1))
        if stage >= 2:
            live.append(readout(stage - 2))
        _round_robin(live)
    hre_scr[...] = state[0]
    him_scr[...] = state[1]


def _s5(u_tm, lam_re, lam_im, log_dt, braw_re, braw_im, cw, nb, fwd_out=None):
    rev = fwd_out is not None
    rows, width = u_tm.shape
    s = rows // nb
    nt = s // TM
    n_state = S5_GROUPS * S5_STATE
    d = 1 if rev else 0
    hbuf = pltpu.VMEM((S5_SUB * nb, 2 * n_state), F32)
    return pl.pallas_call(
        functools.partial(_s5_kernel, nb=nb, rev=rev),
        grid=(nt,),
        in_specs=[pl.BlockSpec((TM * nb, width), lambda j: (_tile_of(d, j, nt), 0)),
                  _const_spec(lam_re, 1), _const_spec(lam_im, 1), _const_spec(log_dt, 1),
                  _const_spec(braw_re, 1), _const_spec(braw_im, 1), _const_spec(cw, 1)]
        + ([pl.BlockSpec((TM * nb, width), lambda j: (_tile_of(d, j, nt), 0))] if rev else []),
        out_specs=pl.BlockSpec((TM * nb, width), lambda j: (_tile_of(d, j, nt), 0)),
        out_shape=jax.ShapeDtypeStruct((rows, width), F32),
        scratch_shapes=[pltpu.VMEM((width, 2 * n_state), MXU_DTYPE),
                        pltpu.VMEM((nb, n_state), F32), pltpu.VMEM((nb, n_state), F32),
                        pltpu.VMEM((nb, n_state), F32), pltpu.VMEM((nb, n_state), F32),
                        hbuf, hbuf, hbuf],
        compiler_params=pltpu.CompilerParams(vmem_limit_bytes=VMEM_LIMIT),
        name="s5_bwd" if rev else "s5_fwd",
    )(u_tm, lam_re, lam_im, log_dt, braw_re, braw_im, cw, *([fwd_out] if rev else []))


def _na_kernel(q_ref, k_ref, v_ref, tbl_ref, o_ref, *, first_tile, n_rows):
    t = first_tile + pl.program_id(1)
    width = NA_HEADS * HEAD_DIM
    head = _head_of_lane(width)
    scale = HEAD_DIM ** -0.5
    k_ctx = k_ref[0:TM, :].astype(MXU_DTYPE)
    v_ctx = v_ref[0:TM, :].astype(MXU_DTYPE)

    def stack_heads(q):
        return jnp.concatenate([jnp.where(head == h, q, 0.0) for h in range(NA_HEADS)], axis=0).astype(MXU_DTYPE)

    def unstack_heads(o, n):
        acc = o[0:n, :]
        for h in range(1, NA_HEADS):
            acc = jnp.where(head == h, o[h * n:(h + 1) * n, :], acc)
        return acc

    @pl.when(t == 0)
    def _():
        s = _mm_nt(stack_heads(q_ref[...] * scale), k_ctx)
        p = jnp.exp(s - jnp.max(s, axis=1, keepdims=True))
        o = _mm(p, v_ctx) / jnp.sum(p, axis=1, keepdims=True)
        o_ref[...] = unstack_heads(o, TM)

    @pl.when(t > 0)
    def _():
        rows_per_tile = TM // GRID_W
        n_lat = NA_KH * GRID_W
        def grid_row(rr):
            r = (t - 1) * rows_per_tile + rr
            row_start = jnp.clip(r - NA_KH // 2, 0, n_rows - NA_KH)
            off = (NA_KH - 1) - (r - row_start)
            win = pl.ds(pl.multiple_of(TM + row_start * GRID_W, GRID_W), n_lat)
            qs = stack_heads(q_ref[rr * GRID_W:(rr + 1) * GRID_W, :] * scale)
            s_lat = _mm_nt(qs, k_ref[win, :]) + tbl_ref[off]
            s_ctx = _mm_nt(qs, k_ctx)
            yield
            m = jnp.maximum(jnp.max(s_lat, axis=1, keepdims=True), jnp.max(s_ctx, axis=1, keepdims=True))
            p_lat = jnp.exp(s_lat - m)
            p_ctx = jnp.exp(s_ctx - m)
            den = jnp.sum(p_lat, axis=1, keepdims=True) + jnp.sum(p_ctx, axis=1, keepdims=True)
            yield
            o = (_mm(p_lat, v_ref[win, :]) + _mm(p_ctx, v_ctx)) / den
            o_ref[rr * GRID_W:(rr + 1) * GRID_W, :] = unstack_heads(o, GRID_W)
            yield

        _round_robin([grid_row(rr) for rr in range(rows_per_tile)])


def _na(nqkv, tbl, with_ctx):
    b, s, _ = nqkv.shape
    nt = s // TM
    width = NA_HEADS * HEAD_DIM
    first_tile = 0 if with_ctx else 1
    n_rows = (s - TM) // GRID_W
    return pl.pallas_call(
        functools.partial(_na_kernel, first_tile=first_tile, n_rows=n_rows),
        grid=(b, nt - first_tile),
        in_specs=[pl.BlockSpec((None, TM, width), lambda bi, j: (bi, first_tile + j, 0)),
                  pl.BlockSpec((None, s, width), lambda bi, j: (bi, 0, 1)),
                  pl.BlockSpec((None, s, width), lambda bi, j: (bi, 0, 2)),
                  _const_spec(tbl, 2)],
        out_specs=pl.BlockSpec((None, TM, width), lambda bi, j: (bi, first_tile + j, 0)),
        out_shape=jax.ShapeDtypeStruct((b, s, width), F32),
        compiler_params=pltpu.CompilerParams(vmem_limit_bytes=VMEM_LIMIT),
        name="na",
    )(nqkv, nqkv, nqkv, tbl)


def _ssd_kernel(x_ref, gc_ref, gr_ref, la_ref, lb_ref, dsk_ref, y_ref, st_scr):
    d = pl.program_id(0)
    j = pl.program_id(1)
    bp = pl.program_id(2)
    rev = d == 1
    width = SSD_HEADS * HEAD_DIM
    gn = SSD_GROUPS * SSD_STATE
    per_group = SSD_HEADS // SSD_GROUPS
    gw = per_group * HEAD_DIM
    nsub = TM // SUB
    stacked = (per_group * SUB, SUB)
    t_minus_s = (jnp.bitwise_and(_iota(stacked, 0), SUB - 1) - _iota(stacked, 1)) * jnp.where(rev, -1, 1)
    mask_s = t_minus_s >= 0
    head = _head_of_lane(width)
    skip = jnp.where(rev, 0.0, 1.0) * dsk_ref[...]
    lane = _iota((1, W_SMALL), 1)
    valid = (lane >= G_ACS) & (lane < G_ACS + SSD_HEADS)
    group = jnp.right_shift(lane, 3)
    to_heads = _as_01(jnp.right_shift(_iota((W_SMALL, width), 1), 6) == _iota((W_SMALL, width), 0) - G_ACS)

    @pl.when(j == 0)
    def _():
        for i in range(NBS):
            st_scr[bp * NBS + i] = jnp.zeros(st_scr.shape[1:], F32)

    def chain(i):
        bi = bp * NBS + i
        st = st_scr[bi]
        chunks = []
        for step in range(nsub):
            ci = jnp.where(rev, nsub - 1 - step, step)
            rows = pl.ds(pl.multiple_of(ci * SUB, SUB), SUB)
            xs = x_ref[i, rows, 0:width]
            bm = x_ref[i, rows, width:width + gn].astype(MXU_DTYPE)
            cm = x_ref[i, rows, width + gn:width + 2 * gn].astype(MXU_DTYPE)
            xsb = xs.astype(MXU_DTYPE)
            acs = gc_ref[i, rows, :]
            dt = pltpu.roll(acs, W_SMALL - (G_DT - G_ACS), 1)
            grow = gr_ref[i, ci]
            la = la_ref[i, rows, :]
            lb = lb_ref[i, ci]

            a_last = jnp.where(rev, acs[0:1, :], acs[SUB - 1:SUB, :])
            e_full = _mm_exact_rhs(jnp.where(valid, jnp.exp(acs), 0.0), to_heads, terms=2)
            xw = xs * _mm_exact_rhs(jnp.where(valid, jnp.exp(a_last - acs) * dt, 0.0), to_heads, terms=2)
            last = jnp.broadcast_to(jnp.where(valid, jnp.exp(a_last), 0.0), (8, W_SMALL))
            last_full = _mm_exact_rhs(last, to_heads)[0:1, :]
            yield

            y_parts, st_parts, c_parts = [], [], []
            for g in range(SSD_GROUPS):
                b_g = bm[:, g * SSD_STATE:(g + 1) * SSD_STATE]
                c_g = cm[:, g * SSD_STATE:(g + 1) * SSD_STATE]
                lanes = slice(g * gw, (g + 1) * gw)
                hs = range(g * per_group, (g + 1) * per_group)
                la_s = jnp.concatenate([jnp.where(group == MLSTM_HEADS + h, la, 0) for h in hs], axis=0)
                dt_s = jnp.concatenate([jnp.broadcast_to(grow[G_DT + h:G_DT + h + 1, :], (SUB, SUB)) for h in hs],
                                       axis=0)
                decay = jnp.exp(jnp.where(mask_s, jnp.dot(la_s, lb, preferred_element_type=F32), NEG)) * dt_s
                cb = _mm_nt(c_g, b_g)
                m_s = (jnp.concatenate([cb] * per_group, axis=0) * decay).astype(MXU_DTYPE)
                yield
                yd = jnp.dot(m_s, xsb[:, lanes], preferred_element_type=F32)
                y_g = yd[0:SUB, :]
                for hh in range(1, per_group):
                    y_g = jnp.where(head[:, lanes] == g * per_group + hh, yd[hh * SUB:(hh + 1) * SUB, :], y_g)
                y_parts.append(y_g)
                st_parts.append(_mm_tn(b_g, xw[:, lanes]))
                c_parts.append(c_g)
                yield
            chunks.append((rows, jnp.concatenate(y_parts, axis=1) + skip * xs, jnp.concatenate(st_parts, axis=1),
                           c_parts, e_full, last_full))

        for rows, y_in, st_in, c_parts, e_full, last_full in chunks:
            y_off = jnp.concatenate([_mm(c_parts[g], st[:, g * gw:(g + 1) * gw]) for g in range(SSD_GROUPS)], axis=1)
            y_ref[i, rows, :] = y_in + y_off * e_full
            st = st * last_full + st_in
            yield
        st_scr[bi] = st

    _round_robin([chain(i) for i in range(NBS)])


def _ssd(xbc, gc, gr, la, lb, dskip):
    b, s, cw = xbc.shape
    nt = s // TM
    width = SSD_HEADS * HEAD_DIM
    return pl.pallas_call(
        _ssd_kernel,
        grid=(2, nt, b // NBS),
        in_specs=_scan_specs(nt, (cw,), ("col", "row", "col", "row")) + [_const_spec(dskip, 3)],
        out_specs=pl.BlockSpec((None, NBS, TM, width), lambda d, j, bp: (d, bp, _tile_of(d, j, nt), 0)),
        out_shape=jax.ShapeDtypeStruct((2, b, s, width), F32),
        scratch_shapes=[pltpu.VMEM((b, SSD_STATE, width), F32)],
        compiler_params=pltpu.CompilerParams(vmem_limit_bytes=VMEM_LIMIT),
        name="ssd",
    )(xbc, gc, gr, la, lb, dskip)


def _merge_kernel(x_ref, mod_ref, nw_ref, hm_ref, mo_ref, mnw_ref, ys_ref, su_ref, s5d_ref, glu_ref, na_ref,
                  yd_ref, dz_ref, dnw_ref, wg_ref, wa_ref, wb_ref, wc_ref, wd_ref, wo_ref, o_ref):
    d_model = x_ref.shape[2]
    h = _stack(lambda i: _norm_mod(x_ref[i], nw_ref[...], mod_ref[i, 1:2, :], mod_ref[i, 0:1, :]))
    h = h.astype(MXU_DTYPE)

    wm = MLSTM_HEADS * HEAD_DIM
    hm = _stack(lambda i: (hm_ref[0, i] + hm_ref[1, i]) * _sigmoid(mo_ref[i]))
    ms = _mm_exact_rhs(hm * hm, _same_head(wm, wm)) * (1.0 / HEAD_DIM)
    ya = hm * lax.rsqrt(ms + EPS) * mnw_ref[...]

    ws = S5_GROUPS * S5_GROUP
    ys = _stack(lambda i: (ys_ref[:, i * ws:(i + 1) * ws]
                           + s5d_ref[...] * su_ref[:, i * ws:(i + 1) * ws]))
    ys = 0.5 * ys * (1.0 + jnp.tanh(math.sqrt(2.0 / math.pi) * (ys + 0.044715 * (ys * ys * ys))))
    ab = _mm(ys, glu_ref[...])
    yb = ab[:, :ws] * _sigmoid(ab[:, ws:])

    yc = _stack(lambda i: na_ref[i])

    yd = _stack(lambda i: (yd_ref[0, i] + yd_ref[1, i]) * _silu(dz_ref[i]))
    yd = yd * lax.rsqrt(jnp.mean(yd * yd, axis=-1, keepdims=True) + EPS) * dnw_ref[...]

    m = None
    for k, (y, w_ref) in enumerate(((ya, wa_ref), (yb, wb_ref), (yc, wc_ref), (yd, wd_ref))):
        gate = _sigmoid(jnp.dot(h, wg_ref[:, k * d_model:(k + 1) * d_model], preferred_element_type=F32))
        term = gate * _mm(y, w_ref[...])
        m = term if m is None else m + term
    out = _mm(m, wo_ref[...])
    for i in range(NB):
        o_ref[i] = x_ref[i] + mod_ref[i, 2:3, :] * out[i * TM:(i + 1) * TM, :]


def _merge(seq, mod, norm_w, hm, mo, mnw, ys, su, s5d, glu_w, yna, yd, dz, dnw, wg, wa, wb, wc, wd, wo,
           with_ctx):
    b, s, d = seq.shape
    nt = s // TM
    first = 0 if with_ctx else 1

    def tok(width):
        return pl.BlockSpec((NB, TM, width), lambda bp, j: (bp, first + j, 0))

    def tok2(width):
        return pl.BlockSpec((2, NB, TM, width), lambda bp, j: (0, bp, first + j, 0))

    def const(arr):
        return _const_spec(arr, 2, single=True)

    ws = S5_GROUPS * S5_GROUP
    return pl.pallas_call(
        _merge_kernel,
        grid=(b // NB, nt - first),
        in_specs=[tok(d), _mod_spec(d, first), const(norm_w),
                  tok2(MLSTM_HEADS * HEAD_DIM), tok(W_O), const(mnw),
                  pl.BlockSpec((TM, NB * ws), lambda bp, j: (first + j, bp)),
                  pl.BlockSpec((TM, NB * ws), lambda bp, j: (first + j, bp)),
                  const(s5d), const(glu_w),
                  tok(NA_HEADS * HEAD_DIM),
                  tok2(SSD_HEADS * HEAD_DIM), tok(W_DZ), const(dnw),
                  const(wg), const(wa), const(wb), const(wc), const(wd), const(wo)],
        out_specs=tok(d),
        out_shape=jax.ShapeDtypeStruct((b, s, d), F32),
        input_output_aliases={0: 0},
        compiler_params=pltpu.CompilerParams(vmem_limit_bytes=VMEM_LIMIT),
        name="merge",
    )(seq, mod, norm_w, hm, mo, mnw, ys, su, s5d, glu_w, yna, yd, dz, dnw, wg, wa, wb, wc, wd, wo)


def _ffn_kernel(x_ref, mod_ref, nw_ref, wi_ref, wo_ref, fw_ref, o_ref, *, final):
    hidden = wo_ref.shape[0]
    h = _stack(lambda i: _norm_mod(x_ref[i], nw_ref[...], mod_ref[i, 4:5, :], mod_ref[i, 3:4, :]))
    h = h.astype(MXU_DTYPE)
    a = jnp.dot(h, wi_ref[:, :hidden], preferred_element_type=F32)
    g = jnp.dot(h, wi_ref[:, hidden:], preferred_element_type=F32)
    out = _mm(_silu(a) * g, wo_ref[...])
    for i in range(NB):
        y = x_ref[i] + mod_ref[i, 5:6, :] * out[i * TM:(i + 1) * TM, :]
        if final:
            y = y * lax.rsqrt(jnp.mean(y * y, axis=-1, keepdims=True) + EPS) * fw_ref[...]
        o_ref[i] = y


def _ffn(seq, mod, norm_w, wi, wo, final_w, final):
    b, s, d = seq.shape
    nt = s // TM
    first = 1 if final else 0

    def const(arr):
        return _const_spec(arr, 2, single=True)

    tok_in = pl.BlockSpec((NB, TM, d), lambda bp, j: (bp, first + j, 0))
    if final:
        out_spec = pl.BlockSpec((NB, TM, d), lambda bp, j: (bp, j, 0))
        out_shape = jax.ShapeDtypeStruct((b, s - TM, d), F32)
        aliases = {}
    else:
        out_spec = tok_in
        out_shape = jax.ShapeDtypeStruct((b, s, d), F32)
        aliases = {0: 0}
    return pl.pallas_call(
        functools.partial(_ffn_kernel, final=final),
        grid=(b // NB, nt - first),
        in_specs=[tok_in, _mod_spec(d, first), const(norm_w), const(wi), const(wo), const(final_w)],
        out_specs=out_spec,
        out_shape=out_shape,
        input_output_aliases=aliases,
        compiler_params=pltpu.CompilerParams(vmem_limit_bytes=VMEM_LIMIT),
        name="ffn_final" if final else "ffn",
    )(seq, mod, norm_w, wi, wo, final_w)


def _rope_tables(t, width):
    nf = HEAD_DIM // 4
    inv = (ROPE_THETA ** (-np.arange(nf, dtype=np.float32) / nf)).astype(np.float32)
    tok = np.arange(t)
    ang_r = (tok // GRID_W).astype(np.float32)[:, None] * inv
    ang_c = (tok % GRID_W).astype(np.float32)[:, None] * inv
    cos_h = np.concatenate([np.cos(ang_r)] * 2 + [np.cos(ang_c)] * 2, axis=1)
    sin_h = np.concatenate([-np.sin(ang_r), np.sin(ang_r), -np.sin(ang_c), np.sin(ang_c)], axis=1)
    cos_x = np.tile(cos_h, (1, width // HEAD_DIM))
    sin_x = np.tile(sin_h, (1, width // HEAD_DIM))
    cos_t = np.concatenate([np.ones((TM, width), np.float32), cos_x], axis=0)
    sin_t = np.concatenate([np.zeros((TM, width), np.float32), sin_x], axis=0)
    scale = np.float32(HEAD_DIM ** -0.5)
    return (jnp.asarray(np.concatenate([cos_t * scale, cos_t], axis=1), F32),
            jnp.asarray(np.concatenate([sin_t * scale, sin_t], axis=1), F32))


def _na_bias_tables(rpb):
    col = np.arange(GRID_W)
    col0 = np.clip(col - NA_KW // 2, 0, GRID_W - NA_KW)
    in_win = (col[None, :] >= col0[:, None]) & (col[None, :] < col0[:, None] + NA_KW)
    dc = np.clip(col[None, :] - col[:, None], -(NA_KW - 1), NA_KW - 1) + (NA_KW - 1)
    pick = jnp.asarray(dc[None] == np.arange(2 * NA_KW - 1)[:, None, None], F32)
    per_row = jnp.einsum('hrd,dqk->hrqk', rpb.astype(F32), pick, precision=HIGHEST)
    per_row = jnp.where(in_win, per_row, NEG)
    bias = jnp.stack([per_row[:, off:off + NA_KH] for off in range(NA_KH)], axis=0)
    return bias.transpose(0, 1, 3, 2, 4).reshape(NA_KH, NA_HEADS * GRID_W, NA_KH * GRID_W)


def _block_diag(blocks):
    g, r, c = blocks.shape
    eye = jnp.eye(g, dtype=blocks.dtype)
    return (eye[:, None, :, None] * blocks[:, :, None, :]).reshape(g * r, g * c)


def _lanes(vec, at, width=W_SMALL):
    return jnp.zeros((1, width), F32).at[0, at:at + vec.shape[0]].set(vec.astype(F32))


def _gate_perm():
    perm = np.zeros((2, 2 * W_SMALL, W_SMALL), np.float32)
    for d in range(2):
        for h in range(MLSTM_HEADS):
            perm[d, L_MI + d * MLSTM_HEADS + h, G_LI + h] = 1.0
            perm[d, L_MF + d * MLSTM_HEADS + h, G_BCUM + h] = 1.0
        for h in range(SSD_HEADS):
            perm[d, L_DDT + d * SSD_HEADS + h, G_ACS + h] = 1.0
            perm[d, W_SMALL + L_DDT + d * SSD_HEADS + h, G_DT + h] = 1.0
    return jnp.asarray(perm, MXU_DTYPE)


def _conv_shifts():
    t = np.arange(TM)[:, None]
    s = np.arange(TM)[None, :]
    blocks = [(s == t + k - CONV_K // 2) for k in range(CONV_K) if k != CONV_K // 2]
    return jnp.asarray(np.concatenate(blocks, axis=1), MXU_DTYPE)


def _log_decay_perms():
    pa = np.zeros((3, W_SMALL, W_SMALL), np.float32)
    pb = np.zeros((3, W_SMALL, W_SMALL), np.float32)
    ones_ab = np.zeros((2, W_SMALL), np.float32)
    slots = ([(G_BCUM + h, G_LI + h) for h in range(MLSTM_HEADS)]
             + [(G_ACS + h, G_ACS + h) for h in range(SSD_HEADS)])
    for n, (cum_lane, x_lane) in enumerate(slots):
        for k in range(3):
            pa[k, cum_lane, 8 * n + k] = 1.0
            pb[k, x_lane, 8 * n + 3 + k] = 1.0
            ones_ab[0, 8 * n + 3 + k] = 1.0
            ones_ab[1, 8 * n + k] = 1.0
    return jnp.asarray(pa, MXU_DTYPE), jnp.asarray(pb, MXU_DTYPE), jnp.asarray(ones_ab, F32)


def kernel(x, c, ctx, c_ctx, ada_w, ada_b, norm1_w, norm2_w, w_in, mlstm_conv_w, mlstm_conv_b, mlstm_ib, mlstm_fb, mlstm_norm_w, s5_lam_re, s5_lam_im, s5_log_dt, s5_b_re, s5_b_im, s5_c_re, s5_c_im, s5_d, s5_glu_w, na_rpb, ssd_conv_w, ssd_conv_b, ssd_a_log, ssd_dt_bias, ssd_d, ssd_norm_w, w_branch_a, w_branch_b, w_branch_c, w_branch_d, w_out, ffn_w_in, ffn_w_out, final_norm_w):
    b, t, d = x.shape
    depth = w_in.shape[0]
    assert ctx.shape[1] == TM and t % TM == 0 and t % GRID_W == 0 and b % 8 == 0 and b % NB == 0
    assert t // GRID_W >= NA_KH

    seq = jnp.concatenate([ctx, x], axis=1)

    pad = (-(b + 1)) % 8
    cc = jnp.concatenate([c, c_ctx[None, :], jnp.zeros((pad, d), F32)], axis=0)
    mod_all = _adaln(cc, ada_w, ada_b)
    mod_x = mod_all[:, :b].reshape(depth, b, 1, 6, d)
    mod_c = jnp.broadcast_to(mod_all[:, b].reshape(depth, 1, 1, 6, d), (depth, b, 1, 6, d))
    mod = jnp.concatenate([mod_c, mod_x], axis=2)

    cos_t, sin_t = _rope_tables(t, MLSTM_HEADS * HEAD_DIM)
    perm = _gate_perm()
    pa, pb, ones_ab = _log_decay_perms()
    shifts = _conv_shifts()
    n_state = S5_GROUPS * S5_STATE

    for l in range(depth):
        with_ctx = l < depth - 1
        wl = w_in[l].astype(MXU_DTYPE)
        w_small = jnp.concatenate([wl[:, 1024:1040], wl[:, 3600:3616], jnp.zeros((d, W_SMALL - 32), MXU_DTYPE)],
                                  axis=1)
        w_proj = jnp.concatenate([wl[:, 0:512], wl[:, 2576:3600], wl[:, 512:768], wl[:, 768:1024], w_small,
                                  wl[:, 1040:1296], wl[:, 1296:2064], wl[:, 2064:2576]], axis=1)
        w_gate = wl[:, 3616:]

        cv, v, mo, small, su, nqkv, dz = _proj(seq, mod[l], norm1_w[l][None, :], w_proj)

        conv_w = jnp.concatenate([mlstm_conv_w[l], ssd_conv_w[l]], axis=1)
        conv_w = jnp.concatenate([conv_w, jnp.zeros((1, W_CV), F32)], axis=0)
        conv_b = jnp.concatenate([mlstm_conv_b[l], ssd_conv_b[l]])[None, :]
        gbias = (_lanes(mlstm_ib[l].reshape(-1), L_MI) + _lanes(mlstm_fb[l].reshape(-1), L_MF)
                 + _lanes(ssd_dt_bias[l].reshape(-1), L_DDT))
        qk, xbc, gc, gr, la, lb = _prep(cv, small, conv_w, conv_b, cos_t, sin_t, gbias,
                                        _lanes(ssd_a_log[l].reshape(-1), L_DDT), perm, pa, pb, ones_ab, shifts)

        hm = _mlstm(qk, v, gc, la, lb)

        braw_re = _block_diag(jnp.swapaxes(s5_b_re[l], 1, 2))
        braw_im = _block_diag(jnp.swapaxes(s5_b_im[l], 1, 2))
        cw = jnp.concatenate([_block_diag(jnp.swapaxes(s5_c_re[l], 1, 2)),
                              -_block_diag(jnp.swapaxes(s5_c_im[l], 1, 2))], axis=0).astype(MXU_DTYPE)
        ldt = jnp.repeat(s5_log_dt[l], S5_STATE, axis=1)
        ys = None
        for dd in range(2):
            ys = _s5(su.reshape(-1, W_SU), s5_lam_re[l, dd].reshape(1, n_state), s5_lam_im[l, dd].reshape(1, n_state),
                     ldt[dd].reshape(1, n_state), braw_re, braw_im, cw, b, fwd_out=ys)
        ys = ys.reshape(-1, b * W_SU)

        yna = _na(nqkv, _na_bias_tables(na_rpb[l]), with_ctx)

        yd = _ssd(xbc, gc, gr, la, lb, jnp.repeat(ssd_d[l], HEAD_DIM)[None, :])

        seq = _merge(seq, mod[l], norm1_w[l][None, :], hm, mo, mlstm_norm_w[l][None, :], ys, su,
                     s5_d[l][None, :], s5_glu_w[l].astype(MXU_DTYPE), yna, yd, dz, ssd_norm_w[l][None, :],
                     w_gate, w_branch_a[l].astype(MXU_DTYPE), w_branch_b[l].astype(MXU_DTYPE),
                     w_branch_c[l].astype(MXU_DTYPE), w_branch_d[l].astype(MXU_DTYPE), w_out[l].astype(MXU_DTYPE),
                     with_ctx)
        seq = _ffn(seq, mod[l], norm2_w[l][None, :], ffn_w_in[l].astype(MXU_DTYPE), ffn_w_out[l].astype(MXU_DTYPE),
                   final_norm_w[None, :], not with_ctx)
    return seq
```

```python
import functools
import math

import jax
import jax.numpy as jnp
import numpy as np
from jax import lax
from jax.experimental import pallas as pl
from jax.experimental.pallas import tpu as pltpu

F32 = jnp.float32
MXU_DTYPE = jnp.bfloat16
HIGHEST = lax.Precision.HIGHEST

GRID_W = 64
EPS = 1e-6
CONV_K = 7
ROPE_THETA = 10000.0
HEAD_DIM = 64
MLSTM_HEADS = 4
S5_GROUPS = 16
S5_GROUP = 16
S5_STATE = 64
NA_HEADS = 4
NA_KH = 8
NA_KW = 16
SSD_HEADS = 8
SSD_GROUPS = 2
SSD_STATE = 128

TM = 256
NB = 2
NBS = 4
SUB = 128
HALO = 8
S5_SUB = 32
NEG = -1e30
VMEM_LIMIT = 56 * 1024 * 1024

W_QK, W_XBC, W_V, W_O, W_SMALL, W_SU, W_NQKV, W_DZ = 512, 1024, 256, 256, 128, 256, 768, 512
W_CV = W_QK + W_XBC
PROJ_WIDTHS = (W_CV, W_V, W_O, W_SMALL, W_SU, W_NQKV, W_DZ)
SU_POS = 4
L_MI, L_MF, L_DDT = 0, 8, 16
G_LI, G_BCUM, G_ACS, G_DT = 0, 4, 8, 16


def _mm(a, b):
    return jnp.dot(a.astype(MXU_DTYPE), b.astype(MXU_DTYPE), preferred_element_type=F32)


def _mm_nt(a, b):
    return lax.dot_general(a.astype(MXU_DTYPE), b.astype(MXU_DTYPE), (((1,), (1,)), ((), ())),
                           preferred_element_type=F32)


def _mm_tn(a, b):
    return lax.dot_general(a.astype(MXU_DTYPE), b.astype(MXU_DTYPE), (((0,), (0,)), ((), ())),
                           preferred_element_type=F32)


def _mm_f32(a, b):
    return jnp.dot(a, b, preferred_element_type=F32, precision=HIGHEST)


def _split3(x):
    hi = x.astype(MXU_DTYPE)
    r1 = x - hi.astype(F32)
    mid = r1.astype(MXU_DTYPE)
    lo = (r1 - mid.astype(F32)).astype(MXU_DTYPE)
    return hi, mid, lo


def _as_01(sel):
    if sel.dtype == jnp.bool_:
        sel = jnp.where(sel, 1.0, 0.0)
    return sel.astype(MXU_DTYPE)


def _mm_exact_rhs(x, sel, terms=3):
    sel = _as_01(sel)
    return sum(jnp.dot(p, sel, preferred_element_type=F32) for p in _split3(x)[:terms])


def _mm_exact_lhs(sel, x):
    sel = _as_01(sel)
    return sum(jnp.dot(sel, p, preferred_element_type=F32) for p in _split3(x))


def _sigmoid(x):
    return 1.0 / (1.0 + jnp.exp(-x))


def _silu(x):
    return x * _sigmoid(x)


def _softplus(x):
    return jnp.maximum(x, 0.0) + jnp.log(1.0 + jnp.exp(-jnp.abs(x)))


def _iota(shape, dim):
    return lax.broadcasted_iota(jnp.int32, shape, dim)


def _head_of_lane(width):
    return jnp.right_shift(_iota((1, width), 1), 6)


def _same_head(rows, cols):
    return jnp.right_shift(_iota((rows, cols), 0), 6) == jnp.right_shift(_iota((rows, cols), 1), 6)


def _expand_heads(cols, width):
    head = _head_of_lane(width)
    out = jnp.broadcast_to(cols[0], (cols[0].shape[0], width))
    for h in range(1, len(cols)):
        out = jnp.where(head == h, cols[h], out)
    return out


def _stack(fn):
    return jnp.concatenate([fn(i) for i in range(NB)], axis=0)


def _round_robin(chains):
    live = list(chains)
    while live:
        for chain in list(live):
            try:
                next(chain)
            except StopIteration:
                live.remove(chain)


def _tile_of(d, j, nt):
    return jnp.where(d == 0, j, jnp.where(j == 0, 0, nt - j))


def _norm_mod(x, w, scale, shift):
    y = x * lax.rsqrt(jnp.mean(x * x, axis=-1, keepdims=True) + EPS) * w
    return y * (1.0 + scale) + shift


def _order_mask(rev, n):
    diff = (_iota((n, n), 1) - _iota((n, n), 0)) * jnp.where(rev, -1, 1)
    return diff <= 0


def _const_spec(arr, n_grid, single=False):
    kwargs = {"pipeline_mode": pl.Buffered(1)} if single else {}
    return pl.BlockSpec(arr.shape, lambda *_: (0,) * arr.ndim, **kwargs)


def _mod_spec(d, first):
    return pl.BlockSpec((NB, None, 6, d), lambda bp, j: (bp, jnp.minimum(first + j, 1), 0, 0))


def _adaln_kernel(c_ref, w_ref, b_ref, o_ref):
    o_ref[...] = _mm_f32(_silu(c_ref[...]), w_ref[...]) + b_ref[...]


def _adaln(cc, ada_w, ada_b):
    depth, d, n = ada_w.shape
    tn = 768
    return pl.pallas_call(
        _adaln_kernel,
        grid=(depth, n // tn),
        in_specs=[pl.BlockSpec(cc.shape, lambda l, i: (0, 0)),
                  pl.BlockSpec((None, d, tn), lambda l, i: (l, 0, i)),
                  pl.BlockSpec((None, 1, tn), lambda l, i: (l, 0, i))],
        out_specs=pl.BlockSpec((None, cc.shape[0], tn), lambda l, i: (l, 0, i)),
        out_shape=jax.ShapeDtypeStruct((depth, cc.shape[0], n), F32),
        name="adaln",
    )(cc, ada_w, ada_b.reshape(depth, 1, n))


def _stream_rows(x_ref, c_ref, i):
    if c_ref is None:
        return x_ref[i]
    return jnp.where(pl.program_id(1) == 0, c_ref[i], x_ref[i])


def _stream_specs(d, ctx, first):
    if ctx is None:
        return [pl.BlockSpec((NB, TM, d), lambda bp, j: (bp, first + j, 0))]
    return [pl.BlockSpec((NB, TM, d), lambda bp, j: (bp, jnp.maximum(j - 1, 0), 0)),
            pl.BlockSpec((NB, TM, d), lambda bp, j: (bp, 0, 0))]


def _proj_kernel(*refs, split):
    x_ref, c_ref = (refs[0], refs[1]) if split else (refs[0], None)
    mod_ref, nw_ref, w_ref, *out_refs = refs[2:] if split else refs[1:]
    h = _stack(lambda i: _norm_mod(_stream_rows(x_ref, c_ref, i), nw_ref[...], mod_ref[i, 1:2, :],
                                   mod_ref[i, 0:1, :]))
    h = h.astype(MXU_DTYPE)
    off = 0
    for pos, (ref, n) in enumerate(zip(out_refs, PROJ_WIDTHS)):
        res = jnp.dot(h, w_ref[:, off:off + n], preferred_element_type=F32)
        if pos == SU_POS:
            ref[...] = jnp.concatenate([res[i * TM:(i + 1) * TM, :] for i in range(NB)], axis=1)
        else:
            for i in range(NB):
                ref[i] = res[i * TM:(i + 1) * TM, :]
        off += n


def _proj(seq, mod, norm_w, w_proj, ctx=None):
    b, s, d = seq.shape
    s = s if ctx is None else s + TM
    nt = s // TM

    def tok(width):
        return pl.BlockSpec((NB, TM, width), lambda bp, j: (bp, j, 0))

    out_specs = [tok(w) for w in PROJ_WIDTHS]
    out_shape = [jax.ShapeDtypeStruct((b, s, w), F32) for w in PROJ_WIDTHS]
    out_specs[SU_POS] = pl.BlockSpec((TM, NB * W_SU), lambda bp, j: (j, bp))
    out_shape[SU_POS] = jax.ShapeDtypeStruct((s, b * W_SU), F32)
    streams = [seq] if ctx is None else [seq, ctx]
    return pl.pallas_call(
        functools.partial(_proj_kernel, split=ctx is not None),
        grid=(b // NB, nt),
        in_specs=_stream_specs(d, ctx, 0) + [_mod_spec(d, 0), _const_spec(norm_w, 2),
                                             _const_spec(w_proj, 2, single=True)],
        out_specs=out_specs,
        out_shape=out_shape,
        compiler_params=pltpu.CompilerParams(vmem_limit_bytes=VMEM_LIMIT),
        name="proj",
    )(*streams, mod, norm_w, w_proj)


def _prep_kernel(cv_ref, cvp_ref, cvn_ref, cw_ref, cb_ref, cos_ref, sin_ref, sm_ref, gb_ref, al_ref, perm_ref,
                 pa_ref, pb_ref, one_ref, shift_ref, qk_ref, xbc_ref, gc_ref, gr_ref, la_ref, lb_ref, *, nt):
    t = pl.program_id(1)
    prev_ok = t > 1
    next_ok = (t >= 1) & (t < nt - 1)
    lane = _iota((1, W_SMALL), 1)
    first = jnp.bitwise_and(_iota((1, W_QK), 1), 31) < 16
    r = _iota((TM, TM), 0)
    c = _iota((TM, TM), 1)
    same_chunk = jnp.right_shift(r, 7) == jnp.right_shift(c, 7)
    cum = (lane >= G_BCUM) & (lane < G_DT)

    mid = CONV_K // 2
    side_taps = [k for k in range(CONV_K) if k != mid]

    def edge_rows(ext24):
        out = cb_ref[...] + cw_ref[mid:mid + 1, :] * ext24[HALO:2 * HALO, :]
        for k in side_taps:
            out = out + cw_ref[k:k + 1, :] * pltpu.roll(ext24, (mid - k) % (3 * HALO), 0)[HALO:2 * HALO, :]
        return out

    for i in range(NB):
        cur = cv_ref[i]
        weighted = jnp.concatenate([(cw_ref[k:k + 1, :] * cur).astype(MXU_DTYPE) for k in side_taps], axis=0)
        acc = (cb_ref[...] + cw_ref[mid:mid + 1, :] * cur
               + jnp.dot(shift_ref[...], weighted, preferred_element_type=F32))
        top = edge_rows(jnp.concatenate([jnp.where(prev_ok, cvp_ref[i], 0.0), cur[0:2 * HALO, :]], axis=0))
        bot = edge_rows(jnp.concatenate([cur[TM - 2 * HALO:, :], jnp.where(next_ok, cvn_ref[i], 0.0)], axis=0))
        acc = _silu(jnp.concatenate([top, acc[HALO:TM - HALO, :], bot], axis=0))
        xbc_ref[i] = acc[:, W_QK:]
        qk = acc[:, :W_QK]
        partner = jnp.where(first, pltpu.roll(qk, W_QK - 16, 1), pltpu.roll(qk, 16, 1))
        qk_ref[i] = qk * cos_ref[...] + partner * sin_ref[...]

        g_all = sm_ref[i] + gb_ref[...]
        dt = _softplus(g_all)
        src = jnp.where(lane < L_MF, g_all,
                        jnp.where(lane < L_DDT, -_softplus(-g_all), dt * (-jnp.exp(al_ref[...]))))
        src = jnp.concatenate([src, dt], axis=1)
        for d in range(2):
            tri = same_chunk & ((c <= r) if d == 0 else (c >= r))
            g = _mm_exact_rhs(src, perm_ref[d])
            g = jnp.where(cum, _mm_exact_lhs(tri, jnp.where(cum, g, 0.0)), g)
            gc_ref[d, i] = g
            a_mat = one_ref[0:1, :] + sum(jnp.dot(p, pa_ref[k], preferred_element_type=F32)
                                           for k, p in enumerate(_split3(g)))
            cv = jnp.where(lane < MLSTM_HEADS, g - pltpu.roll(g, W_SMALL - G_BCUM, 1), -g)
            b_mat = one_ref[1:2, :] + sum(jnp.dot(p, pb_ref[k], preferred_element_type=F32)
                                           for k, p in enumerate(_split3(cv)))
            la_ref[d, i] = a_mat.astype(MXU_DTYPE)
            for ci in range(TM // SUB):
                gr_ref[d, i, ci] = g[ci * SUB:(ci + 1) * SUB, :].T
                lb_ref[d, i, ci] = b_mat[ci * SUB:(ci + 1) * SUB, :].T.astype(MXU_DTYPE)


def _prep(cv, small, conv_w, conv_b, cos_t, sin_t, gbias, alog, perm, pa, pb, ones_ab, shifts):
    b, s, _ = cv.shape
    nt = s // TM
    per = TM // HALO
    last = s // HALO - 1
    nsub = TM // SUB
    return pl.pallas_call(
        functools.partial(_prep_kernel, nt=nt),
        grid=(b // NB, nt),
        in_specs=[pl.BlockSpec((NB, TM, W_CV), lambda bp, j: (bp, j, 0)),
                  pl.BlockSpec((NB, HALO, W_CV), lambda bp, j: (bp, jnp.maximum(j * per - 1, 0), 0)),
                  pl.BlockSpec((NB, HALO, W_CV), lambda bp, j: (bp, jnp.minimum((j + 1) * per, last), 0)),
                  _const_spec(conv_w, 2), _const_spec(conv_b, 2),
                  pl.BlockSpec((TM, W_QK), lambda bp, j: (j, 0)),
                  pl.BlockSpec((TM, W_QK), lambda bp, j: (j, 0)),
                  pl.BlockSpec((NB, TM, W_SMALL), lambda bp, j: (bp, j, 0)),
                  _const_spec(gbias, 2), _const_spec(alog, 2), _const_spec(perm, 2),
                  _const_spec(pa, 2), _const_spec(pb, 2), _const_spec(ones_ab, 2), _const_spec(shifts, 2)],
        out_specs=[pl.BlockSpec((NB, TM, W_QK), lambda bp, j: (bp, j, 0)),
                   pl.BlockSpec((NB, TM, W_XBC), lambda bp, j: (bp, j, 0)),
                   pl.BlockSpec((2, NB, TM, W_SMALL), lambda bp, j: (0, bp, j, 0)),
                   pl.BlockSpec((2, NB, nsub, W_SMALL, SUB), lambda bp, j: (0, bp, j, 0, 0)),
                   pl.BlockSpec((2, NB, TM, W_SMALL), lambda bp, j: (0, bp, j, 0)),
                   pl.BlockSpec((2, NB, nsub, W_SMALL, SUB), lambda bp, j: (0, bp, j, 0, 0))],
        out_shape=[jax.ShapeDtypeStruct((b, s, W_QK), F32),
                   jax.ShapeDtypeStruct((b, s, W_XBC), F32),
                   jax.ShapeDtypeStruct((2, b, s, W_SMALL), F32),
                   jax.ShapeDtypeStruct((2, b, s // SUB, W_SMALL, SUB), F32),
                   jax.ShapeDtypeStruct((2, b, s, W_SMALL), MXU_DTYPE),
                   jax.ShapeDtypeStruct((2, b, s // SUB, W_SMALL, SUB), MXU_DTYPE)],
        compiler_params=pltpu.CompilerParams(vmem_limit_bytes=VMEM_LIMIT),
        name="prep",
    )(cv, cv, cv, conv_w, conv_b, cos_t, sin_t, small, gbias, alog, perm, pa, pb, ones_ab, shifts)


def _scan_specs(nt, widths, kinds):
    nsub = TM // SUB
    specs = [pl.BlockSpec((NBS, TM, w), lambda d, j, bp: (bp, _tile_of(d, j, nt), 0)) for w in widths]
    for kind in kinds:
        if kind == "col":
            specs.append(pl.BlockSpec((None, NBS, TM, W_SMALL), lambda d, j, bp: (d, bp, _tile_of(d, j, nt), 0)))
        else:
            specs.append(pl.BlockSpec((None, NBS, nsub, W_SMALL, SUB),
                                      lambda d, j, bp: (d, bp, _tile_of(d, j, nt), 0, 0)))
    return specs


def _mlstm_kernel(qk_ref, v_ref, gc_ref, la_ref, lb_ref, h_ref, c_scr, n_scr, m_scr):
    d = pl.program_id(0)
    j = pl.program_id(1)
    bp = pl.program_id(2)
    rev = d == 1
    width = MLSTM_HEADS * HEAD_DIM
    nsub = TM // SUB
    heads = list(range(MLSTM_HEADS))
    stacked = (MLSTM_HEADS * SUB, SUB)
    t_minus_s = (jnp.bitwise_and(_iota(stacked, 0), SUB - 1) - _iota(stacked, 1)) * jnp.where(rev, -1, 1)
    mask_s = t_minus_s >= 0
    head = _head_of_lane(width)
    same_head = _same_head(width, width)
    lane = _iota((1, W_SMALL), 1)
    valid = lane < MLSTM_HEADS
    group = jnp.right_shift(lane, 3)
    to_heads = _as_01(jnp.right_shift(_iota((W_SMALL, width), 1), 6) == _iota((W_SMALL, width), 0))
    from_heads = _as_01(jnp.right_shift(_iota((width, W_SMALL), 0), 6) == _iota((width, W_SMALL), 1))
    ones_rows = jnp.ones((8, SUB), MXU_DTYPE)

    @pl.when(j == 0)
    def _():
        for i in range(NBS):
            c_scr[bp * NBS + i] = jnp.zeros(c_scr.shape[1:], F32)
            n_scr[bp * NBS + i] = jnp.zeros(n_scr.shape[1:], F32)
            m_scr[bp * NBS + i] = jnp.zeros(m_scr.shape[1:], F32)

    def chain(i):
        bi = bp * NBS + i
        c_st = c_scr[bi]
        n_st = n_scr[bi, 0:1, :]
        m_st = m_scr[bi, 0:1, :]
        chunks = []
        for step in range(nsub):
            ci = jnp.where(rev, nsub - 1 - step, step)
            rows = pl.ds(pl.multiple_of(ci * SUB, SUB), SUB)
            q = qk_ref[i, rows, 0:width]
            k = qk_ref[i, rows, width:2 * width]
            qb = q.astype(MXU_DTYPE)
            kb = k.astype(MXU_DTYPE)
            vb = v_ref[i, rows, :].astype(MXU_DTYPE)
            li = gc_ref[i, rows, :]
            bc = pltpu.roll(li, W_SMALL - G_BCUM, 1)
            la = la_ref[i, rows, :]
            lb = lb_ref[i, ci]

            la_s = jnp.concatenate([jnp.where(group == h, la, 0) for h in heads], axis=0)
            tiles = [slice((h // 2) * 128, (h // 2 + 1) * 128) for h in heads]
            q_s = [jnp.where(head[:, tiles[h]] == h, qb[:, tiles[h]], 0) for h in heads]
            log_w = jnp.where(mask_s, jnp.dot(la_s, lb, preferred_element_type=F32), NEG)
            yield
            m_in_s = jnp.max(log_w, axis=1, keepdims=True)
            qk = jnp.concatenate([_mm_nt(jnp.concatenate(q_s[2 * p:2 * p + 2], axis=0), kb[:, tiles[2 * p]])
                                  for p in range(MLSTM_HEADS // 2)], axis=0)
            sm = qk * jnp.exp(log_w - m_in_s)
            rs = jnp.sum(sm, axis=1, keepdims=True)
            sm = sm.astype(MXU_DTYPE)
            yield
            pv = jnp.dot(sm, vb, preferred_element_type=F32)
            yield
            num0 = pv[0:SUB, :]
            den0 = jnp.broadcast_to(rs[0:SUB, :], (SUB, W_SMALL))
            m_in = jnp.broadcast_to(m_in_s[0:SUB, :], (SUB, W_SMALL))
            for h in heads[1:]:
                blk = slice(h * SUB, (h + 1) * SUB)
                num0 = jnp.where(head == h, pv[blk, :], num0)
                den0 = jnp.where(lane == h, rs[blk, :], den0)
                m_in = jnp.where(lane == h, m_in_s[blk, :], m_in)
            b_last = jnp.where(rev, bc[0:1, :], bc[SUB - 1:SUB, :])
            log_k = b_last - bc + li
            mk = jnp.max(log_k, axis=0, keepdims=True)
            wk_full = _mm_exact_rhs(jnp.where(valid, jnp.exp(log_k - mk), 0.0), to_heads, terms=2)
            kw = (k * wk_full).astype(MXU_DTYPE)
            yield
            kv0 = jnp.where(same_head, _mm_tn(kw, vb), 0.0)
            ks0 = jnp.dot(ones_rows, kw, preferred_element_type=F32)[0:1, :]
            chunks.append((rows, q, qb, bc, num0, den0, m_in, b_last, mk, kv0, ks0))
            yield

        for rows, q, qb, bc, num0, den0, m_in, b_last, mk, kv0, ks0 in chunks:
            inter = bc + m_st
            m_t = jnp.maximum(inter, m_in)
            r = jnp.exp(m_in - m_t)
            g = jnp.exp(inter - m_t)
            den = r * den0 + g * jnp.dot((q * n_st).astype(MXU_DTYPE), from_heads, preferred_element_type=F32)
            inv = 1.0 / jnp.maximum(jnp.abs(den), jnp.exp(-m_t))
            ir_full = _mm_exact_rhs(jnp.where(valid, inv * r, 0.0), to_heads, terms=2)
            ig_full = _mm_exact_rhs(jnp.where(valid, inv * g, 0.0), to_heads, terms=2)
            h_ref[i, rows, :] = num0 * ir_full + _mm(qb, c_st) * ig_full
            yield
            m_new = jnp.maximum(b_last + m_st, mk)
            scales = jnp.concatenate([jnp.exp(b_last + m_st - m_new), jnp.exp(mk - m_new),
                                      jnp.zeros((6, W_SMALL), F32)], axis=0)
            scales_full = _mm_exact_rhs(jnp.where(valid, scales, 0.0), to_heads)
            c_st = c_st * scales_full[0:1, :] + kv0 * scales_full[1:2, :]
            n_st = n_st * scales_full[0:1, :] + ks0 * scales_full[1:2, :]
            m_st = jnp.where(valid, m_new, 0.0)
            yield

        c_scr[bi] = c_st
        n_scr[bi, 0:1, :] = n_st
        m_scr[bi, 0:1, :] = m_st

    _round_robin([chain(i) for i in range(NBS)])


def _mlstm(qk, v, gc, la, lb):
    b, s, _ = qk.shape
    nt = s // TM
    width = MLSTM_HEADS * HEAD_DIM
    return pl.pallas_call(
        _mlstm_kernel,
        grid=(2, nt, b // NBS),
        in_specs=_scan_specs(nt, (2 * width, width), ("col", "col", "row")),
        out_specs=pl.BlockSpec((None, NBS, TM, width), lambda d, j, bp: (d, bp, _tile_of(d, j, nt), 0)),
        out_shape=jax.ShapeDtypeStruct((2, b, s, width), F32),
        scratch_shapes=[pltpu.VMEM((b, width, width), F32), pltpu.VMEM((b, 8, width), F32),
                        pltpu.VMEM((b, 8, W_SMALL), F32)],
        compiler_params=pltpu.CompilerParams(vmem_limit_bytes=VMEM_LIMIT),
        name="mlstm",
    )(qk, v, gc, la, lb)


def _s5_kernel(u_ref, lre_ref, lim_ref, ldt_ref, bre_ref, bim_ref, cw_ref, *rest, nb, rev):
    acc_ref = rest[0] if rev else None
    y_ref, wbu_scr, are_scr, aim_scr, hre_scr, him_scr, hb0, hb1, hb2 = rest[1:] if rev else rest
    j = pl.program_id(0)
    n_state = S5_GROUPS * S5_STATE

    @pl.when(j == 0)
    def _():
        lre = lre_ref[...]
        lim = lim_ref[...]
        dt = jnp.exp(ldt_ref[...])
        mag = jnp.exp(lre * dt)
        a_re = mag * jnp.cos(lim * dt)
        a_im = mag * jnp.sin(lim * dt)
        den = lre * lre + lim * lim
        nr = a_re - 1.0
        coef_re = (nr * lre + a_im * lim) / den
        coef_im = (a_im * lre - nr * lim) / den
        wbu_scr[:, :n_state] = (coef_re * bre_ref[...] - coef_im * bim_ref[...]).astype(MXU_DTYPE)
        wbu_scr[:, n_state:] = (coef_re * bim_ref[...] + coef_im * bre_ref[...]).astype(MXU_DTYPE)
        are_scr[...] = jnp.broadcast_to(a_re, are_scr.shape)
        aim_scr[...] = jnp.broadcast_to(a_im, aim_scr.shape)
        hre_scr[...] = jnp.zeros(hre_scr.shape, F32)
        him_scr[...] = jnp.zeros(him_scr.shape, F32)

    a_re = are_scr[...]
    a_im = aim_scr[...]
    rows_sub = S5_SUB * nb
    n_sub = TM // S5_SUB
    n_part = 1
    part = rows_sub // n_part
    bufs = (hb0, hb1, hb2)
    state = [hre_scr[...], him_scr[...]]

    def first_row(sb):
        return (n_sub - 1 - sb if rev else sb) * rows_sub

    def project(sb):
        for p in range(n_part):
            lhs = u_ref[first_row(sb) + p * part:first_row(sb) + (p + 1) * part, :].astype(MXU_DTYPE)
            bufs[sb % 3][p * part:(p + 1) * part, :] = jnp.dot(lhs, wbu_scr[...], preferred_element_type=F32)
            yield

    def recur(sb):
        buf = bufs[sb % 3]
        h_re, h_im = state
        for t in range(S5_SUB):
            ti = S5_SUB - 1 - t if rev else t
            r = slice(ti * nb, (ti + 1) * nb)
            n_re = a_re * h_re - a_im * h_im + buf[r, :n_state]
            n_im = a_re * h_im + a_im * h_re + buf[r, n_state:]
            buf[r, :n_state] = n_re
            buf[r, n_state:] = n_im
            h_re, h_im = n_re, n_im
            if t % 4 == 3:
                yield
        state[0], state[1] = h_re, h_im

    def readout(sb):
        for p in range(n_part):
            lhs = bufs[sb % 3][p * part:(p + 1) * part, :].astype(MXU_DTYPE)
            rows = slice(first_row(sb) + p * part, first_row(sb) + (p + 1) * part)
            y = jnp.dot(lhs, cw_ref[...], preferred_element_type=F32)
            y_ref[rows, :] = y + acc_ref[rows, :] if rev else y
            yield

    for stage in range(n_sub + 2):
        live = []
        if stage < n_sub:
            live.append(project(stage))
        if 1 <= stage <= n_sub:
            live.append(recur(stage - 1))
        if stage >= 2:
            live.append(readout(stage - 2))
        _round_robin(live)
    hre_scr[...] = state[0]
    him_scr[...] = state[1]


def _s5(u_tm, lam_re, lam_im, log_dt, braw_re, braw_im, cw, nb, fwd_out=None):
    rev = fwd_out is not None
    rows, width = u_tm.shape
    s = rows // nb
    nt = s // TM
    n_state = S5_GROUPS * S5_STATE
    d = 1 if rev else 0
    hbuf = pltpu.VMEM((S5_SUB * nb, 2 * n_state), F32)
    return pl.pallas_call(
        functools.partial(_s5_kernel, nb=nb, rev=rev),
        grid=(nt,),
        in_specs=[pl.BlockSpec((TM * nb, width), lambda j: (_tile_of(d, j, nt), 0)),
                  _const_spec(lam_re, 1), _const_spec(lam_im, 1), _const_spec(log_dt, 1),
                  _const_spec(braw_re, 1), _const_spec(braw_im, 1), _const_spec(cw, 1)]
        + ([pl.BlockSpec((TM * nb, width), lambda j: (_tile_of(d, j, nt), 0))] if rev else []),
        out_specs=pl.BlockSpec((TM * nb, width), lambda j: (_tile_of(d, j, nt), 0)),
        out_shape=jax.ShapeDtypeStruct((rows, width), F32),
        scratch_shapes=[pltpu.VMEM((width, 2 * n_state), MXU_DTYPE),
                        pltpu.VMEM((nb, n_state), F32), pltpu.VMEM((nb, n_state), F32),
                        pltpu.VMEM((nb, n_state), F32), pltpu.VMEM((nb, n_state), F32),
                        hbuf, hbuf, hbuf],
        compiler_params=pltpu.CompilerParams(vmem_limit_bytes=VMEM_LIMIT),
        name="s5_bwd" if rev else "s5_fwd",
    )(u_tm, lam_re, lam_im, log_dt, braw_re, braw_im, cw, *([fwd_out] if rev else []))


def _na_kernel(q_ref, k_ref, v_ref, tbl_ref, o_ref, *, first_tile, n_rows):
    t = first_tile + pl.program_id(1)
    width = NA_HEADS * HEAD_DIM
    head = _head_of_lane(width)
    scale = HEAD_DIM ** -0.5
    k_ctx = k_ref[0:TM, :].astype(MXU_DTYPE)
    v_ctx = v_ref[0:TM, :].astype(MXU_DTYPE)

    def stack_heads(q):
        return jnp.concatenate([jnp.where(head == h, q, 0.0) for h in range(NA_HEADS)], axis=0).astype(MXU_DTYPE)

    def unstack_heads(o, n):
        acc = o[0:n, :]
        for h in range(1, NA_HEADS):
            acc = jnp.where(head == h, o[h * n:(h + 1) * n, :], acc)
        return acc

    @pl.when(t == 0)
    def _():
        s = _mm_nt(stack_heads(q_ref[...] * scale), k_ctx)
        p = jnp.exp(s - jnp.max(s, axis=1, keepdims=True))
        o = _mm(p, v_ctx) / jnp.sum(p, axis=1, keepdims=True)
        o_ref[...] = unstack_heads(o, TM)

    @pl.when(t > 0)
    def _():
        rows_per_tile = TM // GRID_W
        n_lat = NA_KH * GRID_W
        def grid_row(rr):
            r = (t - 1) * rows_per_tile + rr
            row_start = jnp.clip(r - NA_KH // 2, 0, n_rows - NA_KH)
            off = (NA_KH - 1) - (r - row_start)
            win = pl.ds(pl.multiple_of(TM + row_start * GRID_W, GRID_W), n_lat)
            qs = stack_heads(q_ref[rr * GRID_W:(rr + 1) * GRID_W, :] * scale)
            s_lat = _mm_nt(qs, k_ref[win, :]) + tbl_ref[off]
            s_ctx = _mm_nt(qs, k_ctx)
            yield
            m = jnp.maximum(jnp.max(s_lat, axis=1, keepdims=True), jnp.max(s_ctx, axis=1, keepdims=True))
            p_lat = jnp.exp(s_lat - m)
            p_ctx = jnp.exp(s_ctx - m)
            den = jnp.sum(p_lat, axis=1, keepdims=True) + jnp.sum(p_ctx, axis=1, keepdims=True)
            yield
            o = (_mm(p_lat, v_ref[win, :]) + _mm(p_ctx, v_ctx)) / den
            o_ref[rr * GRID_W:(rr + 1) * GRID_W, :] = unstack_heads(o, GRID_W)
            yield

        _round_robin([grid_row(rr) for rr in range(rows_per_tile)])


def _na(nqkv, tbl, with_ctx):
    b, s, _ = nqkv.shape
    nt = s // TM
    width = NA_HEADS * HEAD_DIM
    first_tile = 0 if with_ctx else 1
    n_rows = (s - TM) // GRID_W
    return pl.pallas_call(
        functools.partial(_na_kernel, first_tile=first_tile, n_rows=n_rows),
        grid=(b, nt - first_tile),
        in_specs=[pl.BlockSpec((None, TM, width), lambda bi, j: (bi, first_tile + j, 0)),
                  pl.BlockSpec((None, s, width), lambda bi, j: (bi, 0, 1)),
                  pl.BlockSpec((None, s, width), lambda bi, j: (bi, 0, 2)),
                  _const_spec(tbl, 2)],
        out_specs=pl.BlockSpec((None, TM, width), lambda bi, j: (bi, first_tile + j, 0)),
        out_shape=jax.ShapeDtypeStruct((b, s, width), F32),
        compiler_params=pltpu.CompilerParams(vmem_limit_bytes=VMEM_LIMIT),
        name="na",
    )(nqkv, nqkv, nqkv, tbl)


def _ssd_kernel(x_ref, gc_ref, gr_ref, la_ref, lb_ref, dsk_ref, y_ref, st_scr):
    d = pl.program_id(0)
    j = pl.program_id(1)
    bp = pl.program_id(2)
    rev = d == 1
    width = SSD_HEADS * HEAD_DIM
    gn = SSD_GROUPS * SSD_STATE
    per_group = SSD_HEADS // SSD_GROUPS
    gw = per_group * HEAD_DIM
    nsub = TM // SUB
    stacked = (per_group * SUB, SUB)
    t_minus_s = (jnp.bitwise_and(_iota(stacked, 0), SUB - 1) - _iota(stacked, 1)) * jnp.where(rev, -1, 1)
    mask_s = t_minus_s >= 0
    head = _head_of_lane(width)
    skip = jnp.where(rev, 0.0, 1.0) * dsk_ref[...]
    lane = _iota((1, W_SMALL), 1)
    valid = (lane >= G_ACS) & (lane < G_ACS + SSD_HEADS)
    group = jnp.right_shift(lane, 3)
    to_heads = _as_01(jnp.right_shift(_iota((W_SMALL, width), 1), 6) == _iota((W_SMALL, width), 0) - G_ACS)

    @pl.when(j == 0)
    def _():
        for i in range(NBS):
            st_scr[bp * NBS + i] = jnp.zeros(st_scr.shape[1:], F32)

    def chain(i):
        bi = bp * NBS + i
        st = st_scr[bi]
        chunks = []
        for step in range(nsub):
            ci = jnp.where(rev, nsub - 1 - step, step)
            rows = pl.ds(pl.multiple_of(ci * SUB, SUB), SUB)
            xs = x_ref[i, rows, 0:width]
            bm = x_ref[i, rows, width:width + gn].astype(MXU_DTYPE)
            cm = x_ref[i, rows, width + gn:width + 2 * gn].astype(MXU_DTYPE)
            xsb = xs.astype(MXU_DTYPE)
            acs = gc_ref[i, rows, :]
            dt = pltpu.roll(acs, W_SMALL - (G_DT - G_ACS), 1)
            grow = gr_ref[i, ci]
            la = la_ref[i, rows, :]
            lb = lb_ref[i, ci]

            a_last = jnp.where(rev, acs[0:1, :], acs[SUB - 1:SUB, :])
            e_full = _mm_exact_rhs(jnp.where(valid, jnp.exp(acs), 0.0), to_heads, terms=2)
            xw = xs * _mm_exact_rhs(jnp.where(valid, jnp.exp(a_last - acs) * dt, 0.0), to_heads, terms=2)
            last = jnp.broadcast_to(jnp.where(valid, jnp.exp(a_last), 0.0), (8, W_SMALL))
            last_full = _mm_exact_rhs(last, to_heads)[0:1, :]
            yield

            y_parts, st_parts, c_parts = [], [], []
            for g in range(SSD_GROUPS):
                b_g = bm[:, g * SSD_STATE:(g + 1) * SSD_STATE]
                c_g = cm[:, g * SSD_STATE:(g + 1) * SSD_STATE]
                lanes = slice(g * gw, (g + 1) * gw)
                hs = range(g * per_group, (g + 1) * per_group)
                la_s = jnp.concatenate([jnp.where(group == MLSTM_HEADS + h, la, 0) for h in hs], axis=0)
                dt_s = jnp.concatenate([jnp.broadcast_to(grow[G_DT + h:G_DT + h + 1, :], (SUB, SUB)) for h in hs],
                                       axis=0)
                decay = jnp.exp(jnp.where(mask_s, jnp.dot(la_s, lb, preferred_element_type=F32), NEG)) * dt_s
                cb = _mm_nt(c_g, b_g)
                m_s = (jnp.concatenate([cb] * per_group, axis=0) * decay).astype(MXU_DTYPE)
                yield
                yd = jnp.dot(m_s, xsb[:, lanes], preferred_element_type=F32)
                y_g = yd[0:SUB, :]
                for hh in range(1, per_group):
                    y_g = jnp.where(head[:, lanes] == g * per_group + hh, yd[hh * SUB:(hh + 1) * SUB, :], y_g)
                y_parts.append(y_g)
                st_parts.append(_mm_tn(b_g, xw[:, lanes]))
                c_parts.append(c_g)
                yield
            chunks.append((rows, jnp.concatenate(y_parts, axis=1) + skip * xs, jnp.concatenate(st_parts, axis=1),
                           c_parts, e_full, last_full))

        for rows, y_in, st_in, c_parts, e_full, last_full in chunks:
            y_off = jnp.concatenate([_mm(c_parts[g], st[:, g * gw:(g + 1) * gw]) for g in range(SSD_GROUPS)], axis=1)
            y_ref[i, rows, :] = y_in + y_off * e_full
            st = st * last_full + st_in
            yield
        st_scr[bi] = st

    _round_robin([chain(i) for i in range(NBS)])


def _ssd(xbc, gc, gr, la, lb, dskip):
    b, s, cw = xbc.shape
    nt = s // TM
    width = SSD_HEADS * HEAD_DIM
    return pl.pallas_call(
        _ssd_kernel,
        grid=(2, nt, b // NBS),
        in_specs=_scan_specs(nt, (cw,), ("col", "row", "col", "row")) + [_const_spec(dskip, 3)],
        out_specs=pl.BlockSpec((None, NBS, TM, width), lambda d, j, bp: (d, bp, _tile_of(d, j, nt), 0)),
        out_shape=jax.ShapeDtypeStruct((2, b, s, width), F32),
        scratch_shapes=[pltpu.VMEM((b, SSD_STATE, width), F32)],
        compiler_params=pltpu.CompilerParams(vmem_limit_bytes=VMEM_LIMIT),
        name="ssd",
    )(xbc, gc, gr, la, lb, dskip)


def _merge_kernel(*refs, split):
    x_ref, c_ref = (refs[0], refs[1]) if split else (refs[0], None)
    (mod_ref, nw_ref, hm_ref, mo_ref, mnw_ref, ys_ref, su_ref, s5d_ref, glu_ref, na_ref, yd_ref, dz_ref, dnw_ref,
     wg_ref, wa_ref, wb_ref, wc_ref, wd_ref, wo_ref, o_ref) = refs[2:] if split else refs[1:]
    d_model = x_ref.shape[2]
    xs = [_stream_rows(x_ref, c_ref, i) for i in range(NB)]
    h = _stack(lambda i: _norm_mod(xs[i], nw_ref[...], mod_ref[i, 1:2, :], mod_ref[i, 0:1, :]))
    h = h.astype(MXU_DTYPE)

    wm = MLSTM_HEADS * HEAD_DIM
    hm = _stack(lambda i: (hm_ref[0, i] + hm_ref[1, i]) * _sigmoid(mo_ref[i]))
    ms = _mm_exact_rhs(hm * hm, _same_head(wm, wm)) * (1.0 / HEAD_DIM)
    ya = hm * lax.rsqrt(ms + EPS) * mnw_ref[...]

    ws = S5_GROUPS * S5_GROUP
    ys = _stack(lambda i: (ys_ref[:, i * ws:(i + 1) * ws]
                           + s5d_ref[...] * su_ref[:, i * ws:(i + 1) * ws]))
    ys = 0.5 * ys * (1.0 + jnp.tanh(math.sqrt(2.0 / math.pi) * (ys + 0.044715 * (ys * ys * ys))))
    ab = _mm(ys, glu_ref[...])
    yb = ab[:, :ws] * _sigmoid(ab[:, ws:])

    yc = _stack(lambda i: na_ref[i])

    yd = _stack(lambda i: (yd_ref[0, i] + yd_ref[1, i]) * _silu(dz_ref[i]))
    yd = yd * lax.rsqrt(jnp.mean(yd * yd, axis=-1, keepdims=True) + EPS) * dnw_ref[...]

    m = None
    for k, (y, w_ref) in enumerate(((ya, wa_ref), (yb, wb_ref), (yc, wc_ref), (yd, wd_ref))):
        gate = _sigmoid(jnp.dot(h, wg_ref[:, k * d_model:(k + 1) * d_model], preferred_element_type=F32))
        term = gate * _mm(y, w_ref[...])
        m = term if m is None else m + term
    out = _mm(m, wo_ref[...])
    for i in range(NB):
        o_ref[i] = xs[i] + mod_ref[i, 2:3, :] * out[i * TM:(i + 1) * TM, :]


def _merge(seq, mod, norm_w, hm, mo, mnw, ys, su, s5d, glu_w, yna, yd, dz, dnw, wg, wa, wb, wc, wd, wo,
           with_ctx, ctx=None):
    b, s, d = seq.shape
    s = s if ctx is None else s + TM
    nt = s // TM
    first = 0 if with_ctx else 1
    assert ctx is None or with_ctx

    def tok(width):
        return pl.BlockSpec((NB, TM, width), lambda bp, j: (bp, first + j, 0))

    def tok2(width):
        return pl.BlockSpec((2, NB, TM, width), lambda bp, j: (0, bp, first + j, 0))

    def const(arr):
        return _const_spec(arr, 2, single=True)

    ws = S5_GROUPS * S5_GROUP
    streams = [seq] if ctx is None else [seq, ctx]
    return pl.pallas_call(
        functools.partial(_merge_kernel, split=ctx is not None),
        grid=(b // NB, nt - first),
        in_specs=_stream_specs(d, ctx, first) + [_mod_spec(d, first), const(norm_w),
                  tok2(MLSTM_HEADS * HEAD_DIM), tok(W_O), const(mnw),
                  pl.BlockSpec((TM, NB * ws), lambda bp, j: (first + j, bp)),
                  pl.BlockSpec((TM, NB * ws), lambda bp, j: (first + j, bp)),
                  const(s5d), const(glu_w),
                  tok(NA_HEADS * HEAD_DIM),
                  tok2(SSD_HEADS * HEAD_DIM), tok(W_DZ), const(dnw),
                  const(wg), const(wa), const(wb), const(wc), const(wd), const(wo)],
        out_specs=tok(d),
        out_shape=jax.ShapeDtypeStruct((b, s, d), F32),
        input_output_aliases={0: 0} if ctx is None else {},
        compiler_params=pltpu.CompilerParams(vmem_limit_bytes=VMEM_LIMIT),
        name="merge",
    )(*streams, mod, norm_w, hm, mo, mnw, ys, su, s5d, glu_w, yna, yd, dz, dnw, wg, wa, wb, wc, wd, wo)


def _ffn_kernel(x_ref, mod_ref, nw_ref, wi_ref, wo_ref, fw_ref, o_ref, *, final):
    hidden = wo_ref.shape[0]
    h = _stack(lambda i: _norm_mod(x_ref[i], nw_ref[...], mod_ref[i, 4:5, :], mod_ref[i, 3:4, :]))
    h = h.astype(MXU_DTYPE)
    a = jnp.dot(h, wi_ref[:, :hidden], preferred_element_type=F32)
    g = jnp.dot(h, wi_ref[:, hidden:], preferred_element_type=F32)
    out = _mm(_silu(a) * g, wo_ref[...])
    for i in range(NB):
        y = x_ref[i] + mod_ref[i, 5:6, :] * out[i * TM:(i + 1) * TM, :]
        if final:
            y = y * lax.rsqrt(jnp.mean(y * y, axis=-1, keepdims=True) + EPS) * fw_ref[...]
        o_ref[i] = y


def _ffn(seq, mod, norm_w, wi, wo, final_w, final):
    b, s, d = seq.shape
    nt = s // TM
    first = 1 if final else 0

    def const(arr):
        return _const_spec(arr, 2, single=True)

    tok_in = pl.BlockSpec((NB, TM, d), lambda bp, j: (bp, first + j, 0))
    if final:
        out_spec = pl.BlockSpec((NB, TM, d), lambda bp, j: (bp, j, 0))
        out_shape = jax.ShapeDtypeStruct((b, s - TM, d), F32)
        aliases = {}
    else:
        out_spec = tok_in
        out_shape = jax.ShapeDtypeStruct((b, s, d), F32)
        aliases = {0: 0}
    return pl.pallas_call(
        functools.partial(_ffn_kernel, final=final),
        grid=(b // NB, nt - first),
        in_specs=[tok_in, _mod_spec(d, first), const(norm_w), const(wi), const(wo), const(final_w)],
        out_specs=out_spec,
        out_shape=out_shape,
        input_output_aliases=aliases,
        compiler_params=pltpu.CompilerParams(vmem_limit_bytes=VMEM_LIMIT),
        name="ffn_final" if final else "ffn",
    )(seq, mod, norm_w, wi, wo, final_w)


def _rope_tables(t, width):
    nf = HEAD_DIM // 4
    inv = (ROPE_THETA ** (-np.arange(nf, dtype=np.float32) / nf)).astype(np.float32)
    tok = np.arange(t)
    ang_r = (tok // GRID_W).astype(np.float32)[:, None] * inv
    ang_c = (tok % GRID_W).astype(np.float32)[:, None] * inv
    cos_h = np.concatenate([np.cos(ang_r)] * 2 + [np.cos(ang_c)] * 2, axis=1)
    sin_h = np.concatenate([-np.sin(ang_r), np.sin(ang_r), -np.sin(ang_c), np.sin(ang_c)], axis=1)
    cos_x = np.tile(cos_h, (1, width // HEAD_DIM))
    sin_x = np.tile(sin_h, (1, width // HEAD_DIM))
    cos_t = np.concatenate([np.ones((TM, width), np.float32), cos_x], axis=0)
    sin_t = np.concatenate([np.zeros((TM, width), np.float32), sin_x], axis=0)
    scale = np.float32(HEAD_DIM ** -0.5)
    return (jnp.asarray(np.concatenate([cos_t * scale, cos_t], axis=1), F32),
            jnp.asarray(np.concatenate([sin_t * scale, sin_t], axis=1), F32))


def _na_bias_tables(rpb):
    col = np.arange(GRID_W)
    col0 = np.clip(col - NA_KW // 2, 0, GRID_W - NA_KW)
    in_win = (col[None, :] >= col0[:, None]) & (col[None, :] < col0[:, None] + NA_KW)
    dc = np.clip(col[None, :] - col[:, None], -(NA_KW - 1), NA_KW - 1) + (NA_KW - 1)
    pick = jnp.asarray(dc[None] == np.arange(2 * NA_KW - 1)[:, None, None], F32)
    per_row = jnp.einsum('hrd,dqk->hrqk', rpb.astype(F32), pick, precision=HIGHEST)
    per_row = jnp.where(in_win, per_row, NEG)
    bias = jnp.stack([per_row[:, off:off + NA_KH] for off in range(NA_KH)], axis=0)
    return bias.transpose(0, 1, 3, 2, 4).reshape(NA_KH, NA_HEADS * GRID_W, NA_KH * GRID_W)


def _block_diag(blocks):
    g, r, c = blocks.shape
    eye = jnp.eye(g, dtype=blocks.dtype)
    return (eye[:, None, :, None] * blocks[:, :, None, :]).reshape(g * r, g * c)


def _lanes(vec, at, width=W_SMALL):
    return jnp.zeros((1, width), F32).at[0, at:at + vec.shape[0]].set(vec.astype(F32))


def _gate_perm():
    perm = np.zeros((2, 2 * W_SMALL, W_SMALL), np.float32)
    for d in range(2):
        for h in range(MLSTM_HEADS):
            perm[d, L_MI + d * MLSTM_HEADS + h, G_LI + h] = 1.0
            perm[d, L_MF + d * MLSTM_HEADS + h, G_BCUM + h] = 1.0
        for h in range(SSD_HEADS):
            perm[d, L_DDT + d * SSD_HEADS + h, G_ACS + h] = 1.0
            perm[d, W_SMALL + L_DDT + d * SSD_HEADS + h, G_DT + h] = 1.0
    return jnp.asarray(perm, MXU_DTYPE)


def _conv_shifts():
    t = np.arange(TM)[:, None]
    s = np.arange(TM)[None, :]
    blocks = [(s == t + k - CONV_K // 2) for k in range(CONV_K) if k != CONV_K // 2]
    return jnp.asarray(np.concatenate(blocks, axis=1), MXU_DTYPE)


def _log_decay_perms():
    pa = np.zeros((3, W_SMALL, W_SMALL), np.float32)
    pb = np.zeros((3, W_SMALL, W_SMALL), np.float32)
    ones_ab = np.zeros((2, W_SMALL), np.float32)
    slots = ([(G_BCUM + h, G_LI + h) for h in range(MLSTM_HEADS)]
             + [(G_ACS + h, G_ACS + h) for h in range(SSD_HEADS)])
    for n, (cum_lane, x_lane) in enumerate(slots):
        for k in range(3):
            pa[k, cum_lane, 8 * n + k] = 1.0
            pb[k, x_lane, 8 * n + 3 + k] = 1.0
            ones_ab[0, 8 * n + 3 + k] = 1.0
            ones_ab[1, 8 * n + k] = 1.0
    return jnp.asarray(pa, MXU_DTYPE), jnp.asarray(pb, MXU_DTYPE), jnp.asarray(ones_ab, F32)


def kernel(x, c, ctx, c_ctx, ada_w, ada_b, norm1_w, norm2_w, w_in, mlstm_conv_w, mlstm_conv_b, mlstm_ib, mlstm_fb, mlstm_norm_w, s5_lam_re, s5_lam_im, s5_log_dt, s5_b_re, s5_b_im, s5_c_re, s5_c_im, s5_d, s5_glu_w, na_rpb, ssd_conv_w, ssd_conv_b, ssd_a_log, ssd_dt_bias, ssd_d, ssd_norm_w, w_branch_a, w_branch_b, w_branch_c, w_branch_d, w_out, ffn_w_in, ffn_w_out, final_norm_w):
    b, t, d = x.shape
    depth = w_in.shape[0]
    assert ctx.shape[1] == TM and t % TM == 0 and t % GRID_W == 0 and b % 8 == 0 and b % NB == 0
    assert t // GRID_W >= NA_KH

    pad = (-(b + 1)) % 8
    cc = jnp.concatenate([c, c_ctx[None, :], jnp.zeros((pad, d), F32)], axis=0)
    mod_all = _adaln(cc, ada_w, ada_b)
    mod_x = mod_all[:, :b].reshape(depth, b, 1, 6, d)
    mod_c = jnp.broadcast_to(mod_all[:, b].reshape(depth, 1, 1, 6, d), (depth, b, 1, 6, d))
    mod = jnp.concatenate([mod_c, mod_x], axis=2)

    cos_t, sin_t = _rope_tables(t, MLSTM_HEADS * HEAD_DIM)
    perm = _gate_perm()
    pa, pb, ones_ab = _log_decay_perms()
    shifts = _conv_shifts()
    n_state = S5_GROUPS * S5_STATE

    seq = None
    for l in range(depth):
        with_ctx = l < depth - 1
        if l == 0:
            stream = (x, ctx) if with_ctx else (jnp.concatenate([ctx, x], axis=1), None)
        else:
            stream = (seq, None)
        wl = w_in[l].astype(MXU_DTYPE)
        w_small = jnp.concatenate([wl[:, 1024:1040], wl[:, 3600:3616], jnp.zeros((d, W_SMALL - 32), MXU_DTYPE)],
                                  axis=1)
        w_proj = jnp.concatenate([wl[:, 0:512], wl[:, 2576:3600], wl[:, 512:768], wl[:, 768:1024], w_small,
                                  wl[:, 1040:1296], wl[:, 1296:2064], wl[:, 2064:2576]], axis=1)
        w_gate = wl[:, 3616:]

        cv, v, mo, small, su, nqkv, dz = _proj(stream[0], mod[l], norm1_w[l][None, :], w_proj, ctx=stream[1])

        conv_w = jnp.concatenate([mlstm_conv_w[l], ssd_conv_w[l]], axis=1)
        conv_w = jnp.concatenate([conv_w, jnp.zeros((1, W_CV), F32)], axis=0)
        conv_b = jnp.concatenate([mlstm_conv_b[l], ssd_conv_b[l]])[None, :]
        gbias = (_lanes(mlstm_ib[l].reshape(-1), L_MI) + _lanes(mlstm_fb[l].reshape(-1), L_MF)
                 + _lanes(ssd_dt_bias[l].reshape(-1), L_DDT))
        qk, xbc, gc, gr, la, lb = _prep(cv, small, conv_w, conv_b, cos_t, sin_t, gbias,
                                        _lanes(ssd_a_log[l].reshape(-1), L_DDT), perm, pa, pb, ones_ab, shifts)

        hm = _mlstm(qk, v, gc, la, lb)

        braw_re = _block_diag(jnp.swapaxes(s5_b_re[l], 1, 2))
        braw_im = _block_diag(jnp.swapaxes(s5_b_im[l], 1, 2))
        cw = jnp.concatenate([_block_diag(jnp.swapaxes(s5_c_re[l], 1, 2)),
                              -_block_diag(jnp.swapaxes(s5_c_im[l], 1, 2))], axis=0).astype(MXU_DTYPE)
        ldt = jnp.repeat(s5_log_dt[l], S5_STATE, axis=1)
        ys = None
        for dd in range(2):
            ys = _s5(su.reshape(-1, W_SU), s5_lam_re[l, dd].reshape(1, n_state), s5_lam_im[l, dd].reshape(1, n_state),
                     ldt[dd].reshape(1, n_state), braw_re, braw_im, cw, b, fwd_out=ys)
        ys = ys.reshape(-1, b * W_SU)

        yna = _na(nqkv, _na_bias_tables(na_rpb[l]), with_ctx)

        yd = _ssd(xbc, gc, gr, la, lb, jnp.repeat(ssd_d[l], HEAD_DIM)[None, :])

        seq = _merge(stream[0], mod[l], norm1_w[l][None, :], hm, mo, mlstm_norm_w[l][None, :], ys, su,
                     s5_d[l][None, :], s5_glu_w[l].astype(MXU_DTYPE), yna, yd, dz, ssd_norm_w[l][None, :],
                     w_gate, w_branch_a[l].astype(MXU_DTYPE), w_branch_b[l].astype(MXU_DTYPE),
                     w_branch_c[l].astype(MXU_DTYPE), w_branch_d[l].astype(MXU_DTYPE), w_out[l].astype(MXU_DTYPE),
                     with_ctx, ctx=stream[1])
        seq = _ffn(seq, mod[l], norm2_w[l][None, :], ffn_w_in[l].astype(MXU_DTYPE), ffn_w_out[l].astype(MXU_DTYPE),
                   final_norm_w[None, :], not with_ctx)
    return seq
```

```python
import functools
import math

import jax
import jax.numpy as jnp
import numpy as np
from jax import lax
from jax.experimental import pallas as pl
from jax.experimental.pallas import tpu as pltpu

F32 = jnp.float32
MXU_DTYPE = jnp.bfloat16
HIGHEST = lax.Precision.HIGHEST

GRID_W = 64
EPS = 1e-6
CONV_K = 7
ROPE_THETA = 10000.0
HEAD_DIM = 64
MLSTM_HEADS = 4
S5_GROUPS = 16
S5_GROUP = 16
S5_STATE = 64
NA_HEADS = 4
NA_KH = 8
NA_KW = 16
SSD_HEADS = 8
SSD_GROUPS = 2
SSD_STATE = 128

TM = 256
NB = 2
NBS = 4
SUB = 128
HALO = 8
S5_SUB = 32
NEG = -1e30
VMEM_LIMIT = 56 * 1024 * 1024

W_QK, W_XBC, W_V, W_O, W_SMALL, W_SU, W_NQKV, W_DZ = 512, 1024, 256, 256, 128, 256, 768, 512
W_CV = W_QK + W_XBC
PROJ_WIDTHS = (W_CV, W_V, W_O, W_SMALL, W_SU, W_NQKV, W_DZ)
SU_POS = 4
L_MI, L_MF, L_DDT = 0, 8, 16
G_LI, G_BCUM, G_ACS, G_DT = 0, 4, 8, 16


def _mm(a, b):
    return jnp.dot(a.astype(MXU_DTYPE), b.astype(MXU_DTYPE), preferred_element_type=F32)


def _mm_nt(a, b):
    return lax.dot_general(a.astype(MXU_DTYPE), b.astype(MXU_DTYPE), (((1,), (1,)), ((), ())),
                           preferred_element_type=F32)


def _mm_tn(a, b):
    return lax.dot_general(a.astype(MXU_DTYPE), b.astype(MXU_DTYPE), (((0,), (0,)), ((), ())),
                           preferred_element_type=F32)


def _mm_f32(a, b):
    return jnp.dot(a, b, preferred_element_type=F32, precision=HIGHEST)


def _split3(x):
    hi = x.astype(MXU_DTYPE)
    r1 = x - hi.astype(F32)
    mid = r1.astype(MXU_DTYPE)
    lo = (r1 - mid.astype(F32)).astype(MXU_DTYPE)
    return hi, mid, lo


def _as_01(sel):
    if sel.dtype == jnp.bool_:
        sel = jnp.where(sel, 1.0, 0.0)
    return sel.astype(MXU_DTYPE)


def _mm_exact_rhs(x, sel, terms=3):
    sel = _as_01(sel)
    return sum(jnp.dot(p, sel, preferred_element_type=F32) for p in _split3(x)[:terms])


def _mm_exact_lhs(sel, x):
    sel = _as_01(sel)
    return sum(jnp.dot(sel, p, preferred_element_type=F32) for p in _split3(x))


def _sigmoid(x):
    return 1.0 / (1.0 + jnp.exp(-x))


def _silu(x):
    return x * _sigmoid(x)


def _softplus(x):
    return jnp.maximum(x, 0.0) + jnp.log(1.0 + jnp.exp(-jnp.abs(x)))


def _iota(shape, dim):
    return lax.broadcasted_iota(jnp.int32, shape, dim)


def _head_of_lane(width):
    return jnp.right_shift(_iota((1, width), 1), 6)


def _same_head(rows, cols):
    return jnp.right_shift(_iota((rows, cols), 0), 6) == jnp.right_shift(_iota((rows, cols), 1), 6)


def _stack(fn):
    return jnp.concatenate([fn(i) for i in range(NB)], axis=0)


def _round_robin(chains):
    live = list(chains)
    while live:
        for chain in list(live):
            try:
                next(chain)
            except StopIteration:
                live.remove(chain)


def _tile_of(d, j, nt):
    return jnp.where(d == 0, j, jnp.where(j == 0, 0, nt - j))


def _norm_mod(x, w, scale, shift):
    y = x * lax.rsqrt(jnp.mean(x * x, axis=-1, keepdims=True) + EPS) * w
    return y * (1.0 + scale) + shift


def _order_mask(rev, blocks, n):
    shape = (blocks * n, n)
    t_minus_s = (jnp.bitwise_and(_iota(shape, 0), n - 1) - _iota(shape, 1)) * jnp.where(rev, -1, 1)
    return t_minus_s >= 0


def _const_spec(arr, n_grid, single=False):
    kwargs = {"pipeline_mode": pl.Buffered(1)} if single else {}
    return pl.BlockSpec(arr.shape, lambda *_: (0,) * arr.ndim, **kwargs)


def _mod_spec(d, first):
    return pl.BlockSpec((NB, None, 6, d), lambda bp, j: (bp, jnp.minimum(first + j, 1), 0, 0))


def _adaln_kernel(c_ref, w_ref, b_ref, o_ref):
    o_ref[...] = _mm_f32(_silu(c_ref[...]), w_ref[...]) + b_ref[...]


def _adaln(cc, ada_w, ada_b):
    depth, d, n = ada_w.shape
    tn = 768
    return pl.pallas_call(
        _adaln_kernel,
        grid=(depth, n // tn),
        in_specs=[pl.BlockSpec(cc.shape, lambda l, i: (0, 0)),
                  pl.BlockSpec((None, d, tn), lambda l, i: (l, 0, i)),
                  pl.BlockSpec((None, 1, tn), lambda l, i: (l, 0, i))],
        out_specs=pl.BlockSpec((None, cc.shape[0], tn), lambda l, i: (l, 0, i)),
        out_shape=jax.ShapeDtypeStruct((depth, cc.shape[0], n), F32),
        name="adaln",
    )(cc, ada_w, ada_b.reshape(depth, 1, n))


def _stream_rows(x_ref, c_ref, i):
    if c_ref is None:
        return x_ref[i]
    return jnp.where(pl.program_id(1) == 0, c_ref[i], x_ref[i])


def _stream_specs(d, ctx, first):
    if ctx is None:
        return [pl.BlockSpec((NB, TM, d), lambda bp, j: (bp, first + j, 0))]
    return [pl.BlockSpec((NB, TM, d), lambda bp, j: (bp, jnp.maximum(j - 1, 0), 0)),
            pl.BlockSpec((NB, TM, d), lambda bp, j: (bp, 0, 0))]


def _proj_kernel(*refs, split):
    x_ref, c_ref = (refs[0], refs[1]) if split else (refs[0], None)
    mod_ref, nw_ref, w_ref, *out_refs = refs[2:] if split else refs[1:]
    h = _stack(lambda i: _norm_mod(_stream_rows(x_ref, c_ref, i), nw_ref[...], mod_ref[i, 1:2, :],
                                   mod_ref[i, 0:1, :]))
    h = h.astype(MXU_DTYPE)
    off = 0
    for pos, (ref, n) in enumerate(zip(out_refs, PROJ_WIDTHS)):
        res = jnp.dot(h, w_ref[:, off:off + n], preferred_element_type=F32)
        if pos == SU_POS:
            ref[...] = jnp.concatenate([res[i * TM:(i + 1) * TM, :] for i in range(NB)], axis=1)
        else:
            for i in range(NB):
                ref[i] = res[i * TM:(i + 1) * TM, :]
        off += n


def _proj(seq, mod, norm_w, w_proj, ctx=None):
    b, s, d = seq.shape
    s = s if ctx is None else s + TM
    nt = s // TM

    def tok(width):
        return pl.BlockSpec((NB, TM, width), lambda bp, j: (bp, j, 0))

    out_specs = [tok(w) for w in PROJ_WIDTHS]
    out_shape = [jax.ShapeDtypeStruct((b, s, w), F32) for w in PROJ_WIDTHS]
    out_specs[SU_POS] = pl.BlockSpec((TM, NB * W_SU), lambda bp, j: (j, bp))
    out_shape[SU_POS] = jax.ShapeDtypeStruct((s, b * W_SU), F32)
    streams = [seq] if ctx is None else [seq, ctx]
    return pl.pallas_call(
        functools.partial(_proj_kernel, split=ctx is not None),
        grid=(b // NB, nt),
        in_specs=_stream_specs(d, ctx, 0) + [_mod_spec(d, 0), _const_spec(norm_w, 2),
                                             _const_spec(w_proj, 2, single=True)],
        out_specs=out_specs,
        out_shape=out_shape,
        compiler_params=pltpu.CompilerParams(vmem_limit_bytes=VMEM_LIMIT),
        name="proj",
    )(*streams, mod, norm_w, w_proj)


def _prep_kernel(cv_ref, cvp_ref, cvn_ref, cw_ref, cb_ref, cos_ref, sin_ref, sm_ref, gb_ref, al_ref, perm_ref,
                 pa_ref, pb_ref, one_ref, shift_ref, qk_ref, xbc_ref, gc_ref, gr_ref, la_ref, lb_ref, *, nt):
    t = pl.program_id(1)
    prev_ok = t > 1
    next_ok = (t >= 1) & (t < nt - 1)
    lane = _iota((1, W_SMALL), 1)
    first = jnp.bitwise_and(_iota((1, W_QK), 1), 31) < 16
    r = _iota((TM, TM), 0)
    c = _iota((TM, TM), 1)
    same_chunk = jnp.right_shift(r, 7) == jnp.right_shift(c, 7)
    cum = (lane >= G_BCUM) & (lane < G_DT)

    mid = CONV_K // 2
    side_taps = [k for k in range(CONV_K) if k != mid]

    def edge_rows(ext24):
        out = cb_ref[...] + cw_ref[mid:mid + 1, :] * ext24[HALO:2 * HALO, :]
        for k in side_taps:
            out = out + cw_ref[k:k + 1, :] * pltpu.roll(ext24, (mid - k) % (3 * HALO), 0)[HALO:2 * HALO, :]
        return out

    for i in range(NB):
        cur = cv_ref[i]
        weighted = jnp.concatenate([(cw_ref[k:k + 1, :] * cur).astype(MXU_DTYPE) for k in side_taps], axis=0)
        acc = (cb_ref[...] + cw_ref[mid:mid + 1, :] * cur
               + jnp.dot(shift_ref[...], weighted, preferred_element_type=F32))
        top = edge_rows(jnp.concatenate([jnp.where(prev_ok, cvp_ref[i], 0.0), cur[0:2 * HALO, :]], axis=0))
        bot = edge_rows(jnp.concatenate([cur[TM - 2 * HALO:, :], jnp.where(next_ok, cvn_ref[i], 0.0)], axis=0))
        acc = _silu(jnp.concatenate([top, acc[HALO:TM - HALO, :], bot], axis=0))
        xbc_ref[i] = acc[:, W_QK:]
        qk = acc[:, :W_QK]
        partner = jnp.where(first, pltpu.roll(qk, W_QK - 16, 1), pltpu.roll(qk, 16, 1))
        qk_ref[i] = qk * cos_ref[...] + partner * sin_ref[...]

        g_all = sm_ref[i] + gb_ref[...]
        dt = _softplus(g_all)
        src = jnp.where(lane < L_MF, g_all,
                        jnp.where(lane < L_DDT, -_softplus(-g_all), dt * (-jnp.exp(al_ref[...]))))
        src = jnp.concatenate([src, dt], axis=1)
        for d in range(2):
            tri = same_chunk & ((c <= r) if d == 0 else (c >= r))
            g = _mm_exact_rhs(src, perm_ref[d])
            g = jnp.where(cum, _mm_exact_lhs(tri, jnp.where(cum, g, 0.0)), g)
            gc_ref[d, i] = g
            a_mat = one_ref[0:1, :] + sum(jnp.dot(p, pa_ref[k], preferred_element_type=F32)
                                           for k, p in enumerate(_split3(g)))
            cv = jnp.where(lane < MLSTM_HEADS, g - pltpu.roll(g, W_SMALL - G_BCUM, 1), -g)
            b_mat = one_ref[1:2, :] + sum(jnp.dot(p, pb_ref[k], preferred_element_type=F32)
                                           for k, p in enumerate(_split3(cv)))
            la_ref[d, i] = a_mat.astype(MXU_DTYPE)
            for ci in range(TM // SUB):
                gr_ref[d, i, ci] = g[ci * SUB:(ci + 1) * SUB, :].T
                lb_ref[d, i, ci] = b_mat[ci * SUB:(ci + 1) * SUB, :].T.astype(MXU_DTYPE)


def _prep(cv, small, conv_w, conv_b, cos_t, sin_t, gbias, alog, perm, pa, pb, ones_ab, shifts):
    b, s, _ = cv.shape
    nt = s // TM
    per = TM // HALO
    last = s // HALO - 1
    nsub = TM // SUB
    return pl.pallas_call(
        functools.partial(_prep_kernel, nt=nt),
        grid=(b // NB, nt),
        in_specs=[pl.BlockSpec((NB, TM, W_CV), lambda bp, j: (bp, j, 0)),
                  pl.BlockSpec((NB, HALO, W_CV), lambda bp, j: (bp, jnp.maximum(j * per - 1, 0), 0)),
                  pl.BlockSpec((NB, HALO, W_CV), lambda bp, j: (bp, jnp.minimum((j + 1) * per, last), 0)),
                  _const_spec(conv_w, 2), _const_spec(conv_b, 2),
                  pl.BlockSpec((TM, W_QK), lambda bp, j: (j, 0)),
                  pl.BlockSpec((TM, W_QK), lambda bp, j: (j, 0)),
                  pl.BlockSpec((NB, TM, W_SMALL), lambda bp, j: (bp, j, 0)),
                  _const_spec(gbias, 2), _const_spec(alog, 2), _const_spec(perm, 2),
                  _const_spec(pa, 2), _const_spec(pb, 2), _const_spec(ones_ab, 2), _const_spec(shifts, 2)],
        out_specs=[pl.BlockSpec((NB, TM, W_QK), lambda bp, j: (bp, j, 0)),
                   pl.BlockSpec((NB, TM, W_XBC), lambda bp, j: (bp, j, 0)),
                   pl.BlockSpec((2, NB, TM, W_SMALL), lambda bp, j: (0, bp, j, 0)),
                   pl.BlockSpec((2, NB, nsub, W_SMALL, SUB), lambda bp, j: (0, bp, j, 0, 0)),
                   pl.BlockSpec((2, NB, TM, W_SMALL), lambda bp, j: (0, bp, j, 0)),
                   pl.BlockSpec((2, NB, nsub, W_SMALL, SUB), lambda bp, j: (0, bp, j, 0, 0))],
        out_shape=[jax.ShapeDtypeStruct((b, s, W_QK), F32),
                   jax.ShapeDtypeStruct((b, s, W_XBC), F32),
                   jax.ShapeDtypeStruct((2, b, s, W_SMALL), F32),
                   jax.ShapeDtypeStruct((2, b, s // SUB, W_SMALL, SUB), F32),
                   jax.ShapeDtypeStruct((2, b, s, W_SMALL), MXU_DTYPE),
                   jax.ShapeDtypeStruct((2, b, s // SUB, W_SMALL, SUB), MXU_DTYPE)],
        compiler_params=pltpu.CompilerParams(vmem_limit_bytes=VMEM_LIMIT),
        name="prep",
    )(cv, cv, cv, conv_w, conv_b, cos_t, sin_t, small, gbias, alog, perm, pa, pb, ones_ab, shifts)


def _scan_specs(nt, widths, kinds):
    nsub = TM // SUB
    specs = [pl.BlockSpec((NBS, TM, w), lambda d, j, bp: (bp, _tile_of(d, j, nt), 0)) for w in widths]
    for kind in kinds:
        if kind == "col":
            specs.append(pl.BlockSpec((None, NBS, TM, W_SMALL), lambda d, j, bp: (d, bp, _tile_of(d, j, nt), 0)))
        else:
            specs.append(pl.BlockSpec((None, NBS, nsub, W_SMALL, SUB),
                                      lambda d, j, bp: (d, bp, _tile_of(d, j, nt), 0, 0)))
    return specs


def _mlstm_kernel(qk_ref, v_ref, gc_ref, la_ref, lb_ref, h_ref, c_scr, n_scr, m_scr):
    d = pl.program_id(0)
    j = pl.program_id(1)
    bp = pl.program_id(2)
    rev = d == 1
    width = MLSTM_HEADS * HEAD_DIM
    nsub = TM // SUB
    heads = list(range(MLSTM_HEADS))
    mask_s = _order_mask(rev, MLSTM_HEADS, SUB)
    head = _head_of_lane(width)
    same_head = _same_head(width, width)
    lane = _iota((1, W_SMALL), 1)
    valid = lane < MLSTM_HEADS
    group = jnp.right_shift(lane, 3)
    to_heads = _as_01(jnp.right_shift(_iota((W_SMALL, width), 1), 6) == _iota((W_SMALL, width), 0))
    from_heads = _as_01(jnp.right_shift(_iota((width, W_SMALL), 0), 6) == _iota((width, W_SMALL), 1))
    ones_rows = jnp.ones((8, SUB), MXU_DTYPE)

    @pl.when(j == 0)
    def _():
        for i in range(NBS):
            c_scr[bp * NBS + i] = jnp.zeros(c_scr.shape[1:], F32)
            n_scr[bp * NBS + i] = jnp.zeros(n_scr.shape[1:], F32)
            m_scr[bp * NBS + i] = jnp.zeros(m_scr.shape[1:], F32)

    def chain(i):
        bi = bp * NBS + i
        c_st = c_scr[bi]
        n_st = n_scr[bi, 0:1, :]
        m_st = m_scr[bi, 0:1, :]
        chunks = []
        for step in range(nsub):
            ci = jnp.where(rev, nsub - 1 - step, step)
            rows = pl.ds(pl.multiple_of(ci * SUB, SUB), SUB)
            q = qk_ref[i, rows, 0:width]
            k = qk_ref[i, rows, width:2 * width]
            qb = q.astype(MXU_DTYPE)
            kb = k.astype(MXU_DTYPE)
            vb = v_ref[i, rows, :].astype(MXU_DTYPE)
            li = gc_ref[i, rows, :]
            bc = pltpu.roll(li, W_SMALL - G_BCUM, 1)
            la = la_ref[i, rows, :]
            lb = lb_ref[i, ci]

            la_s = jnp.concatenate([jnp.where(group == h, la, 0) for h in heads], axis=0)
            tiles = [slice((h // 2) * 128, (h // 2 + 1) * 128) for h in heads]
            q_s = [jnp.where(head[:, tiles[h]] == h, qb[:, tiles[h]], 0) for h in heads]
            log_w = jnp.where(mask_s, jnp.dot(la_s, lb, preferred_element_type=F32), NEG)
            yield
            m_in_s = jnp.max(log_w, axis=1, keepdims=True)
            qk = jnp.concatenate([_mm_nt(jnp.concatenate(q_s[2 * p:2 * p + 2], axis=0), kb[:, tiles[2 * p]])
                                  for p in range(MLSTM_HEADS // 2)], axis=0)
            sm = qk * jnp.exp(log_w - m_in_s)
            rs = jnp.sum(sm, axis=1, keepdims=True)
            sm = sm.astype(MXU_DTYPE)
            yield
            pv = jnp.dot(sm, vb, preferred_element_type=F32)
            yield
            num0 = pv[0:SUB, :]
            den0 = jnp.broadcast_to(rs[0:SUB, :], (SUB, W_SMALL))
            m_in = jnp.broadcast_to(m_in_s[0:SUB, :], (SUB, W_SMALL))
            for h in heads[1:]:
                blk = slice(h * SUB, (h + 1) * SUB)
                num0 = jnp.where(head == h, pv[blk, :], num0)
                den0 = jnp.where(lane == h, rs[blk, :], den0)
                m_in = jnp.where(lane == h, m_in_s[blk, :], m_in)
            b_last = jnp.where(rev, bc[0:1, :], bc[SUB - 1:SUB, :])
            log_k = b_last - bc + li
            mk = jnp.max(log_k, axis=0, keepdims=True)
            wk_full = _mm_exact_rhs(jnp.where(valid, jnp.exp(log_k - mk), 0.0), to_heads, terms=2)
            kw = (k * wk_full).astype(MXU_DTYPE)
            yield
            kv0 = jnp.where(same_head, _mm_tn(kw, vb), 0.0)
            ks0 = jnp.dot(ones_rows, kw, preferred_element_type=F32)[0:1, :]
            chunks.append((rows, q, qb, bc, num0, den0, m_in, b_last, mk, kv0, ks0))
            yield

        for rows, q, qb, bc, num0, den0, m_in, b_last, mk, kv0, ks0 in chunks:
            inter = bc + m_st
            m_t = jnp.maximum(inter, m_in)
            r = jnp.exp(m_in - m_t)
            g = jnp.exp(inter - m_t)
            den = r * den0 + g * jnp.dot((q * n_st).astype(MXU_DTYPE), from_heads, preferred_element_type=F32)
            inv = 1.0 / jnp.maximum(jnp.abs(den), jnp.exp(-m_t))
            ir_full = _mm_exact_rhs(jnp.where(valid, inv * r, 0.0), to_heads, terms=2)
            ig_full = _mm_exact_rhs(jnp.where(valid, inv * g, 0.0), to_heads, terms=2)
            h_ref[i, rows, :] = num0 * ir_full + _mm(qb, c_st) * ig_full
            yield
            m_new = jnp.maximum(b_last + m_st, mk)
            scales = jnp.concatenate([jnp.exp(b_last + m_st - m_new), jnp.exp(mk - m_new),
                                      jnp.zeros((6, W_SMALL), F32)], axis=0)
            scales_full = _mm_exact_rhs(jnp.where(valid, scales, 0.0), to_heads)
            c_st = c_st * scales_full[0:1, :] + kv0 * scales_full[1:2, :]
            n_st = n_st * scales_full[0:1, :] + ks0 * scales_full[1:2, :]
            m_st = jnp.where(valid, m_new, 0.0)
            yield

        c_scr[bi] = c_st
        n_scr[bi, 0:1, :] = n_st
        m_scr[bi, 0:1, :] = m_st

    _round_robin([chain(i) for i in range(NBS)])


def _mlstm(qk, v, gc, la, lb):
    b, s, _ = qk.shape
    nt = s // TM
    width = MLSTM_HEADS * HEAD_DIM
    return pl.pallas_call(
        _mlstm_kernel,
        grid=(2, nt, b // NBS),
        in_specs=_scan_specs(nt, (2 * width, width), ("col", "col", "row")),
        out_specs=pl.BlockSpec((None, NBS, TM, width), lambda d, j, bp: (d, bp, _tile_of(d, j, nt), 0)),
        out_shape=jax.ShapeDtypeStruct((2, b, s, width), F32),
        scratch_shapes=[pltpu.VMEM((b, width, width), F32), pltpu.VMEM((b, 8, width), F32),
                        pltpu.VMEM((b, 8, W_SMALL), F32)],
        compiler_params=pltpu.CompilerParams(vmem_limit_bytes=VMEM_LIMIT),
        name="mlstm",
    )(qk, v, gc, la, lb)


def _s5_kernel(u_ref, lre_ref, lim_ref, ldt_ref, bre_ref, bim_ref, cw_ref, *rest, nb, rev):
    acc_ref = rest[0] if rev else None
    y_ref, wbu_scr, are_scr, aim_scr, hre_scr, him_scr, hb0, hb1, hb2 = rest[1:] if rev else rest
    j = pl.program_id(0)
    n_state = S5_GROUPS * S5_STATE

    @pl.when(j == 0)
    def _():
        lre = lre_ref[...]
        lim = lim_ref[...]
        dt = jnp.exp(ldt_ref[...])
        mag = jnp.exp(lre * dt)
        a_re = mag * jnp.cos(lim * dt)
        a_im = mag * jnp.sin(lim * dt)
        den = lre * lre + lim * lim
        nr = a_re - 1.0
        coef_re = (nr * lre + a_im * lim) / den
        coef_im = (a_im * lre - nr * lim) / den
        wbu_scr[:, :n_state] = (coef_re * bre_ref[...] - coef_im * bim_ref[...]).astype(MXU_DTYPE)
        wbu_scr[:, n_state:] = (coef_re * bim_ref[...] + coef_im * bre_ref[...]).astype(MXU_DTYPE)
        are_scr[...] = jnp.broadcast_to(a_re, are_scr.shape)
        aim_scr[...] = jnp.broadcast_to(a_im, aim_scr.shape)
        hre_scr[...] = jnp.zeros(hre_scr.shape, F32)
        him_scr[...] = jnp.zeros(him_scr.shape, F32)

    a_re = are_scr[...]
    a_im = aim_scr[...]
    rows_sub = S5_SUB * nb
    n_sub = TM // S5_SUB
    n_part = 1
    part = rows_sub // n_part
    bufs = (hb0, hb1, hb2)
    state = [hre_scr[...], him_scr[...]]

    def first_row(sb):
        return (n_sub - 1 - sb if rev else sb) * rows_sub

    def project(sb):
        for p in range(n_part):
            lhs = u_ref[first_row(sb) + p * part:first_row(sb) + (p + 1) * part, :].astype(MXU_DTYPE)
            bufs[sb % 3][p * part:(p + 1) * part, :] = jnp.dot(lhs, wbu_scr[...], preferred_element_type=F32)
            yield

    def recur(sb):
        buf = bufs[sb % 3]
        h_re, h_im = state
        for t in range(S5_SUB):
            ti = S5_SUB - 1 - t if rev else t
            r = slice(ti * nb, (ti + 1) * nb)
            n_re = a_re * h_re - a_im * h_im + buf[r, :n_state]
            n_im = a_re * h_im + a_im * h_re + buf[r, n_state:]
            buf[r, :n_state] = n_re
            buf[r, n_state:] = n_im
            h_re, h_im = n_re, n_im
            if t % 4 == 3:
                yield
        state[0], state[1] = h_re, h_im

    def readout(sb):
        for p in range(n_part):
            lhs = bufs[sb % 3][p * part:(p + 1) * part, :].astype(MXU_DTYPE)
            rows = slice(first_row(sb) + p * part, first_row(sb) + (p + 1) * part)
            y = jnp.dot(lhs, cw_ref[...], preferred_element_type=F32)
            y_ref[rows, :] = y + acc_ref[rows, :] if rev else y
            yield

    for stage in range(n_sub + 2):
        live = []
        if stage < n_sub:
            live.append(project(stage))
        if 1 <= stage <= n_sub:
            live.append(recur(stage - 1))
        if stage >= 2:
            live.append(readout(stage - 2))
        _round_robin(live)
    hre_scr[...] = state[0]
    him_scr[...] = state[1]


def _s5(u_tm, lam_re, lam_im, log_dt, braw_re, braw_im, cw, nb, fwd_out=None):
    rev = fwd_out is not None
    rows, width = u_tm.shape
    s = rows // nb
    nt = s // TM
    n_state = S5_GROUPS * S5_STATE
    d = 1 if rev else 0
    hbuf = pltpu.VMEM((S5_SUB * nb, 2 * n_state), F32)
    return pl.pallas_call(
        functools.partial(_s5_kernel, nb=nb, rev=rev),
        grid=(nt,),
        in_specs=[pl.BlockSpec((TM * nb, width), lambda j: (_tile_of(d, j, nt), 0)),
                  _const_spec(lam_re, 1), _const_spec(lam_im, 1), _const_spec(log_dt, 1),
                  _const_spec(braw_re, 1), _const_spec(braw_im, 1), _const_spec(cw, 1)]
        + ([pl.BlockSpec((TM * nb, width), lambda j: (_tile_of(d, j, nt), 0))] if rev else []),
        out_specs=pl.BlockSpec((TM * nb, width), lambda j: (_tile_of(d, j, nt), 0)),
        out_shape=jax.ShapeDtypeStruct((rows, width), F32),
        scratch_shapes=[pltpu.VMEM((width, 2 * n_state), MXU_DTYPE),
                        pltpu.VMEM((nb, n_state), F32), pltpu.VMEM((nb, n_state), F32),
                        pltpu.VMEM((nb, n_state), F32), pltpu.VMEM((nb, n_state), F32),
                        hbuf, hbuf, hbuf],
        compiler_params=pltpu.CompilerParams(vmem_limit_bytes=VMEM_LIMIT),
        name="s5_bwd" if rev else "s5_fwd",
    )(u_tm, lam_re, lam_im, log_dt, braw_re, braw_im, cw, *([fwd_out] if rev else []))


def _na_kernel(q_ref, k_ref, v_ref, tbl_ref, o_ref, *, first_tile, n_rows):
    t = first_tile + pl.program_id(1)
    width = NA_HEADS * HEAD_DIM
    head = _head_of_lane(width)
    scale = HEAD_DIM ** -0.5
    k_ctx = k_ref[0:TM, :].astype(MXU_DTYPE)
    v_ctx = v_ref[0:TM, :].astype(MXU_DTYPE)

    def stack_heads(q):
        return jnp.concatenate([jnp.where(head == h, q, 0.0) for h in range(NA_HEADS)], axis=0).astype(MXU_DTYPE)

    def unstack_heads(o, n):
        acc = o[0:n, :]
        for h in range(1, NA_HEADS):
            acc = jnp.where(head == h, o[h * n:(h + 1) * n, :], acc)
        return acc

    @pl.when(t == 0)
    def _():
        s = _mm_nt(stack_heads(q_ref[...] * scale), k_ctx)
        p = jnp.exp(s - jnp.max(s, axis=1, keepdims=True))
        o = _mm(p, v_ctx) / jnp.sum(p, axis=1, keepdims=True)
        o_ref[...] = unstack_heads(o, TM)

    @pl.when(t > 0)
    def _():
        rows_per_tile = TM // GRID_W
        n_lat = NA_KH * GRID_W
        def grid_row(rr):
            r = (t - 1) * rows_per_tile + rr
            row_start = jnp.clip(r - NA_KH // 2, 0, n_rows - NA_KH)
            off = (NA_KH - 1) - (r - row_start)
            win = pl.ds(pl.multiple_of(TM + row_start * GRID_W, GRID_W), n_lat)
            qs = stack_heads(q_ref[rr * GRID_W:(rr + 1) * GRID_W, :] * scale)
            s_lat = _mm_nt(qs, k_ref[win, :]) + tbl_ref[off]
            s_ctx = _mm_nt(qs, k_ctx)
            yield
            m = jnp.maximum(jnp.max(s_lat, axis=1, keepdims=True), jnp.max(s_ctx, axis=1, keepdims=True))
            p_lat = jnp.exp(s_lat - m)
            p_ctx = jnp.exp(s_ctx - m)
            den = jnp.sum(p_lat, axis=1, keepdims=True) + jnp.sum(p_ctx, axis=1, keepdims=True)
            yield
            o = (_mm(p_lat, v_ref[win, :]) + _mm(p_ctx, v_ctx)) / den
            o_ref[rr * GRID_W:(rr + 1) * GRID_W, :] = unstack_heads(o, GRID_W)
            yield

        _round_robin([grid_row(rr) for rr in range(rows_per_tile)])


def _na(nqkv, tbl, with_ctx):
    b, s, _ = nqkv.shape
    nt = s // TM
    width = NA_HEADS * HEAD_DIM
    first_tile = 0 if with_ctx else 1
    n_rows = (s - TM) // GRID_W
    return pl.pallas_call(
        functools.partial(_na_kernel, first_tile=first_tile, n_rows=n_rows),
        grid=(b, nt - first_tile),
        in_specs=[pl.BlockSpec((None, TM, width), lambda bi, j: (bi, first_tile + j, 0)),
                  pl.BlockSpec((None, s, width), lambda bi, j: (bi, 0, 1)),
                  pl.BlockSpec((None, s, width), lambda bi, j: (bi, 0, 2)),
                  _const_spec(tbl, 2)],
        out_specs=pl.BlockSpec((None, TM, width), lambda bi, j: (bi, first_tile + j, 0)),
        out_shape=jax.ShapeDtypeStruct((b, s, width), F32),
        compiler_params=pltpu.CompilerParams(vmem_limit_bytes=VMEM_LIMIT),
        name="na",
    )(nqkv, nqkv, nqkv, tbl)


def _ssd_kernel(x_ref, gc_ref, gr_ref, la_ref, lb_ref, dsk_ref, y_ref, st_scr):
    d = pl.program_id(0)
    j = pl.program_id(1)
    bp = pl.program_id(2)
    rev = d == 1
    width = SSD_HEADS * HEAD_DIM
    gn = SSD_GROUPS * SSD_STATE
    per_group = SSD_HEADS // SSD_GROUPS
    gw = per_group * HEAD_DIM
    nsub = TM // SUB
    mask_s = _order_mask(rev, per_group, SUB)
    head = _head_of_lane(width)
    skip = jnp.where(rev, 0.0, 1.0) * dsk_ref[...]
    lane = _iota((1, W_SMALL), 1)
    valid = (lane >= G_ACS) & (lane < G_ACS + SSD_HEADS)
    group = jnp.right_shift(lane, 3)
    to_heads = _as_01(jnp.right_shift(_iota((W_SMALL, width), 1), 6) == _iota((W_SMALL, width), 0) - G_ACS)

    @pl.when(j == 0)
    def _():
        for i in range(NBS):
            st_scr[bp * NBS + i] = jnp.zeros(st_scr.shape[1:], F32)

    def chain(i):
        bi = bp * NBS + i
        st = st_scr[bi]
        chunks = []
        for step in range(nsub):
            ci = jnp.where(rev, nsub - 1 - step, step)
            rows = pl.ds(pl.multiple_of(ci * SUB, SUB), SUB)
            xs = x_ref[i, rows, 0:width]
            bm = x_ref[i, rows, width:width + gn].astype(MXU_DTYPE)
            cm = x_ref[i, rows, width + gn:width + 2 * gn].astype(MXU_DTYPE)
            xsb = xs.astype(MXU_DTYPE)
            acs = gc_ref[i, rows, :]
            dt = pltpu.roll(acs, W_SMALL - (G_DT - G_ACS), 1)
            grow = gr_ref[i, ci]
            la = la_ref[i, rows, :]
            lb = lb_ref[i, ci]

            a_last = jnp.where(rev, acs[0:1, :], acs[SUB - 1:SUB, :])
            e_full = _mm_exact_rhs(jnp.where(valid, jnp.exp(acs), 0.0), to_heads, terms=2)
            xw = xs * _mm_exact_rhs(jnp.where(valid, jnp.exp(a_last - acs) * dt, 0.0), to_heads, terms=2)
            last = jnp.broadcast_to(jnp.where(valid, jnp.exp(a_last), 0.0), (8, W_SMALL))
            last_full = _mm_exact_rhs(last, to_heads)[0:1, :]
            yield

            y_parts, st_parts, c_parts = [], [], []
            for g in range(SSD_GROUPS):
                b_g = bm[:, g * SSD_STATE:(g + 1) * SSD_STATE]
                c_g = cm[:, g * SSD_STATE:(g + 1) * SSD_STATE]
                lanes = slice(g * gw, (g + 1) * gw)
                hs = range(g * per_group, (g + 1) * per_group)
                la_s = jnp.concatenate([jnp.where(group == MLSTM_HEADS + h, la, 0) for h in hs], axis=0)
                dt_s = jnp.concatenate([jnp.broadcast_to(grow[G_DT + h:G_DT + h + 1, :], (SUB, SUB)) for h in hs],
                                       axis=0)
                decay = jnp.exp(jnp.where(mask_s, jnp.dot(la_s, lb, preferred_element_type=F32), NEG)) * dt_s
                cb = _mm_nt(c_g, b_g)
                m_s = (jnp.concatenate([cb] * per_group, axis=0) * decay).astype(MXU_DTYPE)
                yield
                yd = jnp.dot(m_s, xsb[:, lanes], preferred_element_type=F32)
                y_g = yd[0:SUB, :]
                for hh in range(1, per_group):
                    y_g = jnp.where(head[:, lanes] == g * per_group + hh, yd[hh * SUB:(hh + 1) * SUB, :], y_g)
                y_parts.append(y_g)
                st_parts.append(_mm_tn(b_g, xw[:, lanes]))
                c_parts.append(c_g)
                yield
            chunks.append((rows, jnp.concatenate(y_parts, axis=1) + skip * xs, jnp.concatenate(st_parts, axis=1),
                           c_parts, e_full, last_full))

        for rows, y_in, st_in, c_parts, e_full, last_full in chunks:
            y_off = jnp.concatenate([_mm(c_parts[g], st[:, g * gw:(g + 1) * gw]) for g in range(SSD_GROUPS)], axis=1)
            y_ref[i, rows, :] = y_in + y_off * e_full
            st = st * last_full + st_in
            yield
        st_scr[bi] = st

    _round_robin([chain(i) for i in range(NBS)])


def _ssd(xbc, gc, gr, la, lb, dskip):
    b, s, cw = xbc.shape
    nt = s // TM
    width = SSD_HEADS * HEAD_DIM
    return pl.pallas_call(
        _ssd_kernel,
        grid=(2, nt, b // NBS),
        in_specs=_scan_specs(nt, (cw,), ("col", "row", "col", "row")) + [_const_spec(dskip, 3)],
        out_specs=pl.BlockSpec((None, NBS, TM, width), lambda d, j, bp: (d, bp, _tile_of(d, j, nt), 0)),
        out_shape=jax.ShapeDtypeStruct((2, b, s, width), F32),
        scratch_shapes=[pltpu.VMEM((b, SSD_STATE, width), F32)],
        compiler_params=pltpu.CompilerParams(vmem_limit_bytes=VMEM_LIMIT),
        name="ssd",
    )(xbc, gc, gr, la, lb, dskip)


def _merge_kernel(*refs, split):
    x_ref, c_ref = (refs[0], refs[1]) if split else (refs[0], None)
    (mod_ref, nw_ref, hm_ref, mo_ref, mnw_ref, ys_ref, su_ref, s5d_ref, glu_ref, na_ref, yd_ref, dz_ref, dnw_ref,
     wg_ref, wa_ref, wb_ref, wc_ref, wd_ref, wo_ref, o_ref) = refs[2:] if split else refs[1:]
    d_model = x_ref.shape[2]
    xs = [_stream_rows(x_ref, c_ref, i) for i in range(NB)]
    h = _stack(lambda i: _norm_mod(xs[i], nw_ref[...], mod_ref[i, 1:2, :], mod_ref[i, 0:1, :]))
    h = h.astype(MXU_DTYPE)

    wm = MLSTM_HEADS * HEAD_DIM
    hm = _stack(lambda i: (hm_ref[0, i] + hm_ref[1, i]) * _sigmoid(mo_ref[i]))
    ms = _mm_exact_rhs(hm * hm, _same_head(wm, wm)) * (1.0 / HEAD_DIM)
    ya = hm * lax.rsqrt(ms + EPS) * mnw_ref[...]

    ws = S5_GROUPS * S5_GROUP
    ys = _stack(lambda i: (ys_ref[:, i * ws:(i + 1) * ws]
                           + s5d_ref[...] * su_ref[:, i * ws:(i + 1) * ws]))
    ys = 0.5 * ys * (1.0 + jnp.tanh(math.sqrt(2.0 / math.pi) * (ys + 0.044715 * (ys * ys * ys))))
    ab = _mm(ys, glu_ref[...])
    yb = ab[:, :ws] * _sigmoid(ab[:, ws:])

    yc = _stack(lambda i: na_ref[i])

    yd = _stack(lambda i: (yd_ref[0, i] + yd_ref[1, i]) * _silu(dz_ref[i]))
    yd = yd * lax.rsqrt(jnp.mean(yd * yd, axis=-1, keepdims=True) + EPS) * dnw_ref[...]

    m = None
    for k, (y, w_ref) in enumerate(((ya, wa_ref), (yb, wb_ref), (yc, wc_ref), (yd, wd_ref))):
        gate = _sigmoid(jnp.dot(h, wg_ref[:, k * d_model:(k + 1) * d_model], preferred_element_type=F32))
        term = gate * _mm(y, w_ref[...])
        m = term if m is None else m + term
    out = _mm(m, wo_ref[...])
    for i in range(NB):
        o_ref[i] = xs[i] + mod_ref[i, 2:3, :] * out[i * TM:(i + 1) * TM, :]


def _merge(seq, mod, norm_w, hm, mo, mnw, ys, su, s5d, glu_w, yna, yd, dz, dnw, wg, wa, wb, wc, wd, wo,
           with_ctx, ctx=None):
    b, s, d = seq.shape
    s = s if ctx is None else s + TM
    nt = s // TM
    first = 0 if with_ctx else 1
    assert ctx is None or with_ctx

    def tok(width):
        return pl.BlockSpec((NB, TM, width), lambda bp, j: (bp, first + j, 0))

    def tok2(width):
        return pl.BlockSpec((2, NB, TM, width), lambda bp, j: (0, bp, first + j, 0))

    def const(arr):
        return _const_spec(arr, 2, single=True)

    ws = S5_GROUPS * S5_GROUP
    streams = [seq] if ctx is None else [seq, ctx]
    return pl.pallas_call(
        functools.partial(_merge_kernel, split=ctx is not None),
        grid=(b // NB, nt - first),
        in_specs=_stream_specs(d, ctx, first) + [_mod_spec(d, first), const(norm_w),
                  tok2(MLSTM_HEADS * HEAD_DIM), tok(W_O), const(mnw),
                  pl.BlockSpec((TM, NB * ws), lambda bp, j: (first + j, bp)),
                  pl.BlockSpec((TM, NB * ws), lambda bp, j: (first + j, bp)),
                  const(s5d), const(glu_w),
                  tok(NA_HEADS * HEAD_DIM),
                  tok2(SSD_HEADS * HEAD_DIM), tok(W_DZ), const(dnw),
                  const(wg), const(wa), const(wb), const(wc), const(wd), const(wo)],
        out_specs=tok(d),
        out_shape=jax.ShapeDtypeStruct((b, s, d), F32),
        input_output_aliases={0: 0} if ctx is None else {},
        compiler_params=pltpu.CompilerParams(vmem_limit_bytes=VMEM_LIMIT),
        name="merge",
    )(*streams, mod, norm_w, hm, mo, mnw, ys, su, s5d, glu_w, yna, yd, dz, dnw, wg, wa, wb, wc, wd, wo)


def _ffn_kernel(x_ref, mod_ref, nw_ref, wi_ref, wo_ref, fw_ref, o_ref, *, final):
    hidden = wo_ref.shape[0]
    h = _stack(lambda i: _norm_mod(x_ref[i], nw_ref[...], mod_ref[i, 4:5, :], mod_ref[i, 3:4, :]))
    h = h.astype(MXU_DTYPE)
    a = jnp.dot(h, wi_ref[:, :hidden], preferred_element_type=F32)
    g = jnp.dot(h, wi_ref[:, hidden:], preferred_element_type=F32)
    out = _mm(_silu(a) * g, wo_ref[...])
    for i in range(NB):
        y = x_ref[i] + mod_ref[i, 5:6, :] * out[i * TM:(i + 1) * TM, :]
        if final:
            y = y * lax.rsqrt(jnp.mean(y * y, axis=-1, keepdims=True) + EPS) * fw_ref[...]
        o_ref[i] = y


def _ffn(seq, mod, norm_w, wi, wo, final_w, final):
    b, s, d = seq.shape
    nt = s // TM
    first = 1 if final else 0

    def const(arr):
        return _const_spec(arr, 2, single=True)

    tok_in = pl.BlockSpec((NB, TM, d), lambda bp, j: (bp, first + j, 0))
    if final:
        out_spec = pl.BlockSpec((NB, TM, d), lambda bp, j: (bp, j, 0))
        out_shape = jax.ShapeDtypeStruct((b, s - TM, d), F32)
        aliases = {}
    else:
        out_spec = tok_in
        out_shape = jax.ShapeDtypeStruct((b, s, d), F32)
        aliases = {0: 0}
    return pl.pallas_call(
        functools.partial(_ffn_kernel, final=final),
        grid=(b // NB, nt - first),
        in_specs=[tok_in, _mod_spec(d, first), const(norm_w), const(wi), const(wo), const(final_w)],
        out_specs=out_spec,
        out_shape=out_shape,
        input_output_aliases=aliases,
        compiler_params=pltpu.CompilerParams(vmem_limit_bytes=VMEM_LIMIT),
        name="ffn_final" if final else "ffn",
    )(seq, mod, norm_w, wi, wo, final_w)


def _rope_tables(t, width):
    nf = HEAD_DIM // 4
    inv = (ROPE_THETA ** (-np.arange(nf, dtype=np.float32) / nf)).astype(np.float32)
    tok = np.arange(t)
    ang_r = (tok // GRID_W).astype(np.float32)[:, None] * inv
    ang_c = (tok % GRID_W).astype(np.float32)[:, None] * inv
    cos_h = np.concatenate([np.cos(ang_r)] * 2 + [np.cos(ang_c)] * 2, axis=1)
    sin_h = np.concatenate([-np.sin(ang_r), np.sin(ang_r), -np.sin(ang_c), np.sin(ang_c)], axis=1)
    cos_x = np.tile(cos_h, (1, width // HEAD_DIM))
    sin_x = np.tile(sin_h, (1, width // HEAD_DIM))
    cos_t = np.concatenate([np.ones((TM, width), np.float32), cos_x], axis=0)
    sin_t = np.concatenate([np.zeros((TM, width), np.float32), sin_x], axis=0)
    scale = np.float32(HEAD_DIM ** -0.5)
    return (jnp.asarray(np.concatenate([cos_t * scale, cos_t], axis=1), F32),
            jnp.asarray(np.concatenate([sin_t * scale, sin_t], axis=1), F32))


def _na_bias_tables(rpb):
    col = np.arange(GRID_W)
    col0 = np.clip(col - NA_KW // 2, 0, GRID_W - NA_KW)
    in_win = (col[None, :] >= col0[:, None]) & (col[None, :] < col0[:, None] + NA_KW)
    dc = np.clip(col[None, :] - col[:, None], -(NA_KW - 1), NA_KW - 1) + (NA_KW - 1)
    pick = jnp.asarray(dc[None] == np.arange(2 * NA_KW - 1)[:, None, None], F32)
    per_row = jnp.einsum('hrd,dqk->hrqk', rpb.astype(F32), pick, precision=HIGHEST)
    per_row = jnp.where(in_win, per_row, NEG)
    bias = jnp.stack([per_row[:, off:off + NA_KH] for off in range(NA_KH)], axis=0)
    return bias.transpose(0, 1, 3, 2, 4).reshape(NA_KH, NA_HEADS * GRID_W, NA_KH * GRID_W)


def _block_diag(blocks):
    g, r, c = blocks.shape
    eye = jnp.eye(g, dtype=blocks.dtype)
    return (eye[:, None, :, None] * blocks[:, :, None, :]).reshape(g * r, g * c)


def _lanes(vec, at, width=W_SMALL):
    return jnp.zeros((1, width), F32).at[0, at:at + vec.shape[0]].set(vec.astype(F32))


def _gate_perm():
    perm = np.zeros((2, 2 * W_SMALL, W_SMALL), np.float32)
    for d in range(2):
        for h in range(MLSTM_HEADS):
            perm[d, L_MI + d * MLSTM_HEADS + h, G_LI + h] = 1.0
            perm[d, L_MF + d * MLSTM_HEADS + h, G_BCUM + h] = 1.0
        for h in range(SSD_HEADS):
            perm[d, L_DDT + d * SSD_HEADS + h, G_ACS + h] = 1.0
            perm[d, W_SMALL + L_DDT + d * SSD_HEADS + h, G_DT + h] = 1.0
    return jnp.asarray(perm, MXU_DTYPE)


def _conv_shifts():
    t = np.arange(TM)[:, None]
    s = np.arange(TM)[None, :]
    blocks = [(s == t + k - CONV_K // 2) for k in range(CONV_K) if k != CONV_K // 2]
    return jnp.asarray(np.concatenate(blocks, axis=1), MXU_DTYPE)


def _log_decay_perms():
    pa = np.zeros((3, W_SMALL, W_SMALL), np.float32)
    pb = np.zeros((3, W_SMALL, W_SMALL), np.float32)
    ones_ab = np.zeros((2, W_SMALL), np.float32)
    slots = ([(G_BCUM + h, G_LI + h) for h in range(MLSTM_HEADS)]
             + [(G_ACS + h, G_ACS + h) for h in range(SSD_HEADS)])
    for n, (cum_lane, x_lane) in enumerate(slots):
        for k in range(3):
            pa[k, cum_lane, 8 * n + k] = 1.0
            pb[k, x_lane, 8 * n + 3 + k] = 1.0
            ones_ab[0, 8 * n + 3 + k] = 1.0
            ones_ab[1, 8 * n + k] = 1.0
    return jnp.asarray(pa, MXU_DTYPE), jnp.asarray(pb, MXU_DTYPE), jnp.asarray(ones_ab, F32)


def kernel(x, c, ctx, c_ctx, ada_w, ada_b, norm1_w, norm2_w, w_in, mlstm_conv_w, mlstm_conv_b, mlstm_ib, mlstm_fb, mlstm_norm_w, s5_lam_re, s5_lam_im, s5_log_dt, s5_b_re, s5_b_im, s5_c_re, s5_c_im, s5_d, s5_glu_w, na_rpb, ssd_conv_w, ssd_conv_b, ssd_a_log, ssd_dt_bias, ssd_d, ssd_norm_w, w_branch_a, w_branch_b, w_branch_c, w_branch_d, w_out, ffn_w_in, ffn_w_out, final_norm_w):
    b, t, d = x.shape
    depth = w_in.shape[0]
    assert ctx.shape[1] == TM and t % TM == 0 and t % GRID_W == 0 and b % 8 == 0 and b % NB == 0
    assert t // GRID_W >= NA_KH

    pad = (-(b + 1)) % 8
    cc = jnp.concatenate([c, c_ctx[None, :], jnp.zeros((pad, d), F32)], axis=0)
    mod_all = _adaln(cc, ada_w, ada_b)
    mod_x = mod_all[:, :b].reshape(depth, b, 1, 6, d)
    mod_c = jnp.broadcast_to(mod_all[:, b].reshape(depth, 1, 1, 6, d), (depth, b, 1, 6, d))
    mod = jnp.concatenate([mod_c, mod_x], axis=2)

    cos_t, sin_t = _rope_tables(t, MLSTM_HEADS * HEAD_DIM)
    perm = _gate_perm()
    pa, pb, ones_ab = _log_decay_perms()
    shifts = _conv_shifts()
    n_state = S5_GROUPS * S5_STATE

    seq = None
    for l in range(depth):
        with_ctx = l < depth - 1
        if l == 0:
            stream = (x, ctx) if with_ctx else (jnp.concatenate([ctx, x], axis=1), None)
        else:
            stream = (seq, None)
        wl = w_in[l].astype(MXU_DTYPE)
        w_small = jnp.concatenate([wl[:, 1024:1040], wl[:, 3600:3616], jnp.zeros((d, W_SMALL - 32), MXU_DTYPE)],
                                  axis=1)
        w_proj = jnp.concatenate([wl[:, 0:512], wl[:, 2576:3600], wl[:, 512:768], wl[:, 768:1024], w_small,
                                  wl[:, 1040:1296], wl[:, 1296:2064], wl[:, 2064:2576]], axis=1)
        w_gate = wl[:, 3616:]

        cv, v, mo, small, su, nqkv, dz = _proj(stream[0], mod[l], norm1_w[l][None, :], w_proj, ctx=stream[1])

        conv_w = jnp.concatenate([mlstm_conv_w[l], ssd_conv_w[l]], axis=1)
        conv_w = jnp.concatenate([conv_w, jnp.zeros((1, W_CV), F32)], axis=0)
        conv_b = jnp.concatenate([mlstm_conv_b[l], ssd_conv_b[l]])[None, :]
        gbias = (_lanes(mlstm_ib[l].reshape(-1), L_MI) + _lanes(mlstm_fb[l].reshape(-1), L_MF)
                 + _lanes(ssd_dt_bias[l].reshape(-1), L_DDT))
        qk, xbc, gc, gr, la, lb = _prep(cv, small, conv_w, conv_b, cos_t, sin_t, gbias,
                                        _lanes(ssd_a_log[l].reshape(-1), L_DDT), perm, pa, pb, ones_ab, shifts)

        hm = _mlstm(qk, v, gc, la, lb)

        braw_re = _block_diag(jnp.swapaxes(s5_b_re[l], 1, 2))
        braw_im = _block_diag(jnp.swapaxes(s5_b_im[l], 1, 2))
        cw = jnp.concatenate([_block_diag(jnp.swapaxes(s5_c_re[l], 1, 2)),
                              -_block_diag(jnp.swapaxes(s5_c_im[l], 1, 2))], axis=0).astype(MXU_DTYPE)
        ldt = jnp.repeat(s5_log_dt[l], S5_STATE, axis=1)
        ys = None
        for dd in range(2):
            ys = _s5(su.reshape(-1, W_SU), s5_lam_re[l, dd].reshape(1, n_state), s5_lam_im[l, dd].reshape(1, n_state),
                     ldt[dd].reshape(1, n_state), braw_re, braw_im, cw, b, fwd_out=ys)
        ys = ys.reshape(-1, b * W_SU)

        yna = _na(nqkv, _na_bias_tables(na_rpb[l]), with_ctx)

        yd = _ssd(xbc, gc, gr, la, lb, jnp.repeat(ssd_d[l], HEAD_DIM)[None, :])

        seq = _merge(stream[0], mod[l], norm1_w[l][None, :], hm, mo, mlstm_norm_w[l][None, :], ys, su,
                     s5_d[l][None, :], s5_glu_w[l].astype(MXU_DTYPE), yna, yd, dz, ssd_norm_w[l][None, :],
                     w_gate, w_branch_a[l].astype(MXU_DTYPE), w_branch_b[l].astype(MXU_DTYPE),
                     w_branch_c[l].astype(MXU_DTYPE), w_branch_d[l].astype(MXU_DTYPE), w_out[l].astype(MXU_DTYPE),
                     with_ctx, ctx=stream[1])
        seq = _ffn(seq, mod[l], norm2_w[l][None, :], ffn_w_in[l].astype(MXU_DTYPE), ffn_w_out[l].astype(MXU_DTYPE),
                   final_norm_w[None, :], not with_ctx)
    return seq
```

```python
import functools
import math

import jax
import jax.numpy as jnp
import numpy as np
from jax import lax
from jax.experimental import pallas as pl
from jax.experimental.pallas import tpu as pltpu

F32 = jnp.float32
MXU_DTYPE = jnp.bfloat16
HIGHEST = lax.Precision.HIGHEST

GRID_W = 64
EPS = 1e-6
CONV_K = 7
ROPE_THETA = 10000.0
HEAD_DIM = 64
MLSTM_HEADS = 4
S5_GROUPS = 16
S5_GROUP = 16
S5_STATE = 64
NA_HEADS = 4
NA_KH = 8
NA_KW = 16
SSD_HEADS = 8
SSD_GROUPS = 2
SSD_STATE = 128

TM = 256
NB = 2
NBS = 8
SUB = 128
HALO = 8
S5_SUB = 32
NEG = -1e30
VMEM_LIMIT = 56 * 1024 * 1024

W_QK, W_XBC, W_V, W_O, W_SMALL, W_SU, W_NQKV, W_DZ = 512, 1024, 256, 256, 128, 256, 768, 512
W_CV = W_QK + W_XBC
PROJ_WIDTHS = (W_CV, W_V, W_O, W_SMALL, W_SU, W_NQKV, W_DZ)
SU_POS = 4
L_MI, L_MF, L_DDT = 0, 8, 16
G_LI, G_BCUM, G_ACS, G_DT = 0, 4, 8, 16


def _mm(a, b):
    return jnp.dot(a.astype(MXU_DTYPE), b.astype(MXU_DTYPE), preferred_element_type=F32)


def _mm_nt(a, b):
    return lax.dot_general(a.astype(MXU_DTYPE), b.astype(MXU_DTYPE), (((1,), (1,)), ((), ())),
                           preferred_element_type=F32)


def _mm_tn(a, b):
    return lax.dot_general(a.astype(MXU_DTYPE), b.astype(MXU_DTYPE), (((0,), (0,)), ((), ())),
                           preferred_element_type=F32)


def _mm_f32(a, b):
    return jnp.dot(a, b, preferred_element_type=F32, precision=HIGHEST)


def _split3(x):
    hi = x.astype(MXU_DTYPE)
    r1 = x - hi.astype(F32)
    mid = r1.astype(MXU_DTYPE)
    lo = (r1 - mid.astype(F32)).astype(MXU_DTYPE)
    return hi, mid, lo


def _as_01(sel):
    if sel.dtype == jnp.bool_:
        sel = jnp.where(sel, 1.0, 0.0)
    return sel.astype(MXU_DTYPE)


def _mm_exact_rhs(x, sel, terms=3):
    sel = _as_01(sel)
    return sum(jnp.dot(p, sel, preferred_element_type=F32) for p in _split3(x)[:terms])


def _mm_exact_lhs(sel, x):
    sel = _as_01(sel)
    return sum(jnp.dot(sel, p, preferred_element_type=F32) for p in _split3(x))


def _sigmoid(x):
    return 1.0 / (1.0 + jnp.exp(-x))


def _silu(x):
    return x * _sigmoid(x)


def _softplus(x):
    return jnp.maximum(x, 0.0) + jnp.log(1.0 + jnp.exp(-jnp.abs(x)))


def _iota(shape, dim):
    return lax.broadcasted_iota(jnp.int32, shape, dim)


def _head_of_lane(width):
    return jnp.right_shift(_iota((1, width), 1), 6)


def _same_head(rows, cols):
    return jnp.right_shift(_iota((rows, cols), 0), 6) == jnp.right_shift(_iota((rows, cols), 1), 6)


def _stack(fn):
    return jnp.concatenate([fn(i) for i in range(NB)], axis=0)


def _round_robin(chains):
    live = list(chains)
    while live:
        for chain in list(live):
            try:
                next(chain)
            except StopIteration:
                live.remove(chain)


def _tile_of(d, j, nt):
    return jnp.where(d == 0, j, jnp.where(j == 0, 0, nt - j))


def _norm_mod(x, w, scale, shift):
    y = x * lax.rsqrt(jnp.mean(x * x, axis=-1, keepdims=True) + EPS) * w
    return y * (1.0 + scale) + shift


def _order_mask(rev, blocks, n):
    shape = (blocks * n, n)
    t_minus_s = (jnp.bitwise_and(_iota(shape, 0), n - 1) - _iota(shape, 1)) * jnp.where(rev, -1, 1)
    return t_minus_s >= 0


def _const_spec(arr, n_grid, single=False):
    kwargs = {"pipeline_mode": pl.Buffered(1)} if single else {}
    return pl.BlockSpec(arr.shape, lambda *_: (0,) * arr.ndim, **kwargs)


def _mod_spec(d, first):
    return pl.BlockSpec((NB, None, 6, d), lambda bp, j: (bp, jnp.minimum(first + j, 1), 0, 0))


def _adaln_kernel(c_ref, w_ref, b_ref, o_ref):
    o_ref[...] = _mm_f32(_silu(c_ref[...]), w_ref[...]) + b_ref[...]


def _adaln(cc, ada_w, ada_b):
    depth, d, n = ada_w.shape
    tn = 768
    return pl.pallas_call(
        _adaln_kernel,
        grid=(depth, n // tn),
        in_specs=[pl.BlockSpec(cc.shape, lambda l, i: (0, 0)),
                  pl.BlockSpec((None, d, tn), lambda l, i: (l, 0, i)),
                  pl.BlockSpec((None, 1, tn), lambda l, i: (l, 0, i))],
        out_specs=pl.BlockSpec((None, cc.shape[0], tn), lambda l, i: (l, 0, i)),
        out_shape=jax.ShapeDtypeStruct((depth, cc.shape[0], n), F32),
        name="adaln",
    )(cc, ada_w, ada_b.reshape(depth, 1, n))


def _stream_rows(x_ref, c_ref, i):
    if c_ref is None:
        return x_ref[i]
    return jnp.where(pl.program_id(1) == 0, c_ref[i], x_ref[i])


def _stream_specs(d, ctx, first):
    if ctx is None:
        return [pl.BlockSpec((NB, TM, d), lambda bp, j: (bp, first + j, 0))]
    return [pl.BlockSpec((NB, TM, d), lambda bp, j: (bp, jnp.maximum(j - 1, 0), 0)),
            pl.BlockSpec((NB, TM, d), lambda bp, j: (bp, 0, 0))]


def _proj_kernel(*refs, split):
    x_ref, c_ref = (refs[0], refs[1]) if split else (refs[0], None)
    mod_ref, nw_ref, w_ref, *out_refs = refs[2:] if split else refs[1:]
    h = _stack(lambda i: _norm_mod(_stream_rows(x_ref, c_ref, i), nw_ref[...], mod_ref[i, 1:2, :],
                                   mod_ref[i, 0:1, :]))
    h = h.astype(MXU_DTYPE)
    off = 0
    for pos, (ref, n) in enumerate(zip(out_refs, PROJ_WIDTHS)):
        res = jnp.dot(h, w_ref[:, off:off + n], preferred_element_type=F32)
        if pos == SU_POS:
            ref[...] = jnp.concatenate([res[i * TM:(i + 1) * TM, :] for i in range(NB)], axis=1)
        else:
            for i in range(NB):
                ref[i] = res[i * TM:(i + 1) * TM, :]
        off += n


def _proj(seq, mod, norm_w, w_proj, ctx=None):
    b, s, d = seq.shape
    s = s if ctx is None else s + TM
    nt = s // TM

    def tok(width):
        return pl.BlockSpec((NB, TM, width), lambda bp, j: (bp, j, 0))

    out_specs = [tok(w) for w in PROJ_WIDTHS]
    out_shape = [jax.ShapeDtypeStruct((b, s, w), F32) for w in PROJ_WIDTHS]
    out_specs[SU_POS] = pl.BlockSpec((TM, NB * W_SU), lambda bp, j: (j, bp))
    out_shape[SU_POS] = jax.ShapeDtypeStruct((s, b * W_SU), F32)
    streams = [seq] if ctx is None else [seq, ctx]
    return pl.pallas_call(
        functools.partial(_proj_kernel, split=ctx is not None),
        grid=(b // NB, nt),
        in_specs=_stream_specs(d, ctx, 0) + [_mod_spec(d, 0), _const_spec(norm_w, 2),
                                             _const_spec(w_proj, 2, single=True)],
        out_specs=out_specs,
        out_shape=out_shape,
        compiler_params=pltpu.CompilerParams(vmem_limit_bytes=VMEM_LIMIT),
        name="proj",
    )(*streams, mod, norm_w, w_proj)


def _prep_kernel(cv_ref, cvp_ref, cvn_ref, cw_ref, cb_ref, cos_ref, sin_ref, sm_ref, gb_ref, al_ref, perm_ref,
                 pa_ref, pb_ref, one_ref, shift_ref, qk_ref, xbc_ref, gc_ref, gr_ref, la_ref, lb_ref, *, nt):
    t = pl.program_id(1)
    prev_ok = t > 1
    next_ok = (t >= 1) & (t < nt - 1)
    lane = _iota((1, W_SMALL), 1)
    first = jnp.bitwise_and(_iota((1, W_QK), 1), 31) < 16
    r = _iota((TM, TM), 0)
    c = _iota((TM, TM), 1)
    same_chunk = jnp.right_shift(r, 7) == jnp.right_shift(c, 7)
    cum = (lane >= G_BCUM) & (lane < G_DT)

    mid = CONV_K // 2
    side_taps = [k for k in range(CONV_K) if k != mid]

    def edge_rows(ext24):
        out = cb_ref[...] + cw_ref[mid:mid + 1, :] * ext24[HALO:2 * HALO, :]
        for k in side_taps:
            out = out + cw_ref[k:k + 1, :] * pltpu.roll(ext24, (mid - k) % (3 * HALO), 0)[HALO:2 * HALO, :]
        return out

    for i in range(NB):
        cur = cv_ref[i]
        weighted = jnp.concatenate([(cw_ref[k:k + 1, :] * cur).astype(MXU_DTYPE) for k in side_taps], axis=0)
        acc = (cb_ref[...] + cw_ref[mid:mid + 1, :] * cur
               + jnp.dot(shift_ref[...], weighted, preferred_element_type=F32))
        top = edge_rows(jnp.concatenate([jnp.where(prev_ok, cvp_ref[i], 0.0), cur[0:2 * HALO, :]], axis=0))
        bot = edge_rows(jnp.concatenate([cur[TM - 2 * HALO:, :], jnp.where(next_ok, cvn_ref[i], 0.0)], axis=0))
        acc = _silu(jnp.concatenate([top, acc[HALO:TM - HALO, :], bot], axis=0))
        xbc_ref[i] = acc[:, W_QK:]
        qk = acc[:, :W_QK]
        partner = jnp.where(first, pltpu.roll(qk, W_QK - 16, 1), pltpu.roll(qk, 16, 1))
        qk_ref[i] = qk * cos_ref[...] + partner * sin_ref[...]

        g_all = sm_ref[i] + gb_ref[...]
        dt = _softplus(g_all)
        src = jnp.where(lane < L_MF, g_all,
                        jnp.where(lane < L_DDT, -_softplus(-g_all), dt * (-jnp.exp(al_ref[...]))))
        src = jnp.concatenate([src, dt], axis=1)
        for d in range(2):
            tri = same_chunk & ((c <= r) if d == 0 else (c >= r))
            g = _mm_exact_rhs(src, perm_ref[d])
            g = jnp.where(cum, _mm_exact_lhs(tri, jnp.where(cum, g, 0.0)), g)
            gc_ref[d, i] = g
            a_mat = one_ref[0:1, :] + sum(jnp.dot(p, pa_ref[k], preferred_element_type=F32)
                                           for k, p in enumerate(_split3(g)))
            cv = jnp.where(lane < MLSTM_HEADS, g - pltpu.roll(g, W_SMALL - G_BCUM, 1), -g)
            b_mat = one_ref[1:2, :] + sum(jnp.dot(p, pb_ref[k], preferred_element_type=F32)
                                           for k, p in enumerate(_split3(cv)))
            la_ref[d, i] = a_mat.astype(MXU_DTYPE)
            for ci in range(TM // SUB):
                gr_ref[d, i, ci] = g[ci * SUB:(ci + 1) * SUB, :].T
                lb_ref[d, i, ci] = b_mat[ci * SUB:(ci + 1) * SUB, :].T.astype(MXU_DTYPE)


def _prep(cv, small, conv_w, conv_b, cos_t, sin_t, gbias, alog, perm, pa, pb, ones_ab, shifts):
    b, s, _ = cv.shape
    nt = s // TM
    per = TM // HALO
    last = s // HALO - 1
    nsub = TM // SUB
    return pl.pallas_call(
        functools.partial(_prep_kernel, nt=nt),
        grid=(b // NB, nt),
        in_specs=[pl.BlockSpec((NB, TM, W_CV), lambda bp, j: (bp, j, 0)),
                  pl.BlockSpec((NB, HALO, W_CV), lambda bp, j: (bp, jnp.maximum(j * per - 1, 0), 0)),
                  pl.BlockSpec((NB, HALO, W_CV), lambda bp, j: (bp, jnp.minimum((j + 1) * per, last), 0)),
                  _const_spec(conv_w, 2), _const_spec(conv_b, 2),
                  pl.BlockSpec((TM, W_QK), lambda bp, j: (j, 0)),
                  pl.BlockSpec((TM, W_QK), lambda bp, j: (j, 0)),
                  pl.BlockSpec((NB, TM, W_SMALL), lambda bp, j: (bp, j, 0)),
                  _const_spec(gbias, 2), _const_spec(alog, 2), _const_spec(perm, 2),
                  _const_spec(pa, 2), _const_spec(pb, 2), _const_spec(ones_ab, 2), _const_spec(shifts, 2)],
        out_specs=[pl.BlockSpec((NB, TM, W_QK), lambda bp, j: (bp, j, 0)),
                   pl.BlockSpec((NB, TM, W_XBC), lambda bp, j: (bp, j, 0)),
                   pl.BlockSpec((2, NB, TM, W_SMALL), lambda bp, j: (0, bp, j, 0)),
                   pl.BlockSpec((2, NB, nsub, W_SMALL, SUB), lambda bp, j: (0, bp, j, 0, 0)),
                   pl.BlockSpec((2, NB, TM, W_SMALL), lambda bp, j: (0, bp, j, 0)),
                   pl.BlockSpec((2, NB, nsub, W_SMALL, SUB), lambda bp, j: (0, bp, j, 0, 0))],
        out_shape=[jax.ShapeDtypeStruct((b, s, W_QK), F32),
                   jax.ShapeDtypeStruct((b, s, W_XBC), F32),
                   jax.ShapeDtypeStruct((2, b, s, W_SMALL), F32),
                   jax.ShapeDtypeStruct((2, b, s // SUB, W_SMALL, SUB), F32),
                   jax.ShapeDtypeStruct((2, b, s, W_SMALL), MXU_DTYPE),
                   jax.ShapeDtypeStruct((2, b, s // SUB, W_SMALL, SUB), MXU_DTYPE)],
        compiler_params=pltpu.CompilerParams(vmem_limit_bytes=VMEM_LIMIT),
        name="prep",
    )(cv, cv, cv, conv_w, conv_b, cos_t, sin_t, small, gbias, alog, perm, pa, pb, ones_ab, shifts)


def _scan_specs(nt, widths, kinds):
    nsub = TM // SUB
    specs = [pl.BlockSpec((NBS, TM, w), lambda d, j, bp: (bp, _tile_of(d, j, nt), 0)) for w in widths]
    for kind in kinds:
        if kind == "col":
            specs.append(pl.BlockSpec((None, NBS, TM, W_SMALL), lambda d, j, bp: (d, bp, _tile_of(d, j, nt), 0)))
        else:
            specs.append(pl.BlockSpec((None, NBS, nsub, W_SMALL, SUB),
                                      lambda d, j, bp: (d, bp, _tile_of(d, j, nt), 0, 0)))
    return specs


def _mlstm_kernel(qk_ref, v_ref, gc_ref, la_ref, lb_ref, h_ref, c_scr, n_scr, m_scr):
    d = pl.program_id(0)
    j = pl.program_id(1)
    bp = pl.program_id(2)
    rev = d == 1
    width = MLSTM_HEADS * HEAD_DIM
    nsub = TM // SUB
    heads = list(range(MLSTM_HEADS))
    mask_s = _order_mask(rev, MLSTM_HEADS, SUB)
    head = _head_of_lane(width)
    same_head = _same_head(width, width)
    lane = _iota((1, W_SMALL), 1)
    valid = lane < MLSTM_HEADS
    group = jnp.right_shift(lane, 3)
    to_heads = _as_01(jnp.right_shift(_iota((W_SMALL, width), 1), 6) == _iota((W_SMALL, width), 0))
    from_heads = _as_01(jnp.right_shift(_iota((width, W_SMALL), 0), 6) == _iota((width, W_SMALL), 1))
    ones_rows = jnp.ones((8, SUB), MXU_DTYPE)

    @pl.when(j == 0)
    def _():
        for i in range(NBS):
            c_scr[bp * NBS + i] = jnp.zeros(c_scr.shape[1:], F32)
            n_scr[bp * NBS + i] = jnp.zeros(n_scr.shape[1:], F32)
            m_scr[bp * NBS + i] = jnp.zeros(m_scr.shape[1:], F32)

    def chain(i):
        bi = bp * NBS + i
        c_st = c_scr[bi]
        n_st = n_scr[bi, 0:1, :]
        m_st = m_scr[bi, 0:1, :]
        chunks = []
        for step in range(nsub):
            ci = jnp.where(rev, nsub - 1 - step, step)
            rows = pl.ds(pl.multiple_of(ci * SUB, SUB), SUB)
            q = qk_ref[i, rows, 0:width]
            k = qk_ref[i, rows, width:2 * width]
            qb = q.astype(MXU_DTYPE)
            kb = k.astype(MXU_DTYPE)
            vb = v_ref[i, rows, :].astype(MXU_DTYPE)
            li = gc_ref[i, rows, :]
            bc = pltpu.roll(li, W_SMALL - G_BCUM, 1)
            la = la_ref[i, rows, :]
            lb = lb_ref[i, ci]

            la_s = jnp.concatenate([jnp.where(group == h, la, 0) for h in heads], axis=0)
            tiles = [slice((h // 2) * 128, (h // 2 + 1) * 128) for h in heads]
            q_s = [jnp.where(head[:, tiles[h]] == h, qb[:, tiles[h]], 0) for h in heads]
            log_w = jnp.where(mask_s, jnp.dot(la_s, lb, preferred_element_type=F32), NEG)
            yield
            m_in_s = jnp.max(log_w, axis=1, keepdims=True)
            qk = jnp.concatenate([_mm_nt(jnp.concatenate(q_s[2 * p:2 * p + 2], axis=0), kb[:, tiles[2 * p]])
                                  for p in range(MLSTM_HEADS // 2)], axis=0)
            sm = qk * jnp.exp(log_w - m_in_s)
            rs = jnp.sum(sm, axis=1, keepdims=True)
            sm = sm.astype(MXU_DTYPE)
            yield
            pv = jnp.dot(sm, vb, preferred_element_type=F32)
            yield
            num0 = pv[0:SUB, :]
            den0 = jnp.broadcast_to(rs[0:SUB, :], (SUB, W_SMALL))
            m_in = jnp.broadcast_to(m_in_s[0:SUB, :], (SUB, W_SMALL))
            for h in heads[1:]:
                blk = slice(h * SUB, (h + 1) * SUB)
                num0 = jnp.where(head == h, pv[blk, :], num0)
                den0 = jnp.where(lane == h, rs[blk, :], den0)
                m_in = jnp.where(lane == h, m_in_s[blk, :], m_in)
            b_last = jnp.where(rev, bc[0:1, :], bc[SUB - 1:SUB, :])
            log_k = b_last - bc + li
            mk = jnp.max(log_k, axis=0, keepdims=True)
            wk_full = _mm_exact_rhs(jnp.where(valid, jnp.exp(log_k - mk), 0.0), to_heads, terms=2)
            kw = (k * wk_full).astype(MXU_DTYPE)
            yield
            kv0 = jnp.where(same_head, _mm_tn(kw, vb), 0.0)
            ks0 = jnp.dot(ones_rows, kw, preferred_element_type=F32)[0:1, :]
            chunks.append((rows, q, qb, bc, num0, den0, m_in, b_last, mk, kv0, ks0))
            yield

        for rows, q, qb, bc, num0, den0, m_in, b_last, mk, kv0, ks0 in chunks:
            inter = bc + m_st
            m_t = jnp.maximum(inter, m_in)
            r = jnp.exp(m_in - m_t)
            g = jnp.exp(inter - m_t)
            den = r * den0 + g * jnp.dot((q * n_st).astype(MXU_DTYPE), from_heads, preferred_element_type=F32)
            inv = 1.0 / jnp.maximum(jnp.abs(den), jnp.exp(-m_t))
            ir_full = _mm_exact_rhs(jnp.where(valid, inv * r, 0.0), to_heads, terms=2)
            ig_full = _mm_exact_rhs(jnp.where(valid, inv * g, 0.0), to_heads, terms=2)
            h_ref[i, rows, :] = num0 * ir_full + _mm(qb, c_st) * ig_full
            yield
            m_new = jnp.maximum(b_last + m_st, mk)
            scales = jnp.concatenate([jnp.exp(b_last + m_st - m_new), jnp.exp(mk - m_new),
                                      jnp.zeros((6, W_SMALL), F32)], axis=0)
            scales_full = _mm_exact_rhs(jnp.where(valid, scales, 0.0), to_heads)
            c_st = c_st * scales_full[0:1, :] + kv0 * scales_full[1:2, :]
            n_st = n_st * scales_full[0:1, :] + ks0 * scales_full[1:2, :]
            m_st = jnp.where(valid, m_new, 0.0)
            yield

        c_scr[bi] = c_st
        n_scr[bi, 0:1, :] = n_st
        m_scr[bi, 0:1, :] = m_st

    _round_robin([chain(i) for i in range(NBS)])


def _mlstm(qk, v, gc, la, lb):
    b, s, _ = qk.shape
    nt = s // TM
    width = MLSTM_HEADS * HEAD_DIM
    return pl.pallas_call(
        _mlstm_kernel,
        grid=(2, nt, b // NBS),
        in_specs=_scan_specs(nt, (2 * width, width), ("col", "col", "row")),
        out_specs=pl.BlockSpec((None, NBS, TM, width), lambda d, j, bp: (d, bp, _tile_of(d, j, nt), 0)),
        out_shape=jax.ShapeDtypeStruct((2, b, s, width), F32),
        scratch_shapes=[pltpu.VMEM((b, width, width), F32), pltpu.VMEM((b, 8, width), F32),
                        pltpu.VMEM((b, 8, W_SMALL), F32)],
        compiler_params=pltpu.CompilerParams(vmem_limit_bytes=VMEM_LIMIT),
        name="mlstm",
    )(qk, v, gc, la, lb)


def _s5_kernel(u_ref, lre_ref, lim_ref, ldt_ref, bre_ref, bim_ref, cw_ref, *rest, nb, rev):
    acc_ref = rest[0] if rev else None
    y_ref, wbu_scr, are_scr, aim_scr, hre_scr, him_scr, hb0, hb1, hb2 = rest[1:] if rev else rest
    j = pl.program_id(0)
    n_state = S5_GROUPS * S5_STATE

    @pl.when(j == 0)
    def _():
        lre = lre_ref[...]
        lim = lim_ref[...]
        dt = jnp.exp(ldt_ref[...])
        mag = jnp.exp(lre * dt)
        a_re = mag * jnp.cos(lim * dt)
        a_im = mag * jnp.sin(lim * dt)
        den = lre * lre + lim * lim
        nr = a_re - 1.0
        coef_re = (nr * lre + a_im * lim) / den
        coef_im = (a_im * lre - nr * lim) / den
        wbu_scr[:, :n_state] = (coef_re * bre_ref[...] - coef_im * bim_ref[...]).astype(MXU_DTYPE)
        wbu_scr[:, n_state:] = (coef_re * bim_ref[...] + coef_im * bre_ref[...]).astype(MXU_DTYPE)
        are_scr[...] = jnp.broadcast_to(a_re, are_scr.shape)
        aim_scr[...] = jnp.broadcast_to(a_im, aim_scr.shape)
        hre_scr[...] = jnp.zeros(hre_scr.shape, F32)
        him_scr[...] = jnp.zeros(him_scr.shape, F32)

    a_re = are_scr[...]
    a_im = aim_scr[...]
    rows_sub = S5_SUB * nb
    n_sub = TM // S5_SUB
    n_part = 1
    part = rows_sub // n_part
    bufs = (hb0, hb1, hb2)
    state = [hre_scr[...], him_scr[...]]

    def first_row(sb):
        return (n_sub - 1 - sb if rev else sb) * rows_sub

    def project(sb):
        for p in range(n_part):
            lhs = u_ref[first_row(sb) + p * part:first_row(sb) + (p + 1) * part, :].astype(MXU_DTYPE)
            bufs[sb % 3][p * part:(p + 1) * part, :] = jnp.dot(lhs, wbu_scr[...], preferred_element_type=F32)
            yield

    def recur(sb):
        buf = bufs[sb % 3]
        h_re, h_im = state
        for t in range(S5_SUB):
            ti = S5_SUB - 1 - t if rev else t
            r = slice(ti * nb, (ti + 1) * nb)
            n_re = a_re * h_re - a_im * h_im + buf[r, :n_state]
            n_im = a_re * h_im + a_im * h_re + buf[r, n_state:]
            buf[r, :n_state] = n_re
            buf[r, n_state:] = n_im
            h_re, h_im = n_re, n_im
            if t % 4 == 3:
                yield
        state[0], state[1] = h_re, h_im

    def readout(sb):
        for p in range(n_part):
            lhs = bufs[sb % 3][p * part:(p + 1) * part, :].astype(MXU_DTYPE)
            rows = slice(first_row(sb) + p * part, first_row(sb) + (p + 1) * part)
            y = jnp.dot(lhs, cw_ref[...], preferred_element_type=F32)
            y_ref[rows, :] = y + acc_ref[rows, :] if rev else y
            yield

    for stage in range(n_sub + 2):
        live = []
        if stage < n_sub:
            live.append(project(stage))
        if 1 <= stage <= n_sub:
            live.append(recur(stage - 1))
        if stage >= 2:
            live.append(readout(stage - 2))
        _round_robin(live)
    hre_scr[...] = state[0]
    him_scr[...] = state[1]


def _s5(u_tm, lam_re, lam_im, log_dt, braw_re, braw_im, cw, nb, fwd_out=None):
    rev = fwd_out is not None
    rows, width = u_tm.shape
    s = rows // nb
    nt = s // TM
    n_state = S5_GROUPS * S5_STATE
    d = 1 if rev else 0
    hbuf = pltpu.VMEM((S5_SUB * nb, 2 * n_state), F32)
    return pl.pallas_call(
        functools.partial(_s5_kernel, nb=nb, rev=rev),
        grid=(nt,),
        in_specs=[pl.BlockSpec((TM * nb, width), lambda j: (_tile_of(d, j, nt), 0)),
                  _const_spec(lam_re, 1), _const_spec(lam_im, 1), _const_spec(log_dt, 1),
                  _const_spec(braw_re, 1), _const_spec(braw_im, 1), _const_spec(cw, 1)]
        + ([pl.BlockSpec((TM * nb, width), lambda j: (_tile_of(d, j, nt), 0))] if rev else []),
        out_specs=pl.BlockSpec((TM * nb, width), lambda j: (_tile_of(d, j, nt), 0)),
        out_shape=jax.ShapeDtypeStruct((rows, width), F32),
        scratch_shapes=[pltpu.VMEM((width, 2 * n_state), MXU_DTYPE),
                        pltpu.VMEM((nb, n_state), F32), pltpu.VMEM((nb, n_state), F32),
                        pltpu.VMEM((nb, n_state), F32), pltpu.VMEM((nb, n_state), F32),
                        hbuf, hbuf, hbuf],
        compiler_params=pltpu.CompilerParams(vmem_limit_bytes=VMEM_LIMIT),
        name="s5_bwd" if rev else "s5_fwd",
    )(u_tm, lam_re, lam_im, log_dt, braw_re, braw_im, cw, *([fwd_out] if rev else []))


def _na_kernel(q_ref, k_ref, v_ref, tbl_ref, o_ref, *, first_tile, n_rows):
    t = first_tile + pl.program_id(1)
    width = NA_HEADS * HEAD_DIM
    head = _head_of_lane(width)
    scale = HEAD_DIM ** -0.5
    k_ctx = k_ref[0:TM, :].astype(MXU_DTYPE)
    v_ctx = v_ref[0:TM, :].astype(MXU_DTYPE)

    def stack_heads(q):
        return jnp.concatenate([jnp.where(head == h, q, 0.0) for h in range(NA_HEADS)], axis=0).astype(MXU_DTYPE)

    def unstack_heads(o, n):
        acc = o[0:n, :]
        for h in range(1, NA_HEADS):
            acc = jnp.where(head == h, o[h * n:(h + 1) * n, :], acc)
        return acc

    @pl.when(t == 0)
    def _():
        s = _mm_nt(stack_heads(q_ref[...] * scale), k_ctx)
        p = jnp.exp(s - jnp.max(s, axis=1, keepdims=True))
        o = _mm(p, v_ctx) / jnp.sum(p, axis=1, keepdims=True)
        o_ref[...] = unstack_heads(o, TM)

    @pl.when(t > 0)
    def _():
        rows_per_tile = TM // GRID_W
        n_lat = NA_KH * GRID_W
        def grid_row(rr):
            r = (t - 1) * rows_per_tile + rr
            row_start = jnp.clip(r - NA_KH // 2, 0, n_rows - NA_KH)
            off = (NA_KH - 1) - (r - row_start)
            win = pl.ds(pl.multiple_of(TM + row_start * GRID_W, GRID_W), n_lat)
            qs = stack_heads(q_ref[rr * GRID_W:(rr + 1) * GRID_W, :] * scale)
            s_lat = _mm_nt(qs, k_ref[win, :]) + tbl_ref[off]
            s_ctx = _mm_nt(qs, k_ctx)
            yield
            m = jnp.maximum(jnp.max(s_lat, axis=1, keepdims=True), jnp.max(s_ctx, axis=1, keepdims=True))
            p_lat = jnp.exp(s_lat - m)
            p_ctx = jnp.exp(s_ctx - m)
            den = jnp.sum(p_lat, axis=1, keepdims=True) + jnp.sum(p_ctx, axis=1, keepdims=True)
            yield
            o = (_mm(p_lat, v_ref[win, :]) + _mm(p_ctx, v_ctx)) / den
            o_ref[rr * GRID_W:(rr + 1) * GRID_W, :] = unstack_heads(o, GRID_W)
            yield

        _round_robin([grid_row(rr) for rr in range(rows_per_tile)])


def _na(nqkv, tbl, with_ctx):
    b, s, _ = nqkv.shape
    nt = s // TM
    width = NA_HEADS * HEAD_DIM
    first_tile = 0 if with_ctx else 1
    n_rows = (s - TM) // GRID_W
    return pl.pallas_call(
        functools.partial(_na_kernel, first_tile=first_tile, n_rows=n_rows),
        grid=(b, nt - first_tile),
        in_specs=[pl.BlockSpec((None, TM, width), lambda bi, j: (bi, first_tile + j, 0)),
                  pl.BlockSpec((None, s, width), lambda bi, j: (bi, 0, 1)),
                  pl.BlockSpec((None, s, width), lambda bi, j: (bi, 0, 2)),
                  _const_spec(tbl, 2)],
        out_specs=pl.BlockSpec((None, TM, width), lambda bi, j: (bi, first_tile + j, 0)),
        out_shape=jax.ShapeDtypeStruct((b, s, width), F32),
        compiler_params=pltpu.CompilerParams(vmem_limit_bytes=VMEM_LIMIT),
        name="na",
    )(nqkv, nqkv, nqkv, tbl)


def _ssd_kernel(x_ref, gc_ref, gr_ref, la_ref, lb_ref, dsk_ref, y_ref, st_scr):
    d = pl.program_id(0)
    j = pl.program_id(1)
    bp = pl.program_id(2)
    rev = d == 1
    width = SSD_HEADS * HEAD_DIM
    gn = SSD_GROUPS * SSD_STATE
    per_group = SSD_HEADS // SSD_GROUPS
    gw = per_group * HEAD_DIM
    nsub = TM // SUB
    mask_s = _order_mask(rev, per_group, SUB)
    head = _head_of_lane(width)
    skip = jnp.where(rev, 0.0, 1.0) * dsk_ref[...]
    lane = _iota((1, W_SMALL), 1)
    valid = (lane >= G_ACS) & (lane < G_ACS + SSD_HEADS)
    group = jnp.right_shift(lane, 3)
    to_heads = _as_01(jnp.right_shift(_iota((W_SMALL, width), 1), 6) == _iota((W_SMALL, width), 0) - G_ACS)

    @pl.when(j == 0)
    def _():
        for i in range(NBS):
            st_scr[bp * NBS + i] = jnp.zeros(st_scr.shape[1:], F32)

    def chain(i):
        bi = bp * NBS + i
        st = st_scr[bi]
        chunks = []
        for step in range(nsub):
            ci = jnp.where(rev, nsub - 1 - step, step)
            rows = pl.ds(pl.multiple_of(ci * SUB, SUB), SUB)
            xs = x_ref[i, rows, 0:width]
            bm = x_ref[i, rows, width:width + gn].astype(MXU_DTYPE)
            cm = x_ref[i, rows, width + gn:width + 2 * gn].astype(MXU_DTYPE)
            xsb = xs.astype(MXU_DTYPE)
            acs = gc_ref[i, rows, :]
            dt = pltpu.roll(acs, W_SMALL - (G_DT - G_ACS), 1)
            grow = gr_ref[i, ci]
            la = la_ref[i, rows, :]
            lb = lb_ref[i, ci]

            a_last = jnp.where(rev, acs[0:1, :], acs[SUB - 1:SUB, :])
            e_full = _mm_exact_rhs(jnp.where(valid, jnp.exp(acs), 0.0), to_heads, terms=2)
            xw = xs * _mm_exact_rhs(jnp.where(valid, jnp.exp(a_last - acs) * dt, 0.0), to_heads, terms=2)
            last = jnp.broadcast_to(jnp.where(valid, jnp.exp(a_last), 0.0), (8, W_SMALL))
            last_full = _mm_exact_rhs(last, to_heads)[0:1, :]
            yield

            y_parts, st_parts, c_parts = [], [], []
            for g in range(SSD_GROUPS):
                b_g = bm[:, g * SSD_STATE:(g + 1) * SSD_STATE]
                c_g = cm[:, g * SSD_STATE:(g + 1) * SSD_STATE]
                lanes = slice(g * gw, (g + 1) * gw)
                hs = range(g * per_group, (g + 1) * per_group)
                la_s = jnp.concatenate([jnp.where(group == MLSTM_HEADS + h, la, 0) for h in hs], axis=0)
                dt_s = jnp.concatenate([jnp.broadcast_to(grow[G_DT + h:G_DT + h + 1, :], (SUB, SUB)) for h in hs],
                                       axis=0)
                decay = jnp.exp(jnp.where(mask_s, jnp.dot(la_s, lb, preferred_element_type=F32), NEG)) * dt_s
                cb = _mm_nt(c_g, b_g)
                m_s = (jnp.concatenate([cb] * per_group, axis=0) * decay).astype(MXU_DTYPE)
                yield
                yd = jnp.dot(m_s, xsb[:, lanes], preferred_element_type=F32)
                y_g = yd[0:SUB, :]
                for hh in range(1, per_group):
                    y_g = jnp.where(head[:, lanes] == g * per_group + hh, yd[hh * SUB:(hh + 1) * SUB, :], y_g)
                y_parts.append(y_g)
                st_parts.append(_mm_tn(b_g, xw[:, lanes]))
                c_parts.append(c_g)
                yield
            chunks.append((rows, jnp.concatenate(y_parts, axis=1) + skip * xs, jnp.concatenate(st_parts, axis=1),
                           c_parts, e_full, last_full))

        for rows, y_in, st_in, c_parts, e_full, last_full in chunks:
            y_off = jnp.concatenate([_mm(c_parts[g], st[:, g * gw:(g + 1) * gw]) for g in range(SSD_GROUPS)], axis=1)
            y_ref[i, rows, :] = y_in + y_off * e_full
            st = st * last_full + st_in
            yield
        st_scr[bi] = st

    _round_robin([chain(i) for i in range(NBS)])


def _ssd(xbc, gc, gr, la, lb, dskip):
    b, s, cw = xbc.shape
    nt = s // TM
    width = SSD_HEADS * HEAD_DIM
    return pl.pallas_call(
        _ssd_kernel,
        grid=(2, nt, b // NBS),
        in_specs=_scan_specs(nt, (cw,), ("col", "row", "col", "row")) + [_const_spec(dskip, 3)],
        out_specs=pl.BlockSpec((None, NBS, TM, width), lambda d, j, bp: (d, bp, _tile_of(d, j, nt), 0)),
        out_shape=jax.ShapeDtypeStruct((2, b, s, width), F32),
        scratch_shapes=[pltpu.VMEM((b, SSD_STATE, width), F32)],
        compiler_params=pltpu.CompilerParams(vmem_limit_bytes=VMEM_LIMIT),
        name="ssd",
    )(xbc, gc, gr, la, lb, dskip)


def _merge_kernel(*refs, split):
    x_ref, c_ref = (refs[0], refs[1]) if split else (refs[0], None)
    (mod_ref, nw_ref, hm_ref, mo_ref, mnw_ref, ys_ref, su_ref, s5d_ref, glu_ref, na_ref, yd_ref, dz_ref, dnw_ref,
     wg_ref, wa_ref, wb_ref, wc_ref, wd_ref, wo_ref, o_ref) = refs[2:] if split else refs[1:]
    d_model = x_ref.shape[2]
    xs = [_stream_rows(x_ref, c_ref, i) for i in range(NB)]
    h = _stack(lambda i: _norm_mod(xs[i], nw_ref[...], mod_ref[i, 1:2, :], mod_ref[i, 0:1, :]))
    h = h.astype(MXU_DTYPE)

    wm = MLSTM_HEADS * HEAD_DIM
    hm = _stack(lambda i: (hm_ref[0, i] + hm_ref[1, i]) * _sigmoid(mo_ref[i]))
    ms = _mm_exact_rhs(hm * hm, _same_head(wm, wm)) * (1.0 / HEAD_DIM)
    ya = hm * lax.rsqrt(ms + EPS) * mnw_ref[...]

    ws = S5_GROUPS * S5_GROUP
    ys = _stack(lambda i: (ys_ref[:, i * ws:(i + 1) * ws]
                           + s5d_ref[...] * su_ref[:, i * ws:(i + 1) * ws]))
    ys = 0.5 * ys * (1.0 + jnp.tanh(math.sqrt(2.0 / math.pi) * (ys + 0.044715 * (ys * ys * ys))))
    ab = _mm(ys, glu_ref[...])
    yb = ab[:, :ws] * _sigmoid(ab[:, ws:])

    yc = _stack(lambda i: na_ref[i])

    yd = _stack(lambda i: (yd_ref[0, i] + yd_ref[1, i]) * _silu(dz_ref[i]))
    yd = yd * lax.rsqrt(jnp.mean(yd * yd, axis=-1, keepdims=True) + EPS) * dnw_ref[...]

    m = None
    for k, (y, w_ref) in enumerate(((ya, wa_ref), (yb, wb_ref), (yc, wc_ref), (yd, wd_ref))):
        gate = _sigmoid(jnp.dot(h, wg_ref[:, k * d_model:(k + 1) * d_model], preferred_element_type=F32))
        term = gate * _mm(y, w_ref[...])
        m = term if m is None else m + term
    out = _mm(m, wo_ref[...])
    for i in range(NB):
        o_ref[i] = xs[i] + mod_ref[i, 2:3, :] * out[i * TM:(i + 1) * TM, :]


def _merge(seq, mod, norm_w, hm, mo, mnw, ys, su, s5d, glu_w, yna, yd, dz, dnw, wg, wa, wb, wc, wd, wo,
           with_ctx, ctx=None):
    b, s, d = seq.shape
    s = s if ctx is None else s + TM
    nt = s // TM
    first = 0 if with_ctx else 1
    assert ctx is None or with_ctx

    def tok(width):
        return pl.BlockSpec((NB, TM, width), lambda bp, j: (bp, first + j, 0))

    def tok2(width):
        return pl.BlockSpec((2, NB, TM, width), lambda bp, j: (0, bp, first + j, 0))

    def const(arr):
        return _const_spec(arr, 2, single=True)

    ws = S5_GROUPS * S5_GROUP
    streams = [seq] if ctx is None else [seq, ctx]
    return pl.pallas_call(
        functools.partial(_merge_kernel, split=ctx is not None),
        grid=(b // NB, nt - first),
        in_specs=_stream_specs(d, ctx, first) + [_mod_spec(d, first), const(norm_w),
                  tok2(MLSTM_HEADS * HEAD_DIM), tok(W_O), const(mnw),
                  pl.BlockSpec((TM, NB * ws), lambda bp, j: (first + j, bp)),
                  pl.BlockSpec((TM, NB * ws), lambda bp, j: (first + j, bp)),
                  const(s5d), const(glu_w),
                  tok(NA_HEADS * HEAD_DIM),
                  tok2(SSD_HEADS * HEAD_DIM), tok(W_DZ), const(dnw),
                  const(wg), const(wa), const(wb), const(wc), const(wd), const(wo)],
        out_specs=tok(d),
        out_shape=jax.ShapeDtypeStruct((b, s, d), F32),
        input_output_aliases={0: 0} if ctx is None else {},
        compiler_params=pltpu.CompilerParams(vmem_limit_bytes=VMEM_LIMIT),
        name="merge",
    )(*streams, mod, norm_w, hm, mo, mnw, ys, su, s5d, glu_w, yna, yd, dz, dnw, wg, wa, wb, wc, wd, wo)


def _ffn_kernel(x_ref, mod_ref, nw_ref, wi_ref, wo_ref, fw_ref, o_ref, *, final):
    hidden = wo_ref.shape[0]
    h = _stack(lambda i: _norm_mod(x_ref[i], nw_ref[...], mod_ref[i, 4:5, :], mod_ref[i, 3:4, :]))
    h = h.astype(MXU_DTYPE)
    a = jnp.dot(h, wi_ref[:, :hidden], preferred_element_type=F32)
    g = jnp.dot(h, wi_ref[:, hidden:], preferred_element_type=F32)
    out = _mm(_silu(a) * g, wo_ref[...])
    for i in range(NB):
        y = x_ref[i] + mod_ref[i, 5:6, :] * out[i * TM:(i + 1) * TM, :]
        if final:
            y = y * lax.rsqrt(jnp.mean(y * y, axis=-1, keepdims=True) + EPS) * fw_ref[...]
        o_ref[i] = y


def _ffn(seq, mod, norm_w, wi, wo, final_w, final):
    b, s, d = seq.shape
    nt = s // TM
    first = 1 if final else 0

    def const(arr):
        return _const_spec(arr, 2, single=True)

    tok_in = pl.BlockSpec((NB, TM, d), lambda bp, j: (bp, first + j, 0))
    if final:
        out_spec = pl.BlockSpec((NB, TM, d), lambda bp, j: (bp, j, 0))
        out_shape = jax.ShapeDtypeStruct((b, s - TM, d), F32)
        aliases = {}
    else:
        out_spec = tok_in
        out_shape = jax.ShapeDtypeStruct((b, s, d), F32)
        aliases = {0: 0}
    return pl.pallas_call(
        functools.partial(_ffn_kernel, final=final),
        grid=(b // NB, nt - first),
        in_specs=[tok_in, _mod_spec(d, first), const(norm_w), const(wi), const(wo), const(final_w)],
        out_specs=out_spec,
        out_shape=out_shape,
        input_output_aliases=aliases,
        compiler_params=pltpu.CompilerParams(vmem_limit_bytes=VMEM_LIMIT),
        name="ffn_final" if final else "ffn",
    )(seq, mod, norm_w, wi, wo, final_w)


def _rope_tables(t, width):
    nf = HEAD_DIM // 4
    inv = (ROPE_THETA ** (-np.arange(nf, dtype=np.float32) / nf)).astype(np.float32)
    tok = np.arange(t)
    ang_r = (tok // GRID_W).astype(np.float32)[:, None] * inv
    ang_c = (tok % GRID_W).astype(np.float32)[:, None] * inv
    cos_h = np.concatenate([np.cos(ang_r)] * 2 + [np.cos(ang_c)] * 2, axis=1)
    sin_h = np.concatenate([-np.sin(ang_r), np.sin(ang_r), -np.sin(ang_c), np.sin(ang_c)], axis=1)
    cos_x = np.tile(cos_h, (1, width // HEAD_DIM))
    sin_x = np.tile(sin_h, (1, width // HEAD_DIM))
    cos_t = np.concatenate([np.ones((TM, width), np.float32), cos_x], axis=0)
    sin_t = np.concatenate([np.zeros((TM, width), np.float32), sin_x], axis=0)
    scale = np.float32(HEAD_DIM ** -0.5)
    return (jnp.asarray(np.concatenate([cos_t * scale, cos_t], axis=1), F32),
            jnp.asarray(np.concatenate([sin_t * scale, sin_t], axis=1), F32))


def _na_bias_tables(rpb):
    col = np.arange(GRID_W)
    col0 = np.clip(col - NA_KW // 2, 0, GRID_W - NA_KW)
    in_win = (col[None, :] >= col0[:, None]) & (col[None, :] < col0[:, None] + NA_KW)
    dc = np.clip(col[None, :] - col[:, None], -(NA_KW - 1), NA_KW - 1) + (NA_KW - 1)
    pick = jnp.asarray(dc[None] == np.arange(2 * NA_KW - 1)[:, None, None], F32)
    per_row = jnp.einsum('hrd,dqk->hrqk', rpb.astype(F32), pick, precision=HIGHEST)
    per_row = jnp.where(in_win, per_row, NEG)
    bias = jnp.stack([per_row[:, off:off + NA_KH] for off in range(NA_KH)], axis=0)
    return bias.transpose(0, 1, 3, 2, 4).reshape(NA_KH, NA_HEADS * GRID_W, NA_KH * GRID_W)


def _block_diag(blocks):
    g, r, c = blocks.shape
    eye = jnp.eye(g, dtype=blocks.dtype)
    return (eye[:, None, :, None] * blocks[:, :, None, :]).reshape(g * r, g * c)


def _lanes(vec, at, width=W_SMALL):
    return jnp.zeros((1, width), F32).at[0, at:at + vec.shape[0]].set(vec.astype(F32))


def _gate_perm():
    perm = np.zeros((2, 2 * W_SMALL, W_SMALL), np.float32)
    for d in range(2):
        for h in range(MLSTM_HEADS):
            perm[d, L_MI + d * MLSTM_HEADS + h, G_LI + h] = 1.0
            perm[d, L_MF + d * MLSTM_HEADS + h, G_BCUM + h] = 1.0
        for h in range(SSD_HEADS):
            perm[d, L_DDT + d * SSD_HEADS + h, G_ACS + h] = 1.0
            perm[d, W_SMALL + L_DDT + d * SSD_HEADS + h, G_DT + h] = 1.0
    return jnp.asarray(perm, MXU_DTYPE)


def _conv_shifts():
    t = np.arange(TM)[:, None]
    s = np.arange(TM)[None, :]
    blocks = [(s == t + k - CONV_K // 2) for k in range(CONV_K) if k != CONV_K // 2]
    return jnp.asarray(np.concatenate(blocks, axis=1), MXU_DTYPE)


def _log_decay_perms():
    pa = np.zeros((3, W_SMALL, W_SMALL), np.float32)
    pb = np.zeros((3, W_SMALL, W_SMALL), np.float32)
    ones_ab = np.zeros((2, W_SMALL), np.float32)
    slots = ([(G_BCUM + h, G_LI + h) for h in range(MLSTM_HEADS)]
             + [(G_ACS + h, G_ACS + h) for h in range(SSD_HEADS)])
    for n, (cum_lane, x_lane) in enumerate(slots):
        for k in range(3):
            pa[k, cum_lane, 8 * n + k] = 1.0
            pb[k, x_lane, 8 * n + 3 + k] = 1.0
            ones_ab[0, 8 * n + 3 + k] = 1.0
            ones_ab[1, 8 * n + k] = 1.0
    return jnp.asarray(pa, MXU_DTYPE), jnp.asarray(pb, MXU_DTYPE), jnp.asarray(ones_ab, F32)


def kernel(x, c, ctx, c_ctx, ada_w, ada_b, norm1_w, norm2_w, w_in, mlstm_conv_w, mlstm_conv_b, mlstm_ib, mlstm_fb, mlstm_norm_w, s5_lam_re, s5_lam_im, s5_log_dt, s5_b_re, s5_b_im, s5_c_re, s5_c_im, s5_d, s5_glu_w, na_rpb, ssd_conv_w, ssd_conv_b, ssd_a_log, ssd_dt_bias, ssd_d, ssd_norm_w, w_branch_a, w_branch_b, w_branch_c, w_branch_d, w_out, ffn_w_in, ffn_w_out, final_norm_w):
    b, t, d = x.shape
    depth = w_in.shape[0]
    assert ctx.shape[1] == TM and t % TM == 0 and t % GRID_W == 0 and b % 8 == 0 and b % NB == 0
    assert t // GRID_W >= NA_KH

    pad = (-(b + 1)) % 8
    cc = jnp.concatenate([c, c_ctx[None, :], jnp.zeros((pad, d), F32)], axis=0)
    mod_all = _adaln(cc, ada_w, ada_b)
    mod_x = mod_all[:, :b].reshape(depth, b, 1, 6, d)
    mod_c = jnp.broadcast_to(mod_all[:, b].reshape(depth, 1, 1, 6, d), (depth, b, 1, 6, d))
    mod = jnp.concatenate([mod_c, mod_x], axis=2)

    cos_t, sin_t = _rope_tables(t, MLSTM_HEADS * HEAD_DIM)
    perm = _gate_perm()
    pa, pb, ones_ab = _log_decay_perms()
    shifts = _conv_shifts()
    n_state = S5_GROUPS * S5_STATE

    seq = None
    for l in range(depth):
        with_ctx = l < depth - 1
        if l == 0:
            stream = (x, ctx) if with_ctx else (jnp.concatenate([ctx, x], axis=1), None)
        else:
            stream = (seq, None)
        wl = w_in[l].astype(MXU_DTYPE)
        w_small = jnp.concatenate([wl[:, 1024:1040], wl[:, 3600:3616], jnp.zeros((d, W_SMALL - 32), MXU_DTYPE)],
                                  axis=1)
        w_proj = jnp.concatenate([wl[:, 0:512], wl[:, 2576:3600], wl[:, 512:768], wl[:, 768:1024], w_small,
                                  wl[:, 1040:1296], wl[:, 1296:2064], wl[:, 2064:2576]], axis=1)
        w_gate = wl[:, 3616:]

        cv, v, mo, small, su, nqkv, dz = _proj(stream[0], mod[l], norm1_w[l][None, :], w_proj, ctx=stream[1])

        conv_w = jnp.concatenate([mlstm_conv_w[l], ssd_conv_w[l]], axis=1)
        conv_w = jnp.concatenate([conv_w, jnp.zeros((1, W_CV), F32)], axis=0)
        conv_b = jnp.concatenate([mlstm_conv_b[l], ssd_conv_b[l]])[None, :]
        gbias = (_lanes(mlstm_ib[l].reshape(-1), L_MI) + _lanes(mlstm_fb[l].reshape(-1), L_MF)
                 + _lanes(ssd_dt_bias[l].reshape(-1), L_DDT))
        qk, xbc, gc, gr, la, lb = _prep(cv, small, conv_w, conv_b, cos_t, sin_t, gbias,
                                        _lanes(ssd_a_log[l].reshape(-1), L_DDT), perm, pa, pb, ones_ab, shifts)

        hm = _mlstm(qk, v, gc, la, lb)

        braw_re = _block_diag(jnp.swapaxes(s5_b_re[l], 1, 2))
        braw_im = _block_diag(jnp.swapaxes(s5_b_im[l], 1, 2))
        cw = jnp.concatenate([_block_diag(jnp.swapaxes(s5_c_re[l], 1, 2)),
                              -_block_diag(jnp.swapaxes(s5_c_im[l], 1, 2))], axis=0).astype(MXU_DTYPE)
        ldt = jnp.repeat(s5_log_dt[l], S5_STATE, axis=1)
        ys = None
        for dd in range(2):
            ys = _s5(su.reshape(-1, W_SU), s5_lam_re[l, dd].reshape(1, n_state), s5_lam_im[l, dd].reshape(1, n_state),
                     ldt[dd].reshape(1, n_state), braw_re, braw_im, cw, b, fwd_out=ys)
        ys = ys.reshape(-1, b * W_SU)

        yna = _na(nqkv, _na_bias_tables(na_rpb[l]), with_ctx)

        yd = _ssd(xbc, gc, gr, la, lb, jnp.repeat(ssd_d[l], HEAD_DIM)[None, :])

        seq = _merge(stream[0], mod[l], norm1_w[l][None, :], hm, mo, mlstm_norm_w[l][None, :], ys, su,
                     s5_d[l][None, :], s5_glu_w[l].astype(MXU_DTYPE), yna, yd, dz, ssd_norm_w[l][None, :],
                     w_gate, w_branch_a[l].astype(MXU_DTYPE), w_branch_b[l].astype(MXU_DTYPE),
                     w_branch_c[l].astype(MXU_DTYPE), w_branch_d[l].astype(MXU_DTYPE), w_out[l].astype(MXU_DTYPE),
                     with_ctx, ctx=stream[1])
        seq = _ffn(seq, mod[l], norm2_w[l][None, :], ffn_w_in[l].astype(MXU_DTYPE), ffn_w_out[l].astype(MXU_DTYPE),
                   final_norm_w[None, :], not with_ctx)
    return seq
```
